```python
import math
import jax, jax.numpy as jnp
from jax import lax
import numpy as np

D_MODEL = 1024
BATCH = 8
SEQ = 2048
DEPTH = 1
DEC_BATCH = 128
DEC_SEQ = 1
PAST_LEN = 2048
PAGE_SIZE = 128

H_A = 4
DK_A = 128
DV_A = 128
W_A = H_A * DV_A
HGRN_CHUNK = 64
H_B = 4
DK_B = 64
DV_B = 2 * DK_B
QK_B = H_B * 2 * DK_B
W_B = H_B * DV_B
Q_BLOCK = 128
NEG_INF = -1e30
N_EXPERTS = 32
TOP_K = 4
D_FF = D_MODEL
SWIGLU_LIMIT = 7.0
SWIGLU_ALPHA = 1.702
MOE_BLOCK = 128
EPS = 1e-6
IN_SPLITS = (W_A, 2 * W_A, 3 * W_A, 4 * W_A, 4 * W_A + QK_B, 4 * W_A + 2 * QK_B,
             4 * W_A + 2 * QK_B + W_B, 4 * W_A + 2 * QK_B + W_B + D_MODEL)
D_IN = 4 * W_A + 2 * QK_B + W_B + 2 * D_MODEL

kernel_name = 'hgrn2_diffattn_moe_decoder_step'


def _rmsnorm(x, g):
    xf = x.astype(jnp.float32)
    y = xf * lax.rsqrt(jnp.mean(xf * xf, axis=-1, keepdims=True) + EPS)
    return (y * g.astype(jnp.float32)).astype(x.dtype)


def _modulation(c, w_mod, b_mod):
    m = jax.nn.silu(c) @ w_mod + b_mod
    return jnp.split(m[:, None, :], 6, axis=-1)


def _hgrn2(aq, af, ai, ag, s0, lb, g_norm):
    B, L, _ = aq.shape
    f32 = jnp.float32
    lb = lb.reshape(H_A, DK_A).astype(f32)
    q = aq.reshape(B, L, H_A, DK_A).astype(f32)
    f = lb + (1.0 - lb) * jax.nn.sigmoid(af.reshape(B, L, H_A, DK_A).astype(f32))
    k = 1.0 - f
    v = ai.reshape(B, L, H_A, DV_A).astype(f32)
    C = math.gcd(L, HGRN_CHUNK)
    NC = L // C
    q, k, v = (t.reshape(B, NC, C, H_A, t.shape[-1]) for t in (q, k, v))
    gc = jnp.cumsum(jnp.log(f).reshape(B, NC, C, H_A, DK_A), axis=2)
    q_dec = q * jnp.exp(gc)
    k_dec = k * jnp.exp(-gc)
    causal = jnp.tril(jnp.ones((C, C), dtype=bool))
    a = jnp.where(causal, jnp.einsum('bntha,bnsha->bnhts', q_dec, k_dec), 0.0)
    o_intra = jnp.einsum('bnhts,bnshv->bnthv', a, v)
    g_last = gc[:, :, -1]
    kv = jnp.einsum('bnsha,bnshv->bnhav', k * jnp.exp(g_last[:, :, None] - gc), v)

    def step(S, xs):
        qd, gl, kv_c = xs
        o = jnp.einsum('btha,bhav->bthv', qd, S)
        return jnp.exp(gl)[..., None] * S + kv_c, o

    s_fin, o_inter = lax.scan(step, s0.astype(f32),
                              (jnp.moveaxis(q_dec, 1, 0), jnp.moveaxis(g_last, 1, 0), jnp.moveaxis(kv, 1, 0)))
    o = (o_intra + jnp.moveaxis(o_inter, 0, 1)).reshape(B, L, H_A, DV_A)
    o = _rmsnorm(o, g_norm).reshape(B, L, W_A) * jax.nn.silu(ag.astype(f32))
    return o.astype(aq.dtype), s_fin


def _alibi_slopes():
    return jnp.exp2(-8.0 * jnp.arange(1, H_B + 1, dtype=jnp.float32) / H_B)


def _diff_scores(q, k, v, q_pos, k_pos, lam):
    s = jnp.einsum('bqhmd,bkhmd->bhmqk', q, k).astype(jnp.float32) * (DK_B ** -0.5)
    dist = (q_pos[:, None] - k_pos[None, :]).astype(jnp.float32)
    s = s - _alibi_slopes()[None, :, None, None, None] * dist
    s = jnp.where(dist >= 0, s, NEG_INF)
    p = jax.nn.softmax(s, axis=-1)
    p = p[:, :, 0] - lam * p[:, :, 1]
    return jnp.einsum('bhqk,bkhv->bqhv', p, v.astype(jnp.float32))


def _prompt_diff_attention(q, k, v, lam):
    B, L = q.shape[0], q.shape[1]
    nb = L // Q_BLOCK
    pos = jnp.arange(L, dtype=jnp.int32)
    kk = k.reshape(B, L, H_B, 2, DK_B)
    qb = jnp.moveaxis(q.reshape(B, nb, Q_BLOCK, H_B, 2, DK_B), 1, 0)
    pb = pos.reshape(nb, Q_BLOCK)
    ob = lax.map(lambda a: _diff_scores(a[0], kk, v, a[1], pos, lam), (qb, pb))
    return jnp.moveaxis(ob, 0, 1).reshape(B, L, H_B, DV_B)


def _clamped_swiglu(xb, wgu, bgu, wd, bd):
    gu = xb @ wgu + bgu
    glu, lin = gu[:, :D_FF], gu[:, D_FF:]
    glu = jnp.minimum(glu, SWIGLU_LIMIT)
    lin = jnp.clip(lin, -SWIGLU_LIMIT, SWIGLU_LIMIT)
    return ((lin + 1.0) * glu * jax.nn.sigmoid(SWIGLU_ALPHA * glu)) @ wd + bd


def _moe(h, w_router, b_router, w_gate_up, b_gate_up, w_down, b_down):
    B, L, D = h.shape
    T = B * L
    xt = h.reshape(T, D)
    logits = (xt @ w_router + b_router).astype(jnp.float32)
    top_val, top_idx = lax.top_k(logits, TOP_K)
    gates = jax.nn.softmax(top_val, axis=-1)
    n_assign = T * TOP_K
    e_flat = top_idx.reshape(-1)
    tok_flat = jnp.arange(n_assign, dtype=jnp.int32) // TOP_K
    order = jnp.argsort(e_flat)
    e_sorted = e_flat[order]
    tok_sorted = tok_flat[order]
    gate_sorted = gates.reshape(-1)[order]
    counts = jnp.bincount(e_flat, length=N_EXPERTS)
    padded = (counts + MOE_BLOCK - 1) // MOE_BLOCK * MOE_BLOCK
    pad_end = jnp.cumsum(padded)
    pad_start = pad_end - padded
    start = jnp.cumsum(counts) - counts
    dest = pad_start[e_sorted] + jnp.arange(n_assign, dtype=jnp.int32) - start[e_sorted]
    n_blocks = -(-n_assign // MOE_BLOCK) + N_EXPERTS
    x_buf = jnp.zeros((n_blocks * MOE_BLOCK, D), xt.dtype).at[dest].set(xt[tok_sorted])
    block_expert = jnp.minimum(
        jnp.searchsorted(pad_end, jnp.arange(n_blocks, dtype=pad_end.dtype) * MOE_BLOCK, side='right'),
        N_EXPERTS - 1)

    def expert_block(a):
        xb, e = a
        return _clamped_swiglu(xb, w_gate_up[e], b_gate_up[e], w_down[e], b_down[e])

    y_buf = lax.map(expert_block, (x_buf.reshape(n_blocks, MOE_BLOCK, D), block_expert)).reshape(-1, D)
    y = jax.ops.segment_sum(y_buf[dest] * gate_sorted[:, None].astype(y_buf.dtype), tok_sorted,
                            num_segments=T)
    return y.reshape(B, L, D).astype(h.dtype)


def _trunk_layer(x, c, s_hgrn0, k_cache, v_cache, page_table, l, lb,
                 w_mod, b_mod, g_pre_mix, g_post_mix, g_pre_ffn, g_post_ffn,
                 w_in, g_hgrn, lam_q1, lam_k1, lam_q2, lam_k2, g_subln,
                 w_proj_a, w_proj_b, w_out,
                 w_router, b_router, w_gate_up, b_gate_up, w_down, b_down):
    B, L, _ = x.shape
    sh1, sc1, gt1, sh2, sc2, gt2 = _modulation(c, w_mod, b_mod)
    h = _rmsnorm(x, g_pre_mix) * (1.0 + sc1) + sh1
    aq, af, ai, ag, bq, bk, bv, gate_a, gate_b = jnp.split(h @ w_in, IN_SPLITS, axis=-1)
    o_a, s_new = _hgrn2(aq, af, ai, ag, s_hgrn0, lb, g_hgrn)
    lam_init = 0.8 - 0.6 * math.exp(-0.3 * l)
    lam = (jnp.exp(jnp.sum(lam_q1.astype(jnp.float32) * lam_k1.astype(jnp.float32)))
           - jnp.exp(jnp.sum(lam_q2.astype(jnp.float32) * lam_k2.astype(jnp.float32))) + lam_init)
    q = bq.reshape(B, L, H_B, 2, DK_B)
    k_new = bk.reshape(B, L, H_B, 2 * DK_B)
    v_new = bv.reshape(B, L, H_B, DV_B)
    if k_cache is None:
        o_b = _prompt_diff_attention(q, k_new, v_new, lam)
    else:
        past = page_table.shape[1] * k_cache.shape[1]
        k_past = k_cache[page_table].reshape(B, past, H_B, 2 * DK_B)
        v_past = v_cache[page_table].reshape(B, past, H_B, DV_B)
        k_all = jnp.concatenate([k_past, k_new.astype(k_past.dtype)], axis=1)
        v_all = jnp.concatenate([v_past, v_new.astype(v_past.dtype)], axis=1)
        q_pos = past + jnp.arange(L, dtype=jnp.int32)
        k_pos = jnp.arange(past + L, dtype=jnp.int32)
        o_b = _diff_scores(q, k_all.reshape(B, past + L, H_B, 2, DK_B), v_all, q_pos, k_pos, lam)
    o_b = (_rmsnorm(o_b, g_subln) * (1.0 - lam_init)).reshape(B, L, W_B).astype(h.dtype)
    merged = jax.nn.sigmoid(gate_a) * (o_a @ w_proj_a) + jax.nn.sigmoid(gate_b) * (o_b @ w_proj_b)
    x = x + gt1 * _rmsnorm(merged @ w_out, g_post_mix)
    h2 = _rmsnorm(x, g_pre_ffn) * (1.0 + sc2) + sh2
    x = x + gt2 * _rmsnorm(_moe(h2, w_router, b_router, w_gate_up, b_gate_up, w_down, b_down), g_post_ffn)
    return x, s_new, k_new, v_new


def setup_inputs(seed: int = 0) -> dict:
    key = jax.random.key(seed)
    ks = jax.random.split(key, 32)
    f32 = jnp.float32
    n_pages = PAST_LEN // PAGE_SIZE
    n_used = DEC_BATCH * n_pages
    n_pool = n_used + max(1, n_used // 4)

    def nrm(k, shape, scale):
        return scale * jax.random.normal(k, shape, f32)

    def gain(k, shape):
        return 1.0 + 0.05 * jax.random.normal(k, shape, f32)

    page_table = jax.random.permutation(ks[5], n_pool)[:n_used].reshape(DEC_BATCH, n_pages).astype(jnp.int32)
    return {
        'x_prompt': nrm(ks[0], (BATCH, SEQ, D_MODEL), 1.0),
        'x_sample': nrm(ks[1], (DEC_BATCH, DEC_SEQ, D_MODEL), 1.0),
        'cache_k': nrm(ks[2], (DEPTH, n_pool, PAGE_SIZE, H_B, 2 * DK_B), 1.0),
        'cache_v': nrm(ks[3], (DEPTH, n_pool, PAGE_SIZE, H_B, DV_B), 1.0),
        'state_hgrn': nrm(ks[4], (DEPTH, DEC_BATCH, H_A, DK_A, DV_A), 0.5),
        'page_table': page_table,
        'c_prompt': nrm(ks[6], (BATCH, D_MODEL), 1.0),
        'c_sample': nrm(ks[7], (DEC_BATCH, D_MODEL), 1.0),
        'w_mod': nrm(ks[8], (DEPTH, D_MODEL, 6 * D_MODEL), 0.5 * D_MODEL ** -0.5),
        'b_mod': nrm(ks[9], (DEPTH, 6 * D_MODEL), 0.02),
        'g_pre_mix': gain(ks[10], (DEPTH, D_MODEL)),
        'g_post_mix': gain(ks[11], (DEPTH, D_MODEL)),
        'g_pre_ffn': gain(ks[12], (DEPTH, D_MODEL)),
        'g_post_ffn': gain(ks[13], (DEPTH, D_MODEL)),
        'w_in': nrm(ks[14], (DEPTH, D_MODEL, D_IN), D_MODEL ** -0.5),
        'hgrn_lb': nrm(ks[15], (DEPTH + 1, W_A), 0.1),
        'g_hgrn': gain(ks[16], (DEPTH, H_A, DV_A)),
        'lam_q1': nrm(ks[17], (DEPTH, DK_B), 0.1),
        'lam_k1': nrm(ks[18], (DEPTH, DK_B), 0.1),
        'lam_q2': nrm(ks[19], (DEPTH, DK_B), 0.1),
        'lam_k2': nrm(ks[20], (DEPTH, DK_B), 0.1),
        'g_subln': gain(ks[21], (DEPTH, DV_B)),
        'w_proj_a': nrm(ks[22], (DEPTH, W_A, D_MODEL), W_A ** -0.5),
        'w_proj_b': nrm(ks[23], (DEPTH, W_B, D_MODEL), W_B ** -0.5),
        'w_out': nrm(ks[24], (DEPTH, D_MODEL, D_MODEL), D_MODEL ** -0.5),
        'w_router': nrm(ks[25], (DEPTH, D_MODEL, N_EXPERTS), D_MODEL ** -0.5),
        'b_router': nrm(ks[26], (DEPTH, N_EXPERTS), 0.01),
        'w_gate_up': nrm(ks[27], (DEPTH, N_EXPERTS, D_MODEL, 2 * D_FF), D_MODEL ** -0.5),
        'b_gate_up': nrm(ks[28], (DEPTH, N_EXPERTS, 2 * D_FF), 0.01),
        'w_down': nrm(ks[29], (DEPTH, N_EXPERTS, D_FF, D_MODEL), D_FF ** -0.5),
        'b_down': nrm(ks[30], (DEPTH, N_EXPERTS, D_MODEL), 0.01),
    }


def reference(x_prompt, x_sample, cache_k, cache_v, state_hgrn, page_table, c_prompt, c_sample,
              w_mod, b_mod, g_pre_mix, g_post_mix, g_pre_ffn, g_post_ffn, w_in, hgrn_lb, g_hgrn,
              lam_q1, lam_k1, lam_q2, lam_k2, g_subln, w_proj_a, w_proj_b, w_out,
              w_router, b_router, w_gate_up, b_gate_up, w_down, b_down):
    lb_all = jnp.cumsum(jax.nn.softmax(hgrn_lb.astype(jnp.float32), axis=0), axis=0)
    y_p, y_s = x_prompt, x_sample
    k_p_list, v_p_list, s_p_list, k_s_list, v_s_list, s_s_list = [], [], [], [], [], []
    for l in range(DEPTH):
        wl = (w_mod[l], b_mod[l], g_pre_mix[l], g_post_mix[l], g_pre_ffn[l], g_post_ffn[l],
              w_in[l], g_hgrn[l], lam_q1[l], lam_k1[l], lam_q2[l], lam_k2[l], g_subln[l],
              w_proj_a[l], w_proj_b[l], w_out[l],
              w_router[l], b_router[l], w_gate_up[l], b_gate_up[l], w_down[l], b_down[l])
        s0_p = jnp.zeros((x_prompt.shape[0], H_A, DK_A, DV_A), jnp.float32)
        y_p, s_p, k_p, v_p = _trunk_layer(y_p, c_prompt, s0_p, None, None, None, l, lb_all[l], *wl)
        y_s, s_s, k_s, v_s = _trunk_layer(y_s, c_sample, state_hgrn[l], cache_k[l], cache_v[l], page_table,
                                          l, lb_all[l], *wl)
        k_p_list.append(k_p)
        v_p_list.append(v_p)
        s_p_list.append(s_p)
        k_s_list.append(k_s)
        v_s_list.append(v_s)
        s_s_list.append(s_s)
    return (y_p, y_s, jnp.stack(k_p_list), jnp.stack(v_p_list), jnp.stack(s_p_list),
            jnp.stack(k_s_list), jnp.stack(v_s_list), jnp.stack(s_s_list))
```

```python
import functools
import math

import jax
import jax.numpy as jnp
from jax import lax
from jax.experimental import pallas as pl
from jax.experimental.pallas import tpu as pltpu

F32 = jnp.float32
BF16 = jnp.bfloat16

D = 1024
H = 4
DH = 128
DK = 64
W_MIX = H * DH
HGRN_CHUNK = 64
N_EXPERTS = 32
TOP_K = 4
SWIGLU_LIMIT = 7.0
SWIGLU_ALPHA = 1.702
EPS = 1e-6
NEG_INF = -1e30
LAM_INIT = 0.8 - 0.6 * math.exp(-0.3 * 0)
C_HG = (0, 4 * W_MIX)
C_BQ = (4 * W_MIX, 5 * W_MIX)
C_BK = (5 * W_MIX, 6 * W_MIX)
C_BV = (6 * W_MIX, 7 * W_MIX)
C_GT = (7 * W_MIX, 7 * W_MIX + 2 * D)
D_IN = 7 * W_MIX + 2 * D

VMEM_LIMIT = 56 * 1024 * 1024
MOE_BLOCK = 256


def _cp(sem, vmem=VMEM_LIMIT):
    return pltpu.CompilerParams(dimension_semantics=sem, vmem_limit_bytes=vmem)


def _sigmoid(x):
    return 1.0 / (1.0 + jnp.exp(-x))


def _rms(x, g):
    return x * lax.rsqrt(jnp.mean(x * x, axis=-1, keepdims=True) + EPS) * g


def _dot(a, b):
    return jnp.dot(a, b, preferred_element_type=F32)


def _dot_nt(a, b):
    return lax.dot_general(a, b, (((1,), (1,)), ((), ())), preferred_element_type=F32)


def _dot_tn(a, b):
    return lax.dot_general(a, b, (((0,), (0,)), ((), ())), preferred_element_type=F32)


def _mod_kernel(c_ref, w_ref, b_ref, o_ref):
    c = c_ref[...]
    o_ref[...] = _dot(c * _sigmoid(c), w_ref[...]) + b_ref[...]


def _modulation(c_all, w_mod, b_mod):
    n = c_all.shape[0]
    tn = 1024
    return pl.pallas_call(
        _mod_kernel,
        grid=(6 * D // tn,),
        in_specs=[pl.BlockSpec((n, D), lambda j: (0, 0)),
                  pl.BlockSpec((D, tn), lambda j: (0, j)),
                  pl.BlockSpec((1, tn), lambda j: (0, j))],
        out_specs=pl.BlockSpec((n, tn), lambda j: (0, j)),
        out_shape=jax.ShapeDtypeStruct((n, 6 * D), F32),
        compiler_params=_cp(("arbitrary",)),
        name="modulation",
    )(c_all, w_mod, b_mod.reshape(1, 6 * D))


def _inproj_kernel(x_ref, sc_ref, sh_ref, g_ref, w_ref, hg_ref, bq_ref, k_ref, v_ref, gt_ref):
    h = _rms(x_ref[...], g_ref[...]) * (1.0 + sc_ref[...]) + sh_ref[...]
    hb = h.astype(BF16)
    for (lo, hi), o_ref in ((C_HG, hg_ref), (C_BQ, bq_ref), (C_BK, k_ref), (C_BV, v_ref), (C_GT, gt_ref)):
        o_ref[...] = _dot(hb, w_ref[:, lo:hi])


def _inproj(x2d, m3d, tiles_per_mod, g_pre, w_in_bf16, tm):
    R = x2d.shape[0]
    r = m3d.shape[1]
    mod = lambda col: pl.BlockSpec((None, r, D), lambda i: (i // tiles_per_mod, 0, col))
    row = lambda w: pl.BlockSpec((tm, w), lambda i: (i, 0))
    widths = (4 * W_MIX, W_MIX, W_MIX, W_MIX, 2 * D)
    return pl.pallas_call(
        _inproj_kernel,
        grid=(R // tm,),
        in_specs=[row(D), mod(1), mod(0),
                  pl.BlockSpec((1, D), lambda i: (0, 0)),
                  pl.BlockSpec((D, D_IN), lambda i: (0, 0))],
        out_specs=[row(w) for w in widths],
        out_shape=[jax.ShapeDtypeStruct((R, w), F32) for w in widths],
        compiler_params=_cp(("arbitrary",)),
        name="inproj",
    )(x2d, m3d, m3d, g_pre.reshape(1, D), w_in_bf16)


def _hgrn_prompt_kernel(lb_ref, gn_ref, q_ref, f_ref, i_ref, g_ref, o_ref, sout_ref, st_ref):
    c = pl.program_id(0)
    nb = q_ref.shape[0]
    C = HGRN_CHUNK

    @pl.when(c == 0)
    def _():
        st_ref[...] = jnp.zeros_like(st_ref)

    row = lax.broadcasted_iota(jnp.int32, (C, DH), 0)
    causal = lax.broadcasted_iota(jnp.int32, (C, C), 0) >= lax.broadcasted_iota(jnp.int32, (C, C), 1)

    def per_batch(b, carry):
        for h in range(H):
            hs = slice(h * DH, (h + 1) * DH)
            lb = lb_ref[:, hs]
            q = q_ref[b, :, hs]
            f = lb + (1.0 - lb) * _sigmoid(f_ref[b, :, hs])
            k = 1.0 - f
            v = i_ref[b, :, hs]
            gc = jnp.log(f)
            for s in (1, 2, 4, 8, 16, 32):
                gc = gc + jnp.where(row >= s, pltpu.roll(gc, s, axis=0), 0.0)
            g_last = gc[C - 1:C, :]
            qd = q * jnp.exp(gc)
            kd = k * jnp.exp(-gc)
            a = jnp.where(causal, _dot_nt(qd, kd), 0.0)
            st = st_ref[b, h]
            o = _dot(a, v) + _dot_nt(qd, st)
            kv_t = _dot_tn(v, k * jnp.exp(g_last - gc))
            st_ref[b, h] = jnp.exp(g_last) * st + kv_t
            o = _rms(o, gn_ref[:, hs])
            ag = g_ref[b, :, hs]
            o_ref[b, :, hs] = o * (ag * _sigmoid(ag))
        return carry

    lax.fori_loop(0, nb, per_batch, 0)

    @pl.when(c == pl.num_programs(0) - 1)
    def _():
        def fin(b, carry):
            for h in range(H):
                sout_ref[b, h] = st_ref[b, h].T
            return carry
        lax.fori_loop(0, nb, fin, 0)


def _hgrn_prompt(hg3, lb, g_hgrn):
    B, L, _ = hg3.shape
    C = HGRN_CHUNK
    col = lambda j: pl.BlockSpec((B, C, W_MIX), lambda c: (0, c, j))
    vec = pl.BlockSpec((1, W_MIX), lambda c: (0, 0))
    return pl.pallas_call(
        _hgrn_prompt_kernel,
        grid=(L // C,),
        in_specs=[vec, vec, col(0), col(1), col(2), col(3)],
        out_specs=[pl.BlockSpec((B, C, W_MIX), lambda c: (0, c, 0)),
                   pl.BlockSpec((B, H, DH, DH), lambda c: (0, 0, 0, 0))],
        out_shape=[jax.ShapeDtypeStruct((B, L, W_MIX), F32),
                   jax.ShapeDtypeStruct((B, H, DH, DH), F32)],
        scratch_shapes=[pltpu.VMEM((B, H, DH, DH), F32)],
        compiler_params=_cp(("arbitrary",)),
        name="hgrn_prompt",
    )(lb.reshape(1, W_MIX), g_hgrn.reshape(1, W_MIX), hg3, hg3, hg3, hg3)


def _hgrn_step_kernel(lb_ref, gn_ref, x_ref, s_ref, o_ref, sout_ref):
    nt = x_ref.shape[0]
    eye = lax.broadcasted_iota(jnp.int32, (DH, DH), 0) == lax.broadcasted_iota(jnp.int32, (DH, DH), 1)

    def col(r):
        return jnp.sum(jnp.where(eye, jnp.broadcast_to(r, (DH, DH)), 0.0), axis=1, keepdims=True)

    for t in range(nt):
        for h in range(H):
            seg = lambda j: x_ref[t:t + 1, j * W_MIX + h * DH: j * W_MIX + (h + 1) * DH]
            lb = lb_ref[:, h * DH:(h + 1) * DH]
            q, v, ag = seg(0), seg(2), seg(3)
            f = lb + (1.0 - lb) * _sigmoid(seg(1))
            s_new = col(f) * s_ref[t, h] + col(1.0 - f) * v
            sout_ref[t, h] = s_new
            o = jnp.sum(col(q) * s_new, axis=0, keepdims=True)
            o = _rms(o, gn_ref[:, h * DH:(h + 1) * DH])
            o_ref[t:t + 1, h * DH:(h + 1) * DH] = o * (ag * _sigmoid(ag))


def _hgrn_step(hg, state, lb, g_hgrn):
    N = hg.shape[0]
    nt = 8
    vec = pl.BlockSpec((1, W_MIX), lambda i: (0, 0))
    st = pl.BlockSpec((nt, H, DH, DH), lambda i: (i, 0, 0, 0))
    return pl.pallas_call(
        _hgrn_step_kernel,
        grid=(N // nt,),
        in_specs=[vec, vec, pl.BlockSpec((nt, 4 * W_MIX), lambda i: (i, 0)), st],
        out_specs=[pl.BlockSpec((nt, W_MIX), lambda i: (i, 0)), st],
        out_shape=[jax.ShapeDtypeStruct((N, W_MIX), F32),
                   jax.ShapeDtypeStruct((N, H, DH, DH), F32)],
        compiler_params=_cp(("arbitrary",)),
        name="hgrn_step",
    )(lb.reshape(1, W_MIX), g_hgrn.reshape(1, W_MIX), hg, state)


def _attn_prompt_kernel(lam_ref, sl_ref, q_ref, k_ref, v_ref, g_ref, o_ref, *, tq):
    h = pl.program_id(1)
    qi = pl.program_id(2)
    slope = sl_ref[h]
    lam = lam_ref[0]
    q = q_ref[...] * (DK ** -0.5)
    lane = lax.broadcasted_iota(jnp.int32, (tq, DH), 1)
    qs = (jnp.where(lane < DK, q, 0.0), jnp.where(lane >= DK, q, 0.0))
    ri = lax.broadcasted_iota(jnp.int32, (tq, tq), 0)
    ci = lax.broadcasted_iota(jnp.int32, (tq, tq), 1)
    base = slope * (ri - ci).astype(F32)

    def block(kb, carry, masked):
        kt = k_ref[pl.ds(kb * tq, tq), :]
        vt = v_ref[pl.ds(kb * tq, tq), :]
        bias = base + slope * ((qi - kb) * tq).astype(F32)
        out = []
        for mp in range(2):
            m, l, acc = carry[mp]
            s = _dot_nt(qs[mp], kt) - bias
            if masked:
                s = jnp.where(ri >= ci, s, NEG_INF)
            m_new = jnp.maximum(m, jnp.max(s, axis=-1, keepdims=True))
            alpha = jnp.exp(m - m_new)
            p = jnp.exp(s - m_new)
            out.append((m_new, alpha * l + jnp.sum(p, axis=-1, keepdims=True), alpha * acc + _dot(p, vt)))
        return tuple(out)

    init1 = (jnp.full((tq, 1), NEG_INF, F32), jnp.zeros((tq, 1), F32), jnp.zeros((tq, DH), F32))
    carry = lax.fori_loop(0, qi, lambda kb, c: block(kb, c, False), (init1, init1))
    (_, l1, a1), (_, l2, a2) = block(qi, carry, True)
    o = a1 / l1 - lam * (a2 / l2)
    o_ref[...] = _rms(o, g_ref[...]) * (1.0 - LAM_INIT)


def _attn_prompt(bq, bk, bv, lam, slopes, g_subln, B, L):
    tq = 256
    nq = L // tq
    smem = pl.BlockSpec(memory_space=pltpu.SMEM)
    qspec = pl.BlockSpec((tq, DH), lambda b, h, i: (b * nq + i, h))
    kvspec = pl.BlockSpec((L, DH), lambda b, h, i: (b, h))
    return pl.pallas_call(
        functools.partial(_attn_prompt_kernel, tq=tq),
        grid=(B, H, nq),
        in_specs=[smem, smem, qspec, kvspec, kvspec, pl.BlockSpec((1, DH), lambda b, h, i: (0, 0))],
        out_specs=qspec,
        out_shape=jax.ShapeDtypeStruct((B * L, W_MIX), F32),
        compiler_params=_cp(("arbitrary", "arbitrary", "arbitrary")),
        name="attn_prompt",
    )(lam.reshape(1), slopes, bq, bk, bv, g_subln.reshape(1, DH))


def _attn_step_kernel(pt_ref, lam_ref, sl_ref, q_ref, kn_ref, vn_ref, g_ref, *refs, n_pages, page):
    k_refs, v_refs, o_ref = refs[:n_pages], refs[n_pages:2 * n_pages], refs[2 * n_pages]
    past = n_pages * page
    lam = lam_ref[0]
    R = 2 * H
    grp = lax.broadcasted_iota(jnp.int32, (R, W_MIX), 1) // DK
    rid = lax.broadcasted_iota(jnp.int32, (R, W_MIX), 0)
    qbd = jnp.where(grp == rid, jnp.broadcast_to(q_ref[...], (R, W_MIX)), 0.0) * (DK ** -0.5)
    slope = jnp.concatenate([jnp.full((2, 1), sl_ref[hh], F32) for hh in range(H)], axis=0)
    kpos = lax.broadcasted_iota(jnp.int32, (R, past), 1)
    s = jnp.concatenate([_dot_nt(qbd, k_refs[j][...]) for j in range(n_pages)], axis=1)
    s = s - slope * (past - kpos).astype(F32)
    s_new = jnp.sum(qbd * kn_ref[...], axis=1, keepdims=True)
    m = jnp.maximum(jnp.max(s, axis=1, keepdims=True), s_new)
    p = jnp.exp(s - m)
    p_new = jnp.exp(s_new - m)
    inv_l = 1.0 / (jnp.sum(p, axis=1, keepdims=True) + p_new)
    p = p * inv_l
    o8 = (p_new * inv_l) * vn_ref[...]
    for j in range(n_pages):
        o8 = o8 + _dot(p[:, j * page:(j + 1) * page], v_refs[j][...])
    for hh in range(H):
        hs = slice(hh * DH, (hh + 1) * DH)
        o = o8[2 * hh:2 * hh + 1, hs] - lam * o8[2 * hh + 1:2 * hh + 2, hs]
        o_ref[:, hs] = _rms(o, g_ref[...]) * (1.0 - LAM_INIT)


def _attn_step(bq, k_new, v_new, cache_k, cache_v, page_table, lam, slopes, g_subln):
    N = bq.shape[0]
    n_pool, page = cache_k.shape[0], cache_k.shape[1]
    n_pages = page_table.shape[1]
    kc = cache_k.reshape(n_pool, page, W_MIX)
    vc = cache_v.reshape(n_pool, page, W_MIX)
    smem = pl.BlockSpec(memory_space=pltpu.SMEM)
    rowspec = pl.BlockSpec((None, 1, W_MIX), lambda b, pt: (b, 0, 0))
    pspec = lambda j: pl.BlockSpec((None, page, W_MIX), lambda b, pt: (pt[b * n_pages + j], 0, 0))
    grid_spec = pltpu.PrefetchScalarGridSpec(
        num_scalar_prefetch=1,
        grid=(N,),
        in_specs=[smem, smem, rowspec, rowspec, rowspec, pl.BlockSpec((1, DH), lambda b, pt: (0, 0))]
                 + [pspec(j) for j in range(n_pages)] * 2,
        out_specs=rowspec,
    )
    out = pl.pallas_call(
        functools.partial(_attn_step_kernel, n_pages=n_pages, page=page),
        grid_spec=grid_spec,
        out_shape=jax.ShapeDtypeStruct((N, 1, W_MIX), F32),
        compiler_params=_cp(("arbitrary",)),
        name="attn_step",
    )(page_table.reshape(-1), lam.reshape(1), slopes, bq.reshape(N, 1, W_MIX), k_new.reshape(N, 1, W_MIX),
      v_new.reshape(N, 1, W_MIX), g_subln.reshape(1, DH), *([kc] * n_pages), *([vc] * n_pages))
    return out.reshape(N, W_MIX)


def _merge_kernel(oa_ref, ob_ref, gt_ref, x_ref, g1_ref, sc2_ref, sh2_ref, gpost_ref, gpre_ref,
                  wa_ref, wb_ref, wo_ref, wr_ref, br_ref, x1_ref, h2_ref, lg_ref):
    ga = _sigmoid(gt_ref[:, :D])
    gb = _sigmoid(gt_ref[:, D:])
    merged = ga * _dot(oa_ref[...], wa_ref[...]) + gb * _dot(ob_ref[...], wb_ref[...])
    x1 = x_ref[...] + g1_ref[...] * _rms(_dot(merged, wo_ref[...]), gpost_ref[...])
    x1_ref[...] = x1
    h2 = _rms(x1, gpre_ref[...]) * (1.0 + sc2_ref[...]) + sh2_ref[...]
    h2_ref[...] = h2
    lg_ref[...] = _dot(h2, wr_ref[...]) + br_ref[...]


def _merge(oa, ob, gt, x2d, m3d, tiles_per_mod, g_post, g_pre, wa, wb, wo, wr, br, tm):
    R = x2d.shape[0]
    r = m3d.shape[1]
    mod = lambda col: pl.BlockSpec((None, r, D), lambda i: (i // tiles_per_mod, 0, col))
    row = lambda w: pl.BlockSpec((tm, w), lambda i: (i, 0))
    full = lambda a, b: pl.BlockSpec((a, b), lambda i: (0, 0))
    return pl.pallas_call(
        _merge_kernel,
        grid=(R // tm,),
        in_specs=[row(W_MIX), row(W_MIX), row(2 * D), row(D), mod(2), mod(4), mod(3),
                  full(1, D), full(1, D), full(W_MIX, D), full(W_MIX, D), full(D, D),
                  full(D, N_EXPERTS), full(1, N_EXPERTS)],
        out_specs=[row(D), row(D), row(N_EXPERTS)],
        out_shape=[jax.ShapeDtypeStruct((R, D), F32), jax.ShapeDtypeStruct((R, D), F32),
                   jax.ShapeDtypeStruct((R, N_EXPERTS), F32)],
        compiler_params=_cp(("arbitrary",)),
        name="merge",
    )(oa, ob, gt, x2d, m3d, m3d, m3d, g_post.reshape(1, D), g_pre.reshape(1, D), wa, wb, wo, wr,
      br.reshape(1, N_EXPERTS))


def _expert_kernel(be_ref, nu_ref, x_ref, wgu_ref, bgu_ref, wd_ref, bd_ref, y_ref):
    i = pl.program_id(0)

    @pl.when(i < nu_ref[0])
    def _():
        gu = _dot(x_ref[...], wgu_ref[...]) + bgu_ref[...]
        glu = jnp.minimum(gu[:, :D], SWIGLU_LIMIT)
        lin = jnp.clip(gu[:, D:], -SWIGLU_LIMIT, SWIGLU_LIMIT)
        act = (lin + 1.0) * glu * _sigmoid(SWIGLU_ALPHA * glu)
        y_ref[...] = _dot(act, wd_ref[...]) + bd_ref[...]

    @pl.when(i >= nu_ref[0])
    def _():
        y_ref[...] = jnp.zeros_like(y_ref)


def _experts(x_buf, block_expert, n_used, w_gate_up, b_gate_up, w_down, b_down):
    n_rows = x_buf.shape[0]
    bm = MOE_BLOCK
    grid_spec = pltpu.PrefetchScalarGridSpec(
        num_scalar_prefetch=2,
        grid=(n_rows // bm,),
        in_specs=[pl.BlockSpec((bm, D), lambda i, be, nu: (i, 0)),
                  pl.BlockSpec((None, D, 2 * D), lambda i, be, nu: (be[i], 0, 0)),
                  pl.BlockSpec((None, 1, 2 * D), lambda i, be, nu: (be[i], 0, 0)),
                  pl.BlockSpec((None, D, D), lambda i, be, nu: (be[i], 0, 0)),
                  pl.BlockSpec((None, 1, D), lambda i, be, nu: (be[i], 0, 0))],
        out_specs=pl.BlockSpec((bm, D), lambda i, be, nu: (i, 0)),
    )
    return pl.pallas_call(
        _expert_kernel,
        grid_spec=grid_spec,
        out_shape=jax.ShapeDtypeStruct((n_rows, D), F32),
        compiler_params=_cp(("arbitrary",)),
        name="experts",
    )(block_expert, n_used, x_buf, w_gate_up, b_gate_up.reshape(N_EXPERTS, 1, 2 * D), w_down,
      b_down.reshape(N_EXPERTS, 1, D))


def _final_kernel(x1_ref, g2_ref, gpost_ref, gates_ref, y0_ref, y1_ref, y2_ref, y3_ref, o_ref):
    gates = gates_ref[...]
    y = gates[:, 0:1] * y0_ref[...]
    for kk, y_ref in ((1, y1_ref), (2, y2_ref), (3, y3_ref)):
        y = y + gates[:, kk:kk + 1] * y_ref[...]
    o_ref[...] = x1_ref[...] + g2_ref[...] * _rms(y, gpost_ref[...])


def _final(x1, m3d, tiles_per_mod, g_post, gates, ys, row0, tm):
    R = x1.shape[0]
    r = m3d.shape[1]
    off = row0 // tm
    row = pl.BlockSpec((tm, D), lambda i: (i, 0))
    rowo = pl.BlockSpec((tm, D), lambda i: (i + off, 0))
    return pl.pallas_call(
        _final_kernel,
        grid=(R // tm,),
        in_specs=[row, pl.BlockSpec((None, r, D), lambda i: (i // tiles_per_mod, 0, 5)),
                  pl.BlockSpec((1, D), lambda i: (0, 0)),
                  pl.BlockSpec((tm, TOP_K), lambda i: (i + off, 0)), rowo, rowo, rowo, rowo],
        out_specs=row,
        out_shape=jax.ShapeDtypeStruct((R, D), F32),
        compiler_params=_cp(("arbitrary",)),
        name="final",
    )(x1, m3d, g_post.reshape(1, D), gates, *ys)


def kernel(x_prompt, x_sample, cache_k, cache_v, state_hgrn, page_table, c_prompt, c_sample, w_mod, b_mod, g_pre_mix, g_post_mix, g_pre_ffn, g_post_ffn, w_in, hgrn_lb, g_hgrn, lam_q1, lam_k1, lam_q2, lam_k2, g_subln, w_proj_a, w_proj_b, w_out, w_router, b_router, w_gate_up, b_gate_up, w_down, b_down):
    B, L, _ = x_prompt.shape
    NS = x_sample.shape[0]
    T = B * L
    T_all = T + NS
    l = 0

    lb = jnp.cumsum(jax.nn.softmax(hgrn_lb.astype(F32), axis=0), axis=0)[l]
    lam = (jnp.exp(jnp.sum(lam_q1[l] * lam_k1[l])) - jnp.exp(jnp.sum(lam_q2[l] * lam_k2[l])) + LAM_INIT).astype(F32)
    slopes = jnp.exp2(-8.0 * jnp.arange(1, H + 1, dtype=F32) / H)

    m_all = _modulation(jnp.concatenate([c_prompt, c_sample], axis=0), w_mod[l], b_mod[l])
    m_p = m_all[:B].reshape(B, 1, 6 * D)
    m_s = m_all[B:].reshape(1, NS, 6 * D)

    w_in_b = w_in[l].astype(BF16)
    tm = 256
    tpm = L // tm
    xp = x_prompt.reshape(T, D)
    xs = x_sample.reshape(NS, D)

    hg_p, bq_p, k_p, v_p, gt_p = _inproj(xp, m_p, tpm, g_pre_mix[l], w_in_b, tm)
    oa_p, s_p = _hgrn_prompt(hg_p.reshape(B, L, 4 * W_MIX), lb, g_hgrn[l])
    ob_p = _attn_prompt(bq_p, k_p, v_p, lam, slopes, g_subln[l], B, L)
    x1_p, h2_p, lg_p = _merge(oa_p.reshape(T, W_MIX), ob_p, gt_p, xp, m_p, tpm, g_post_mix[l], g_pre_ffn[l],
                              w_proj_a[l], w_proj_b[l], w_out[l], w_router[l], b_router[l], tm)

    hg_s, bq_s, k_s, v_s, gt_s = _inproj(xs, m_s, 1, g_pre_mix[l], w_in_b, NS)
    oa_s, s_s = _hgrn_step(hg_s, state_hgrn[l], lb, g_hgrn[l])
    ob_s = _attn_step(bq_s, k_s, v_s, cache_k[l], cache_v[l], page_table, lam, slopes, g_subln[l])
    x1_s, h2_s, lg_s = _merge(oa_s, ob_s, gt_s, xs, m_s, 1, g_post_mix[l], g_pre_ffn[l],
                              w_proj_a[l], w_proj_b[l], w_out[l], w_router[l], b_router[l], NS)

    h2 = jnp.concatenate([h2_p, h2_s], axis=0)
    logits = jnp.concatenate([lg_p, lg_s], axis=0)
    top_val, top_idx = lax.top_k(logits, TOP_K)
    gates = jax.nn.softmax(top_val, axis=-1)
    n_assign = T_all * TOP_K
    e_flat = top_idx.reshape(-1).astype(jnp.int32)
    order = jnp.argsort(e_flat).astype(jnp.int32)
    e_sorted = e_flat[order]
    tok_sorted = order // TOP_K
    counts = jnp.bincount(e_flat, length=N_EXPERTS).astype(jnp.int32)
    bm = MOE_BLOCK
    padded = (counts + bm - 1) // bm * bm
    pad_end = jnp.cumsum(padded)
    pad_start = pad_end - padded
    start = jnp.cumsum(counts) - counts
    dest = pad_start[e_sorted] + jnp.arange(n_assign, dtype=jnp.int32) - start[e_sorted]
    n_blocks = -(-n_assign // bm) + N_EXPERTS
    src_tok = jnp.zeros((n_blocks * bm,), jnp.int32).at[dest].set(tok_sorted)
    pos = jnp.zeros((n_assign,), jnp.int32).at[order].set(dest).reshape(T_all, TOP_K)
    block_expert = jnp.minimum(
        jnp.searchsorted(pad_end, jnp.arange(n_blocks, dtype=jnp.int32) * bm, side='right'),
        N_EXPERTS - 1).astype(jnp.int32)
    n_used = (pad_end[-1:] // bm).astype(jnp.int32)

    x_buf = h2[src_tok]
    y_buf = _experts(x_buf, block_expert, n_used, w_gate_up[l], b_gate_up[l], w_down[l], b_down[l])
    ys = [y_buf[pos[:, kk]] for kk in range(TOP_K)]

    y_p = _final(x1_p, m_p, tpm, g_post_ffn[l], gates, ys, 0, tm)
    y_s = _final(x1_s, m_s, 1, g_post_ffn[l], gates, ys, T, NS)

    return (y_p.reshape(B, L, D), y_s.reshape(NS, 1, D),
            k_p.reshape(1, B, L, H, DH), v_p.reshape(1, B, L, H, DH), s_p.reshape(1, B, H, DH, DH),
            k_s.reshape(1, NS, 1, H, DH), v_s.reshape(1, NS, 1, H, DH), s_s.reshape(1, NS, H, DH, DH))
```

```python
import functools
import math

import jax
import jax.numpy as jnp
from jax import lax
from jax.experimental import pallas as pl
from jax.experimental.pallas import tpu as pltpu

F32 = jnp.float32
BF16 = jnp.bfloat16

D = 1024
LANES = 128
SUB = D // LANES
H = 4
DH = 128
DK = 64
W_MIX = H * DH
HGRN_CHUNK = 64
N_EXPERTS = 32
TOP_K = 4
SWIGLU_LIMIT = 7.0
SWIGLU_ALPHA = 1.702
EPS = 1e-6
NEG_INF = -1e30
LOG2E = 1.4426950408889634
LAM_INIT = 0.8 - 0.6 * math.exp(-0.3 * 0)
C_HG = (0, 4 * W_MIX)
C_BQ = (4 * W_MIX, 5 * W_MIX)
C_BK = (5 * W_MIX, 6 * W_MIX)
C_BV = (6 * W_MIX, 7 * W_MIX)
C_GT = (7 * W_MIX, 7 * W_MIX + 2 * D)
D_IN = 7 * W_MIX + 2 * D
RT_E, RT_RANK, RT_GATE = 0, TOP_K, 2 * TOP_K

VMEM_LIMIT = 56 * 1024 * 1024
MOE_BLOCK = 256
DISPATCH_ROWS = 128


def _cp(sem, vmem=VMEM_LIMIT):
    return pltpu.CompilerParams(dimension_semantics=sem, vmem_limit_bytes=vmem)


def _sigmoid(x):
    return 1.0 / (1.0 + jnp.exp(-x))


def _rms(x, g):
    return x * lax.rsqrt(jnp.mean(x * x, axis=-1, keepdims=True) + EPS) * g


def _dot(a, b):
    return jnp.dot(a, b, preferred_element_type=F32)


def _dot_nt(a, b):
    return lax.dot_general(a, b, (((1,), (1,)), ((), ())), preferred_element_type=F32)


def _dot_tn(a, b):
    return lax.dot_general(a, b, (((0,), (0,)), ((), ())), preferred_element_type=F32)


def _to_tiles(ref3, x2d):
    for j in range(SUB):
        ref3[:, j, :] = x2d[:, j * LANES:(j + 1) * LANES]


def _from_tiles(ref3):
    return jnp.concatenate([ref3[:, j, :] for j in range(SUB)], axis=1)


def _mod_kernel(c_ref, w_ref, b_ref, o_ref):
    c = c_ref[...]
    o_ref[...] = _dot(c * _sigmoid(c), w_ref[...]) + b_ref[...]


def _modulation(c_all, w_mod, b_mod):
    n = c_all.shape[0]
    tn = 1024
    return pl.pallas_call(
        _mod_kernel,
        grid=(6 * D // tn,),
        in_specs=[pl.BlockSpec((n, D), lambda j: (0, 0)),
                  pl.BlockSpec((D, tn), lambda j: (0, j)),
                  pl.BlockSpec((1, tn), lambda j: (0, j))],
        out_specs=pl.BlockSpec((n, tn), lambda j: (0, j)),
        out_shape=jax.ShapeDtypeStruct((n, 6 * D), F32),
        compiler_params=_cp(("arbitrary",)),
        name="modulation",
    )(c_all, w_mod, b_mod.reshape(1, 6 * D))


def _inproj_kernel(x_ref, sc_ref, sh_ref, g_ref, w_ref, hg_ref, bq_ref, k_ref, v_ref, gt_ref, k4_ref, v4_ref):
    h = _rms(x_ref[...], g_ref[...]) * (1.0 + sc_ref[...]) + sh_ref[...]
    hb = h.astype(BF16)
    mm = lambda c: _dot(hb, w_ref[:, c[0]:c[1]])
    hg_ref[...] = mm(C_HG)
    bq_ref[...] = mm(C_BQ)
    gt_ref[...] = mm(C_GT)
    for c, o_ref, o4_ref in ((C_BK, k_ref, k4_ref), (C_BV, v_ref, v4_ref)):
        r = mm(c)
        o_ref[...] = r
        for hh in range(H):
            o4_ref[:, hh, :] = r[:, hh * DH:(hh + 1) * DH]


def _inproj(x2d, m3d, tiles_per_mod, g_pre, w_in_bf16, tm):
    R = x2d.shape[0]
    r = m3d.shape[1]
    mod = lambda col: pl.BlockSpec((None, r, D), lambda i: (i // tiles_per_mod, 0, col))
    row = lambda w: pl.BlockSpec((tm, w), lambda i: (i, 0))
    row4 = pl.BlockSpec((tm, H, DH), lambda i: (i, 0, 0))
    widths = (4 * W_MIX, W_MIX, W_MIX, W_MIX, 2 * D)
    kv4 = jax.ShapeDtypeStruct((R, H, DH), F32)
    return pl.pallas_call(
        _inproj_kernel,
        grid=(R // tm,),
        in_specs=[row(D), mod(1), mod(0),
                  pl.BlockSpec((1, D), lambda i: (0, 0)),
                  pl.BlockSpec((D, D_IN), lambda i: (0, 0))],
        out_specs=[row(w) for w in widths] + [row4, row4],
        out_shape=[jax.ShapeDtypeStruct((R, w), F32) for w in widths] + [kv4, kv4],
        compiler_params=_cp(("arbitrary",)),
        name="inproj",
    )(x2d, m3d, m3d, g_pre.reshape(1, D), w_in_bf16)


def _hgrn_prompt_kernel(lb_ref, gn_ref, q_ref, f_ref, i_ref, g_ref, o_ref, sout_ref, st_ref):
    c = pl.program_id(0)
    nb = q_ref.shape[0]
    C = HGRN_CHUNK

    @pl.when(c == 0)
    def _():
        st_ref[...] = jnp.zeros_like(st_ref)

    row = lax.broadcasted_iota(jnp.int32, (C, DH), 0)
    causal = lax.broadcasted_iota(jnp.int32, (C, C), 0) >= lax.broadcasted_iota(jnp.int32, (C, C), 1)

    def per_batch(b, carry):
        for h in range(H):
            hs = slice(h * DH, (h + 1) * DH)
            lb = lb_ref[:, hs]
            q = q_ref[b, :, hs]
            f = lb + (1.0 - lb) * _sigmoid(f_ref[b, :, hs])
            k = 1.0 - f
            v = i_ref[b, :, hs]
            gc = jnp.log(f)
            for s in (1, 2, 4, 8, 16, 32):
                gc = gc + jnp.where(row >= s, pltpu.roll(gc, s, axis=0), 0.0)
            g_last = gc[C - 1:C, :]
            qd = q * jnp.exp(gc)
            kd = k * jnp.exp(-gc)
            a = jnp.where(causal, _dot_nt(qd, kd), 0.0)
            st = st_ref[b, h]
            o = _dot(a, v) + _dot_nt(qd, st)
            kv_t = _dot_tn(v, k * jnp.exp(g_last - gc))
            st_ref[b, h] = jnp.exp(g_last) * st + kv_t
            o = _rms(o, gn_ref[:, hs])
            ag = g_ref[b, :, hs]
            o_ref[b, :, hs] = o * (ag * _sigmoid(ag))
        return carry

    lax.fori_loop(0, nb, per_batch, 0)

    @pl.when(c == pl.num_programs(0) - 1)
    def _():
        def fin(b, carry):
            for h in range(H):
                sout_ref[b, h] = st_ref[b, h].T
            return carry
        lax.fori_loop(0, nb, fin, 0)


def _hgrn_prompt(hg3, lb, g_hgrn):
    B, L, _ = hg3.shape
    C = HGRN_CHUNK
    col = lambda j: pl.BlockSpec((B, C, W_MIX), lambda c: (0, c, j))
    vec = pl.BlockSpec((1, W_MIX), lambda c: (0, 0))
    return pl.pallas_call(
        _hgrn_prompt_kernel,
        grid=(L // C,),
        in_specs=[vec, vec, col(0), col(1), col(2), col(3)],
        out_specs=[pl.BlockSpec((B, C, W_MIX), lambda c: (0, c, 0)),
                   pl.BlockSpec((B, H, DH, DH), lambda c: (0, 0, 0, 0))],
        out_shape=[jax.ShapeDtypeStruct((B, L, W_MIX), F32),
                   jax.ShapeDtypeStruct((B, H, DH, DH), F32)],
        scratch_shapes=[pltpu.VMEM((B, H, DH, DH), F32)],
        compiler_params=_cp(("arbitrary",)),
        name="hgrn_prompt",
    )(lb.reshape(1, W_MIX), g_hgrn.reshape(1, W_MIX), hg3, hg3, hg3, hg3)


def _hgrn_step_kernel(lb_ref, gn_ref, x_ref, s_ref, o_ref, sout_ref):
    nt = x_ref.shape[0]
    eye = lax.broadcasted_iota(jnp.int32, (DH, DH), 0) == lax.broadcasted_iota(jnp.int32, (DH, DH), 1)

    def col(r):
        return jnp.sum(jnp.where(eye, jnp.broadcast_to(r, (DH, DH)), 0.0), axis=1, keepdims=True)

    for t in range(nt):
        for h in range(H):
            seg = lambda j: x_ref[t:t + 1, j * W_MIX + h * DH: j * W_MIX + (h + 1) * DH]
            lb = lb_ref[:, h * DH:(h + 1) * DH]
            q, v, ag = seg(0), seg(2), seg(3)
            f = lb + (1.0 - lb) * _sigmoid(seg(1))
            s_new = col(f) * s_ref[t, h] + col(1.0 - f) * v
            sout_ref[t, h] = s_new
            o = jnp.sum(col(q) * s_new, axis=0, keepdims=True)
            o = _rms(o, gn_ref[:, h * DH:(h + 1) * DH])
            o_ref[t:t + 1, h * DH:(h + 1) * DH] = o * (ag * _sigmoid(ag))


def _hgrn_step(hg, state, lb, g_hgrn):
    N = hg.shape[0]
    nt = 8
    vec = pl.BlockSpec((1, W_MIX), lambda i: (0, 0))
    st = pl.BlockSpec((nt, H, DH, DH), lambda i: (i, 0, 0, 0))
    return pl.pallas_call(
        _hgrn_step_kernel,
        grid=(N // nt,),
        in_specs=[vec, vec, pl.BlockSpec((nt, 4 * W_MIX), lambda i: (i, 0)), st],
        out_specs=[pl.BlockSpec((nt, W_MIX), lambda i: (i, 0)), st],
        out_shape=[jax.ShapeDtypeStruct((N, W_MIX), F32),
                   jax.ShapeDtypeStruct((N, H, DH, DH), F32)],
        compiler_params=_cp(("arbitrary",)),
        name="hgrn_step",
    )(lb.reshape(1, W_MIX), g_hgrn.reshape(1, W_MIX), hg, state)


def _attn_prompt_kernel(lam_ref, sl_ref, q_ref, k_ref, v_ref, g_ref, o_ref, va_ref, *, tq, tk):
    slope = sl_ref[pl.program_id(1)] * LOG2E
    lam = lam_ref[0]
    L = q_ref.shape[0]
    va_ref[:, :DH] = v_ref[...]
    va_ref[:, DH:] = jnp.ones((L, DH), F32)
    idiff = lax.broadcasted_iota(jnp.int32, (tq, tk), 0) - lax.broadcasted_iota(jnp.int32, (tq, tk), 1)
    dmat = idiff.astype(F32) * slope
    lane = lax.broadcasted_iota(jnp.int32, (tq, DH), 1)

    def q_tile(qi, carry):
        q0 = pl.multiple_of(qi * tq, tq)
        q = q_ref[pl.ds(q0, tq), :] * (DK ** -0.5 * LOG2E)
        o_maps = []
        for mp in range(2):
            qm = jnp.where(lane < DK, q, 0.0) if mp == 0 else jnp.where(lane >= DK, q, 0.0)

            def block(kb, mc, masked):
                m, acc = mc
                k0 = pl.multiple_of(kb * tk, tk)
                s = _dot_nt(qm, k_ref[pl.ds(k0, tk), :]) - dmat
                if masked:
                    s = jnp.where(idiff >= k0 - q0, s, NEG_INF)
                c = slope * (q0 - k0).astype(F32)
                m_new = jnp.maximum(m, jnp.max(s, axis=-1, keepdims=True) - c)
                p = jnp.exp2(s - (m_new + c))
                acc = jnp.exp2(m - m_new) * acc + _dot(p, va_ref[pl.ds(k0, tk), :])
                return m_new, acc

            init = (jnp.full((tq, 1), NEG_INF, F32), jnp.zeros((tq, 2 * DH), F32))
            n_full = q0 // tk
            mc = lax.fori_loop(0, n_full, lambda kb, mc: block(kb, mc, False), init)
            _, acc = block(n_full, mc, True)
            o_maps.append(acc[:, :DH] / acc[:, DH:])
        o = o_maps[0] - lam * o_maps[1]
        o_ref[pl.ds(q0, tq), :] = _rms(o, g_ref[...]) * (1.0 - LAM_INIT)
        return carry

    lax.fori_loop(0, L // tq, q_tile, 0)


def _attn_prompt(bq, bk, bv, lam, slopes, g_subln, B, L):
    tq, tk = 128, 256
    smem = pl.BlockSpec(memory_space=pltpu.SMEM)
    spec = pl.BlockSpec((L, DH), lambda b, h: (b, h))
    return pl.pallas_call(
        functools.partial(_attn_prompt_kernel, tq=tq, tk=tk),
        grid=(B, H),
        in_specs=[smem, smem, spec, spec, spec, pl.BlockSpec((1, DH), lambda b, h: (0, 0))],
        out_specs=spec,
        out_shape=jax.ShapeDtypeStruct((B * L, W_MIX), F32),
        scratch_shapes=[pltpu.VMEM((L, 2 * DH), F32)],
        compiler_params=_cp(("arbitrary", "arbitrary")),
        name="attn_prompt",
    )(lam.reshape(1), slopes, bq, bk, bv, g_subln.reshape(1, DH))


def _attn_step_kernel(pt_ref, lam_ref, sl_ref, q_ref, kn_ref, vn_ref, g_ref, *refs, n_pages, page):
    k_refs, v_refs, o_ref = refs[:n_pages], refs[n_pages:2 * n_pages], refs[2 * n_pages]
    past = n_pages * page
    lam = lam_ref[0]
    R = 2 * H
    grp = lax.broadcasted_iota(jnp.int32, (R, W_MIX), 1) // DK
    rid = lax.broadcasted_iota(jnp.int32, (R, W_MIX), 0)
    qbd = jnp.where(grp == rid, jnp.broadcast_to(q_ref[...], (R, W_MIX)), 0.0) * (DK ** -0.5)
    slope = jnp.concatenate([jnp.full((2, 1), sl_ref[hh], F32) for hh in range(H)], axis=0)
    kpos = lax.broadcasted_iota(jnp.int32, (R, past), 1)
    s = jnp.concatenate(
        [sum(_dot_nt(qbd[:, hh * DH:(hh + 1) * DH], k_refs[j][:, hh, :]) for hh in range(H))
         for j in range(n_pages)], axis=1)
    s = s - slope * (past - kpos).astype(F32)
    s_new = jnp.sum(qbd * kn_ref[...], axis=1, keepdims=True)
    m = jnp.maximum(jnp.max(s, axis=1, keepdims=True), s_new)
    p = jnp.exp(s - m)
    p_new = jnp.exp(s_new - m)
    inv_l = 1.0 / (jnp.sum(p, axis=1, keepdims=True) + p_new)
    p = p * inv_l
    w_new = p_new * inv_l
    for hh in range(H):
        hs = slice(hh * DH, (hh + 1) * DH)
        o8 = w_new * vn_ref[:, hs]
        for j in range(n_pages):
            o8 = o8 + _dot(p[:, j * page:(j + 1) * page], v_refs[j][:, hh, :])
        o = o8[2 * hh:2 * hh + 1, :] - lam * o8[2 * hh + 1:2 * hh + 2, :]
        o_ref[:, hs] = _rms(o, g_ref[...]) * (1.0 - LAM_INIT)


def _attn_step(bq, k_new, v_new, cache_k, cache_v, page_table, lam, slopes, g_subln):
    N = bq.shape[0]
    page = cache_k.shape[2]
    n_pages = page_table.shape[1]
    smem = pl.BlockSpec(memory_space=pltpu.SMEM)
    rowspec = pl.BlockSpec((None, 1, W_MIX), lambda b, pt: (b, 0, 0))
    pspec = lambda j: pl.BlockSpec((None, None, page, H, DH), lambda b, pt: (0, pt[b * n_pages + j], 0, 0, 0))
    grid_spec = pltpu.PrefetchScalarGridSpec(
        num_scalar_prefetch=1,
        grid=(N,),
        in_specs=[smem, smem, rowspec, rowspec, rowspec, pl.BlockSpec((1, DH), lambda b, pt: (0, 0))]
                 + [pspec(j) for j in range(n_pages)] * 2,
        out_specs=rowspec,
    )
    out = pl.pallas_call(
        functools.partial(_attn_step_kernel, n_pages=n_pages, page=page),
        grid_spec=grid_spec,
        out_shape=jax.ShapeDtypeStruct((N, 1, W_MIX), F32),
        compiler_params=_cp(("arbitrary",)),
        name="attn_step",
    )(page_table.reshape(-1), lam.reshape(1), slopes, bq.reshape(N, 1, W_MIX), k_new.reshape(N, 1, W_MIX),
      v_new.reshape(N, 1, W_MIX), g_subln.reshape(1, DH), *([cache_k] * n_pages), *([cache_v] * n_pages))
    return out.reshape(N, W_MIX)


def _merge_kernel(oa_ref, ob_ref, gt_ref, x_ref, g1_ref, sc2_ref, sh2_ref, gpost_ref, gpre_ref,
                  wa_ref, wb_ref, wo_ref, wr_ref, br_ref, cnt0_ref, x1_ref, h3_ref, rt_ref, cnt_ref, run_ref):
    i = pl.program_id(0)
    tm = x_ref.shape[0]

    @pl.when(i == 0)
    def _():
        run_ref[...] = cnt0_ref[...]

    ga = _sigmoid(gt_ref[:, :D])
    gb = _sigmoid(gt_ref[:, D:])
    merged = ga * _dot(oa_ref[...], wa_ref[...]) + gb * _dot(ob_ref[...], wb_ref[...])
    x1 = x_ref[...] + g1_ref[...] * _rms(_dot(merged, wo_ref[...]), gpost_ref[...])
    x1_ref[...] = x1
    h2 = _rms(x1, gpre_ref[...]) * (1.0 + sc2_ref[...]) + sh2_ref[...]
    _to_tiles(h3_ref, h2)

    logits = _dot(h2, wr_ref[...]) + br_ref[...]
    lane = lax.broadcasted_iota(jnp.int32, (tm, N_EXPERTS), 1).astype(F32)
    work = logits
    sels, vals, idxs = [], [], []
    for _ in range(TOP_K):
        mx = jnp.max(work, axis=1, keepdims=True)
        idx = jnp.min(jnp.where(work == mx, lane, float(N_EXPERTS)), axis=1, keepdims=True)
        sel = lane == idx
        work = jnp.where(sel, -jnp.inf, work)
        sels.append(sel)
        vals.append(mx)
        idxs.append(idx)
    ex = [jnp.exp(v - vals[0]) for v in vals]
    inv_den = 1.0 / (ex[0] + ex[1] + ex[2] + ex[3])

    chosen = sum(s.astype(F32) for s in sels)
    before = (lax.broadcasted_iota(jnp.int32, (tm, tm), 0) > lax.broadcasted_iota(jnp.int32, (tm, tm), 1))
    prior = _dot(before.astype(BF16), chosen.astype(BF16)) + run_ref[...]
    run_ref[...] = run_ref[...] + jnp.sum(chosen, axis=0, keepdims=True)
    cnt_ref[...] = run_ref[...]

    rlane = lax.broadcasted_iota(jnp.int32, (tm, LANES), 1)
    rt = jnp.zeros((tm, LANES), F32)
    for kk in range(TOP_K):
        rank = jnp.sum(jnp.where(sels[kk], prior, 0.0), axis=1, keepdims=True)
        rt = jnp.where(rlane == RT_E + kk, idxs[kk], rt)
        rt = jnp.where(rlane == RT_RANK + kk, rank, rt)
        rt = jnp.where(rlane == RT_GATE + kk, ex[kk] * inv_den, rt)
    rt_ref[...] = rt


def _merge(oa, ob, gt, x2d, m3d, tiles_per_mod, g_post, g_pre, wa, wb, wo, wr, br, cnt0, tm):
    R = x2d.shape[0]
    r = m3d.shape[1]
    mod = lambda col: pl.BlockSpec((None, r, D), lambda i: (i // tiles_per_mod, 0, col))
    row = lambda w: pl.BlockSpec((tm, w), lambda i: (i, 0))
    full = lambda a, b: pl.BlockSpec((a, b), lambda i: (0, 0))
    return pl.pallas_call(
        _merge_kernel,
        grid=(R // tm,),
        in_specs=[row(W_MIX), row(W_MIX), row(2 * D), row(D), mod(2), mod(4), mod(3),
                  full(1, D), full(1, D), full(W_MIX, D), full(W_MIX, D), full(D, D),
                  full(D, N_EXPERTS), full(1, N_EXPERTS), full(1, N_EXPERTS)],
        out_specs=[row(D), pl.BlockSpec((tm, SUB, LANES), lambda i: (i, 0, 0)), row(LANES), full(1, N_EXPERTS)],
        out_shape=[jax.ShapeDtypeStruct((R, D), F32),
                   jax.ShapeDtypeStruct((R, SUB, LANES), F32),
                   jax.ShapeDtypeStruct((R, LANES), F32),
                   jax.ShapeDtypeStruct((1, N_EXPERTS), F32)],
        scratch_shapes=[pltpu.VMEM((1, N_EXPERTS), F32)],
        compiler_params=_cp(("arbitrary",)),
        name="merge",
    )(oa, ob, gt, x2d, m3d, m3d, m3d, g_post.reshape(1, D), g_pre.reshape(1, D), wa, wb, wo, wr,
      br.reshape(1, N_EXPERTS), cnt0)


def _dispatch_kernel(lo_ref, hi_ref, nu_ref, dest_ref, ha_ref, hb_ref, x_hbm, zero_ref, sem, *, n_first):
    i = pl.program_id(0)
    tm = ha_ref.shape[0]
    bm = zero_ref.shape[0]
    unroll = 8

    def scatter_rows(h_ref):
        def issue(j, carry):
            for u in range(unroll):
                t = j * unroll + u
                for kk in range(TOP_K):
                    d = dest_ref[t * TOP_K + kk]
                    pltpu.make_async_copy(h_ref.at[pl.ds(t, 1)], x_hbm.at[pl.ds(d, 1)], sem).start()
            return carry

        lax.fori_loop(0, tm // unroll, issue, 0)
        for _ in range(TOP_K):
            pltpu.make_async_copy(h_ref, x_hbm.at[pl.ds(0, tm)], sem).wait()

    @pl.when(i < n_first)
    def _():
        scatter_rows(ha_ref)

    @pl.when(i >= n_first)
    def _():
        scatter_rows(hb_ref)

    @pl.when(i == pl.num_programs(0) - 1)
    def _():
        zero_ref[...] = jnp.zeros_like(zero_ref)

        def per_expert(e, carry):
            zrow = lambda r: pltpu.make_async_copy(zero_ref.at[pl.ds(0, 1)], x_hbm.at[pl.ds(r, 1)], sem)
            lax.fori_loop(lo_ref[e], hi_ref[e], lambda r, c: (zrow(r).start(), c)[1], 0)
            lax.fori_loop(lo_ref[e], hi_ref[e], lambda r, c: (zrow(r).wait(), c)[1], 0)
            return carry

        lax.fori_loop(0, N_EXPERTS, per_expert, 0)

        n_blocks = x_hbm.shape[0] // bm
        zblk = lambda b: pltpu.make_async_copy(zero_ref, x_hbm.at[pl.ds(pl.multiple_of(b * bm, bm), bm)], sem)
        lax.fori_loop(nu_ref[0], n_blocks, lambda b, c: (zblk(b).start(), c)[1], 0)
        lax.fori_loop(nu_ref[0], n_blocks, lambda b, c: (zblk(b).wait(), c)[1], 0)


def _dispatch(h3_a, h3_b, dest_flat, pad_lo, pad_hi, n_used, n_rows):
    tm = DISPATCH_ROWS
    n_a, n_b = h3_a.shape[0] // tm, h3_b.shape[0] // tm
    grid_spec = pltpu.PrefetchScalarGridSpec(
        num_scalar_prefetch=3,
        grid=(n_a + n_b,),
        in_specs=[pl.BlockSpec((tm * TOP_K,), lambda i, *_: (i,), memory_space=pltpu.SMEM),
                  pl.BlockSpec((tm, SUB, LANES), lambda i, *_: (jnp.minimum(i, n_a - 1), 0, 0)),
                  pl.BlockSpec((tm, SUB, LANES), lambda i, *_: (jnp.maximum(i - n_a, 0), 0, 0))],
        out_specs=pl.BlockSpec(memory_space=pl.ANY),
        scratch_shapes=[pltpu.VMEM((MOE_BLOCK, SUB, LANES), F32), pltpu.SemaphoreType.DMA],
    )
    return pl.pallas_call(
        functools.partial(_dispatch_kernel, n_first=n_a),
        grid_spec=grid_spec,
        out_shape=jax.ShapeDtypeStruct((n_rows, SUB, LANES), F32),
        compiler_params=_cp(("arbitrary",)),
        name="dispatch",
    )(pad_lo, pad_hi, n_used, dest_flat, h3_a, h3_b)


def _expert_kernel(be_ref, nu_ref, x_ref, wgu_ref, bgu_ref, wd_ref, bd_ref, y_ref):
    i = pl.program_id(0)

    @pl.when(i < nu_ref[0])
    def _():
        gu = _dot(_from_tiles(x_ref), wgu_ref[...]) + bgu_ref[...]
        glu = jnp.minimum(gu[:, :D], SWIGLU_LIMIT)
        lin = jnp.clip(gu[:, D:], -SWIGLU_LIMIT, SWIGLU_LIMIT)
        act = (lin + 1.0) * glu * _sigmoid(SWIGLU_ALPHA * glu)
        _to_tiles(y_ref, _dot(act, wd_ref[...]) + bd_ref[...])

    @pl.when(i >= nu_ref[0])
    def _():
        y_ref[...] = jnp.zeros_like(y_ref)


def _experts(x3, block_expert, n_used, w_gate_up, b_gate_up, w_down, b_down):
    n_rows = x3.shape[0]
    bm = MOE_BLOCK
    used = lambda i, nu: jnp.minimum(i, nu[0] - 1)
    grid_spec = pltpu.PrefetchScalarGridSpec(
        num_scalar_prefetch=2,
        grid=(n_rows // bm,),
        in_specs=[pl.BlockSpec((bm, SUB, LANES), lambda i, be, nu: (used(i, nu), 0, 0)),
                  pl.BlockSpec((None, D, 2 * D), lambda i, be, nu: (be[used(i, nu)], 0, 0)),
                  pl.BlockSpec((None, 1, 2 * D), lambda i, be, nu: (be[used(i, nu)], 0, 0)),
                  pl.BlockSpec((None, D, D), lambda i, be, nu: (be[used(i, nu)], 0, 0)),
                  pl.BlockSpec((None, 1, D), lambda i, be, nu: (be[used(i, nu)], 0, 0))],
        out_specs=pl.BlockSpec((bm, SUB, LANES), lambda i, be, nu: (i, 0, 0)),
    )
    return pl.pallas_call(
        _expert_kernel,
        grid_spec=grid_spec,
        out_shape=jax.ShapeDtypeStruct((n_rows, SUB, LANES), F32),
        compiler_params=_cp(("arbitrary",)),
        name="experts",
    )(block_expert, n_used, x3, w_gate_up, b_gate_up.reshape(N_EXPERTS, 1, 2 * D), w_down,
      b_down.reshape(N_EXPERTS, 1, D))


def _final_kernel(dcur_ref, dnext_ref, x1_ref, g2_ref, gpost_ref, rt_ref, y_hbm, o_ref, ybuf, sem):
    i = pl.program_id(0)
    n = pl.num_programs(0)
    tm = x1_ref.shape[0]
    slot = i % 2
    unroll = 8

    def issue(d_ref, s):
        def body(j, carry):
            for u in range(unroll):
                t = j * unroll + u
                for kk in range(TOP_K):
                    d = d_ref[t * TOP_K + kk]
                    pltpu.make_async_copy(y_hbm.at[pl.ds(d, 1)], ybuf.at[s, kk, pl.ds(t, 1)], sem.at[s]).start()
            return carry
        lax.fori_loop(0, tm // unroll, body, 0)

    @pl.when(i == 0)
    def _():
        issue(dcur_ref, 0)

    @pl.when(i + 1 < n)
    def _():
        issue(dnext_ref, 1 - slot)

    for kk in range(TOP_K):
        pltpu.make_async_copy(y_hbm.at[pl.ds(0, tm)], ybuf.at[slot, kk], sem.at[slot]).wait()

    y = jnp.zeros((tm, D), F32)
    for kk in range(TOP_K):
        y = y + rt_ref[:, RT_GATE + kk:RT_GATE + kk + 1] * _from_tiles(ybuf.at[slot, kk])
    o_ref[...] = x1_ref[...] + g2_ref[...] * _rms(y, gpost_ref[...])


def _final(x1, m3d, tiles_per_mod, g_post, route, dest_flat, y3, row0, tm):
    R = x1.shape[0]
    r = m3d.shape[1]
    off = row0 // tm
    nsteps = R // tm
    row = pl.BlockSpec((tm, D), lambda i: (i, 0))
    dspec = lambda nxt: pl.BlockSpec((tm * TOP_K,), lambda i: (jnp.minimum(i + nxt, nsteps - 1) + off,),
                                     memory_space=pltpu.SMEM)
    return pl.pallas_call(
        _final_kernel,
        grid=(nsteps,),
        in_specs=[dspec(0), dspec(1), row,
                  pl.BlockSpec((None, r, D), lambda i: (i // tiles_per_mod, 0, 5)),
                  pl.BlockSpec((1, D), lambda i: (0, 0)),
                  pl.BlockSpec((tm, LANES), lambda i: (i, 0)),
                  pl.BlockSpec(memory_space=pl.ANY)],
        out_specs=row,
        out_shape=jax.ShapeDtypeStruct((R, D), F32),
        scratch_shapes=[pltpu.VMEM((2, TOP_K, tm, SUB, LANES), F32), pltpu.SemaphoreType.DMA((2,))],
        compiler_params=_cp(("arbitrary",)),
        name="final",
    )(dest_flat, dest_flat, x1, m3d, g_post.reshape(1, D), route, y3)


def kernel(x_prompt, x_sample, cache_k, cache_v, state_hgrn, page_table, c_prompt, c_sample, w_mod, b_mod, g_pre_mix, g_post_mix, g_pre_ffn, g_post_ffn, w_in, hgrn_lb, g_hgrn, lam_q1, lam_k1, lam_q2, lam_k2, g_subln, w_proj_a, w_proj_b, w_out, w_router, b_router, w_gate_up, b_gate_up, w_down, b_down):
    B, L, _ = x_prompt.shape
    NS = x_sample.shape[0]
    T = B * L
    T_all = T + NS
    l = 0

    lb = jnp.cumsum(jax.nn.softmax(hgrn_lb.astype(F32), axis=0), axis=0)[l]
    lam = (jnp.exp(jnp.sum(lam_q1[l] * lam_k1[l])) - jnp.exp(jnp.sum(lam_q2[l] * lam_k2[l])) + LAM_INIT).astype(F32)
    slopes = jnp.exp2(-8.0 * jnp.arange(1, H + 1, dtype=F32) / H)

    m_all = _modulation(jnp.concatenate([c_prompt, c_sample], axis=0), w_mod[l], b_mod[l])
    m_p = m_all[:B].reshape(B, 1, 6 * D)
    m_s = m_all[B:].reshape(1, NS, 6 * D)

    w_in_b = w_in[l].astype(BF16)
    tm = 256
    tpm = L // tm
    xp = x_prompt.reshape(T, D)
    xs = x_sample.reshape(NS, D)
    merge_w = (g_post_mix[l], g_pre_ffn[l], w_proj_a[l], w_proj_b[l], w_out[l], w_router[l], b_router[l])

    hg_p, bq_p, k_p, v_p, gt_p, k4_p, v4_p = _inproj(xp, m_p, tpm, g_pre_mix[l], w_in_b, tm)
    oa_p, s_p = _hgrn_prompt(hg_p.reshape(B, L, 4 * W_MIX), lb, g_hgrn[l])
    ob_p = _attn_prompt(bq_p, k_p, v_p, lam, slopes, g_subln[l], B, L)
    x1_p, h3_p, rt_p, cnt = _merge(oa_p.reshape(T, W_MIX), ob_p, gt_p, xp, m_p, tpm, *merge_w,
                                   jnp.zeros((1, N_EXPERTS), F32), tm)

    hg_s, bq_s, k_s, v_s, gt_s, k4_s, v4_s = _inproj(xs, m_s, 1, g_pre_mix[l], w_in_b, NS)
    oa_s, s_s = _hgrn_step(hg_s, state_hgrn[l], lb, g_hgrn[l])
    ob_s = _attn_step(bq_s, k_s, v_s, cache_k, cache_v, page_table, lam, slopes, g_subln[l])
    x1_s, h3_s, rt_s, cnt = _merge(oa_s, ob_s, gt_s, xs, m_s, 1, *merge_w, cnt, NS)

    bm = MOE_BLOCK
    counts = cnt.reshape(N_EXPERTS).astype(jnp.int32)
    padded = (counts + bm - 1) // bm * bm
    pad_end = jnp.cumsum(padded)
    pad_start = pad_end - padded
    n_blocks = -(-T_all * TOP_K // bm) + N_EXPERTS
    route = jnp.concatenate([rt_p[:, :RT_GATE], rt_s[:, :RT_GATE]], axis=0)
    e_idx = route[:, RT_E:RT_E + TOP_K].astype(jnp.int32)
    rank = route[:, RT_RANK:RT_RANK + TOP_K].astype(jnp.int32)
    e_iota = jnp.arange(N_EXPERTS, dtype=jnp.int32)
    dest = rank + jnp.sum(jnp.where(e_idx[..., None] == e_iota, pad_start, 0), axis=-1)
    dest_flat = dest.reshape(-1)
    block_expert = jnp.minimum(
        jnp.sum((jnp.arange(n_blocks, dtype=jnp.int32)[:, None] * bm >= pad_end[None, :]).astype(jnp.int32), axis=1),
        N_EXPERTS - 1)
    n_used = pad_end[-1:] // bm

    x3 = _dispatch(h3_p, h3_s, dest_flat, pad_start + counts, pad_end, n_used, n_blocks * bm)
    y3 = _experts(x3, block_expert, n_used, w_gate_up[l], b_gate_up[l], w_down[l], b_down[l])
    y_p = _final(x1_p, m_p, tpm, g_post_ffn[l], rt_p, dest_flat, y3, 0, tm)
    y_s = _final(x1_s, m_s, 1, g_post_ffn[l], rt_s, dest_flat, y3, T, NS)

    return (y_p.reshape(B, L, D), y_s.reshape(NS, 1, D),
            k4_p.reshape(1, B, L, H, DH), v4_p.reshape(1, B, L, H, DH), s_p.reshape(1, B, H, DH, DH),
            k4_s.reshape(1, NS, 1, H, DH), v4_s.reshape(1, NS, 1, H, DH), s_s.reshape(1, NS, H, DH, DH))
```

```python
import functools
import math

import jax
import jax.numpy as jnp
from jax import lax
from jax.experimental import pallas as pl
from jax.experimental.pallas import tpu as pltpu

F32 = jnp.float32
BF16 = jnp.bfloat16

D = 1024
LANES = 128
SUB = D // LANES
H = 4
DH = 128
DK = 64
W_MIX = H * DH
HGRN_CHUNK = 64
N_EXPERTS = 32
TOP_K = 4
SWIGLU_LIMIT = 7.0
SWIGLU_ALPHA = 1.702
EPS = 1e-6
NEG_INF = -1e30
LOG2E = 1.4426950408889634
LAM_INIT = 0.8 - 0.6 * math.exp(-0.3 * 0)
C_HG = (0, 4 * W_MIX)
C_BQ = (4 * W_MIX, 5 * W_MIX)
C_BK = (5 * W_MIX, 6 * W_MIX)
C_BV = (6 * W_MIX, 7 * W_MIX)
C_GT = (7 * W_MIX, 7 * W_MIX + 2 * D)
D_IN = 7 * W_MIX + 2 * D
RT_E, RT_RANK, RT_GATE = 0, TOP_K, 2 * TOP_K

VMEM_LIMIT = 56 * 1024 * 1024
MOE_BLOCK = 256
DISPATCH_ROWS = 512
ATT_G = 256
ATT_NG = 4
ATT_TK = 256
ATT_VR = DH + 8


def _cp(sem, vmem=VMEM_LIMIT):
    return pltpu.CompilerParams(dimension_semantics=sem, vmem_limit_bytes=vmem)


def _sigmoid(x):
    return 1.0 / (1.0 + jnp.exp(-x))


def _rms(x, g):
    return x * lax.rsqrt(jnp.mean(x * x, axis=-1, keepdims=True) + EPS) * g


def _dot(a, b):
    return jnp.dot(a, b, preferred_element_type=F32)


def _dot_nt(a, b):
    return lax.dot_general(a, b, (((1,), (1,)), ((), ())), preferred_element_type=F32)


def _dot_tn(a, b):
    return lax.dot_general(a, b, (((0,), (0,)), ((), ())), preferred_element_type=F32)


def _store_row_tiles(ref2, x2d):
    n = x2d.shape[0]
    for j in range(SUB):
        ref2[pl.ds(j, n, stride=SUB), :] = x2d[:, j * LANES:(j + 1) * LANES]


def _load_row_tiles(ref2):
    n = ref2.shape[0] // SUB
    return jnp.concatenate([ref2[pl.ds(j, n, stride=SUB), :] for j in range(SUB)], axis=1)


def _mod_kernel(c_ref, w_ref, b_ref, o_ref):
    c = c_ref[...]
    o_ref[...] = _dot(c * _sigmoid(c), w_ref[...]) + b_ref[...]


def _modulation(c_all, w_mod, b_mod):
    n = c_all.shape[0]
    tn = 1024
    return pl.pallas_call(
        _mod_kernel,
        grid=(6 * D // tn,),
        in_specs=[pl.BlockSpec((n, D), lambda j: (0, 0)),
                  pl.BlockSpec((D, tn), lambda j: (0, j)),
                  pl.BlockSpec((1, tn), lambda j: (0, j))],
        out_specs=pl.BlockSpec((n, tn), lambda j: (0, j)),
        out_shape=jax.ShapeDtypeStruct((n, 6 * D), F32),
        compiler_params=_cp(("arbitrary",)),
        name="modulation",
    )(c_all, w_mod, b_mod.reshape(1, 6 * D))


def _inproj_kernel(x_ref, sc_ref, sh_ref, g_ref, w_ref, hg_ref, bq_ref, k_ref, v_ref, gt_ref, k4_ref, v4_ref):
    h = _rms(x_ref[...], g_ref[...]) * (1.0 + sc_ref[...]) + sh_ref[...]
    hb = h.astype(BF16)
    mm = lambda c: _dot(hb, w_ref[:, c[0]:c[1]])
    hg_ref[...] = mm(C_HG)
    bq_ref[...] = mm(C_BQ)
    gt_ref[...] = mm(C_GT)
    for c, o_ref, o4_ref in ((C_BK, k_ref, k4_ref), (C_BV, v_ref, v4_ref)):
        r = mm(c)
        o_ref[...] = r
        for hh in range(H):
            o4_ref[pl.ds(hh, r.shape[0], stride=H), :] = r[:, hh * DH:(hh + 1) * DH]


def _inproj(x2d, m3d, tiles_per_mod, g_pre, w_in_bf16, tm):
    R = x2d.shape[0]
    r = m3d.shape[1]
    mod = lambda col: pl.BlockSpec((None, r, D), lambda i: (i // tiles_per_mod, 0, col))
    row = lambda w: pl.BlockSpec((tm, w), lambda i: (i, 0))
    row4 = pl.BlockSpec((tm * H, DH), lambda i: (i, 0))
    widths = (4 * W_MIX, W_MIX, W_MIX, W_MIX, 2 * D)
    kv4 = jax.ShapeDtypeStruct((R * H, DH), F32)
    return pl.pallas_call(
        _inproj_kernel,
        grid=(R // tm,),
        in_specs=[row(D), mod(1), mod(0),
                  pl.BlockSpec((1, D), lambda i: (0, 0)),
                  pl.BlockSpec((D, D_IN), lambda i: (0, 0))],
        out_specs=[row(w) for w in widths] + [row4, row4],
        out_shape=[jax.ShapeDtypeStruct((R, w), F32) for w in widths] + [kv4, kv4],
        compiler_params=_cp(("arbitrary",)),
        name="inproj",
    )(x2d, m3d, m3d, g_pre.reshape(1, D), w_in_bf16)


def _hgrn_prompt_kernel(lb_ref, gn_ref, q_ref, f_ref, i_ref, g_ref, o_ref, sout_ref, st_ref):
    c = pl.program_id(0)
    nb = q_ref.shape[0]
    C = HGRN_CHUNK

    @pl.when(c == 0)
    def _():
        st_ref[...] = jnp.zeros_like(st_ref)

    row = lax.broadcasted_iota(jnp.int32, (C, DH), 0)
    causal = lax.broadcasted_iota(jnp.int32, (C, C), 0) >= lax.broadcasted_iota(jnp.int32, (C, C), 1)

    def per_batch(b, carry):
        for h in range(H):
            hs = slice(h * DH, (h + 1) * DH)
            lb = lb_ref[:, hs]
            q = q_ref[b, :, hs]
            f = lb + (1.0 - lb) * _sigmoid(f_ref[b, :, hs])
            k = 1.0 - f
            v = i_ref[b, :, hs]
            gc = jnp.log(f)
            for s in (1, 2, 4, 8, 16, 32):
                gc = gc + jnp.where(row >= s, pltpu.roll(gc, s, axis=0), 0.0)
            g_last = gc[C - 1:C, :]
            qd = q * jnp.exp(gc)
            kd = k * jnp.exp(-gc)
            a = jnp.where(causal, _dot_nt(qd, kd), 0.0)
            st = st_ref[b, h]
            o = _dot(a, v) + _dot_nt(qd, st)
            kv_t = _dot_tn(v, k * jnp.exp(g_last - gc))
            st_ref[b, h] = jnp.exp(g_last) * st + kv_t
            o = _rms(o, gn_ref[:, hs])
            ag = g_ref[b, :, hs]
            o_ref[b, :, hs] = o * (ag * _sigmoid(ag))
        return carry

    lax.fori_loop(0, nb, per_batch, 0)

    @pl.when(c == pl.num_programs(0) - 1)
    def _():
        def fin(b, carry):
            for h in range(H):
                sout_ref[b, h] = st_ref[b, h].T
            return carry
        lax.fori_loop(0, nb, fin, 0)


def _hgrn_prompt(hg3, lb, g_hgrn):
    B, L, _ = hg3.shape
    C = HGRN_CHUNK
    col = lambda j: pl.BlockSpec((B, C, W_MIX), lambda c: (0, c, j))
    vec = pl.BlockSpec((1, W_MIX), lambda c: (0, 0))
    return pl.pallas_call(
        _hgrn_prompt_kernel,
        grid=(L // C,),
        in_specs=[vec, vec, col(0), col(1), col(2), col(3)],
        out_specs=[pl.BlockSpec((B, C, W_MIX), lambda c: (0, c, 0)),
                   pl.BlockSpec((B, H, DH, DH), lambda c: (0, 0, 0, 0))],
        out_shape=[jax.ShapeDtypeStruct((B, L, W_MIX), F32),
                   jax.ShapeDtypeStruct((B, H, DH, DH), F32)],
        scratch_shapes=[pltpu.VMEM((B, H, DH, DH), F32)],
        compiler_params=_cp(("arbitrary",)),
        name="hgrn_prompt",
    )(lb.reshape(1, W_MIX), g_hgrn.reshape(1, W_MIX), hg3, hg3, hg3, hg3)


def _hgrn_step_kernel(lb_ref, gn_ref, x_ref, s_ref, o_ref, sout_ref):
    nt = x_ref.shape[0]
    eye = lax.broadcasted_iota(jnp.int32, (DH, DH), 0) == lax.broadcasted_iota(jnp.int32, (DH, DH), 1)

    def col(r):
        return jnp.sum(jnp.where(eye, jnp.broadcast_to(r, (DH, DH)), 0.0), axis=1, keepdims=True)

    for t in range(nt):
        for h in range(H):
            seg = lambda j: x_ref[t:t + 1, j * W_MIX + h * DH: j * W_MIX + (h + 1) * DH]
            lb = lb_ref[:, h * DH:(h + 1) * DH]
            q, v, ag = seg(0), seg(2), seg(3)
            f = lb + (1.0 - lb) * _sigmoid(seg(1))
            s_new = col(f) * s_ref[t, h] + col(1.0 - f) * v
            sout_ref[t, h] = s_new
            o = jnp.sum(col(q) * s_new, axis=0, keepdims=True)
            o = _rms(o, gn_ref[:, h * DH:(h + 1) * DH])
            o_ref[t:t + 1, h * DH:(h + 1) * DH] = o * (ag * _sigmoid(ag))


def _hgrn_step(hg, state, lb, g_hgrn):
    N = hg.shape[0]
    nt = 8
    vec = pl.BlockSpec((1, W_MIX), lambda i: (0, 0))
    st = pl.BlockSpec((nt, H, DH, DH), lambda i: (i, 0, 0, 0))
    return pl.pallas_call(
        _hgrn_step_kernel,
        grid=(N // nt,),
        in_specs=[vec, vec, pl.BlockSpec((nt, 4 * W_MIX), lambda i: (i, 0)), st],
        out_specs=[pl.BlockSpec((nt, W_MIX), lambda i: (i, 0)), st],
        out_shape=[jax.ShapeDtypeStruct((N, W_MIX), F32),
                   jax.ShapeDtypeStruct((N, H, DH, DH), F32)],
        compiler_params=_cp(("arbitrary",)),
        name="hgrn_step",
    )(lb.reshape(1, W_MIX), g_hgrn.reshape(1, W_MIX), hg, state)


def _attn_prompt_kernel(lam_ref, sl_ref, q_ref, k_ref, v_ref, g_ref, o_ref, qt_ref, vt_ref, *chain_refs):
    G, NG, tk = ATT_G, ATT_NG, ATT_TK
    tq = G * NG
    m_refs = [chain_refs[mp * NG:(mp + 1) * NG] for mp in range(2)]
    acc_refs = [chain_refs[(2 + mp) * NG:(3 + mp) * NG] for mp in range(2)]
    slope = sl_ref[pl.program_id(1)] * LOG2E
    lam = lam_ref[0]
    L = q_ref.shape[0]
    q_t = q_ref[...].T * (DK ** -0.5 * LOG2E)
    sub = lax.broadcasted_iota(jnp.int32, (DH, L), 0)
    for mp in range(2):
        qm_t = jnp.where(sub < DK, q_t, 0.0) if mp == 0 else jnp.where(sub >= DK, q_t, 0.0)
        for t in range(L // G):
            qt_ref[mp, t] = qm_t[:, t * G:(t + 1) * G]
    v_t = v_ref[...].T
    for kb in range(L // tk):
        vt_ref[kb, :DH, :] = v_t[:, kb * tk:(kb + 1) * tk]
        vt_ref[kb, DH:, :] = jnp.ones((ATT_VR - DH, tk), F32)
    idiff = lax.broadcasted_iota(jnp.int32, (tk, G), 1) - lax.broadcasted_iota(jnp.int32, (tk, G), 0)
    dmat = idiff.astype(F32) * slope

    def fold(qi, kb, chains):
        k0 = pl.multiple_of(kb * tk, tk)
        kt = k_ref[pl.ds(k0, tk), :]
        scores = [_dot(kt, qt_ref[mp, qi * NG + g]) for mp, g, _ in chains]
        probs = []
        for (mp, g, thresh), s in zip(chains, scores):
            s = s - dmat
            if thresh is not None:
                s = jnp.where(idiff >= thresh, s, NEG_INF)
            c = slope * ((qi * NG + g) * G - k0).astype(F32)
            m = m_refs[mp][g][...]
            m_new = jnp.maximum(m, jnp.max(s, axis=0, keepdims=True) - c)
            m_refs[mp][g][...] = m_new
            probs.append((jnp.exp2(s - (m_new + c)), jnp.exp2(m - m_new)))
        va = vt_ref[kb]
        pvs = [_dot(va, p) for p, _ in probs]
        for (mp, g, _), (_, alpha), pv in zip(chains, probs, pvs):
            acc_refs[mp][g][...] = alpha * acc_refs[mp][g][...] + pv

    def q_tile(qi, carry):
        for mp in range(2):
            for g in range(NG):
                m_refs[mp][g][...] = jnp.full((1, G), NEG_INF, F32)
                acc_refs[mp][g][...] = jnp.zeros((ATT_VR, G), F32)

        every = [(mp, g, None) for mp in range(2) for g in range(NG)]
        n_full = qi * (tq // tk)
        lax.fori_loop(0, n_full, lambda kb, c: (fold(qi, kb, every), c)[1], 0)
        for j in range(tq // tk):
            chains = []
            for mp in range(2):
                for g in range(NG):
                    thresh = j * tk - G * g
                    if thresh <= G - 1:
                        chains.append((mp, g, None if thresh <= -(tk - 1) else thresh))
            fold(qi, n_full + j, chains)


        for g in range(NG):
            a0, a1 = acc_refs[0][g][...], acc_refs[1][g][...]
            o_t = a0[:DH] / a0[DH:DH + 1] - lam * (a1[:DH] / a1[DH:DH + 1])
            o_t = o_t * lax.rsqrt(jnp.mean(o_t * o_t, axis=0, keepdims=True) + EPS)
            q0 = pl.multiple_of((qi * NG + g) * G, G)
            o_ref[pl.ds(q0, G), :] = o_t.T * (g_ref[...] * (1.0 - LAM_INIT))
        return carry

    lax.fori_loop(0, L // tq, q_tile, 0)


def _attn_prompt(bq, bk, bv, lam, slopes, g_subln, B, L):
    smem = pl.BlockSpec(memory_space=pltpu.SMEM)
    spec = pl.BlockSpec((L, DH), lambda b, h: (b, h))
    return pl.pallas_call(
        _attn_prompt_kernel,
        grid=(B, H),
        in_specs=[smem, smem, spec, spec, spec, pl.BlockSpec((1, DH), lambda b, h: (0, 0))],
        out_specs=spec,
        out_shape=jax.ShapeDtypeStruct((B * L, W_MIX), F32),
        scratch_shapes=[pltpu.VMEM((2, L // ATT_G, DH, ATT_G), F32),
                        pltpu.VMEM((L // ATT_TK, ATT_VR, ATT_TK), F32)]
                       + [pltpu.VMEM((1, ATT_G), F32)] * (2 * ATT_NG)
                       + [pltpu.VMEM((ATT_VR, ATT_G), F32)] * (2 * ATT_NG),
        compiler_params=_cp(("arbitrary", "arbitrary")),
        name="attn_prompt",
    )(lam.reshape(1), slopes, bq, bk, bv, g_subln.reshape(1, DH))


def _attn_step_kernel(pt_ref, lam_ref, sl_ref, q_ref, kn_ref, vn_ref, g_ref, *refs, n_pages, page):
    k_refs, v_refs, o_ref = refs[:n_pages], refs[n_pages:2 * n_pages], refs[2 * n_pages]
    PR = page * H
    past = n_pages * page
    lam = lam_ref[0]
    R = 2 * H
    grp = lax.broadcasted_iota(jnp.int32, (R, W_MIX), 1) // DK
    rid = lax.broadcasted_iota(jnp.int32, (R, W_MIX), 0)
    qbd = jnp.where(grp == rid, jnp.broadcast_to(q_ref[...], (R, W_MIX)), 0.0) * (DK ** -0.5)
    s_new = jnp.sum(qbd * kn_ref[...], axis=1, keepdims=True)
    qx = sum(qbd[:, hh * DH:(hh + 1) * DH] for hh in range(H))
    rhead = lax.broadcasted_iota(jnp.int32, (R, DH), 0) // 2
    vnx = sum(jnp.where(rhead == hh, vn_ref[:, hh * DH:(hh + 1) * DH], 0.0) for hh in range(H))
    slope = jnp.concatenate([jnp.full((2, 1), sl_ref[hh], F32) for hh in range(H)], axis=0)

    col = lax.broadcasted_iota(jnp.int32, (R, n_pages * PR), 1)
    own = (col % H) == (lax.broadcasted_iota(jnp.int32, (R, n_pages * PR), 0) // 2)
    s = jnp.concatenate([_dot_nt(qx, k_refs[j][...]) for j in range(n_pages)], axis=1)
    s = jnp.where(own, s - slope * (past - col // H).astype(F32), NEG_INF)
    m = jnp.maximum(jnp.max(s, axis=1, keepdims=True), s_new)
    p = jnp.exp(s - m)
    p_new = jnp.exp(s_new - m)
    inv_l = 1.0 / (jnp.sum(p, axis=1, keepdims=True) + p_new)
    p = p * inv_l
    o8 = (p_new * inv_l) * vnx
    for j in range(n_pages):
        o8 = o8 + _dot(p[:, j * PR:(j + 1) * PR], v_refs[j][...])
    for hh in range(H):
        o = o8[2 * hh:2 * hh + 1, :] - lam * o8[2 * hh + 1:2 * hh + 2, :]
        o_ref[:, hh * DH:(hh + 1) * DH] = _rms(o, g_ref[...]) * (1.0 - LAM_INIT)


def _attn_step(bq, k_new, v_new, cache_k, cache_v, page_table, lam, slopes, g_subln):
    N = bq.shape[0]
    n_pool, page = cache_k.shape[0], cache_k.shape[1]
    n_pages = page_table.shape[1]
    cache_k = cache_k.reshape(n_pool * page * H, DH)
    cache_v = cache_v.reshape(n_pool * page * H, DH)
    smem = pl.BlockSpec(memory_space=pltpu.SMEM)
    rowspec = pl.BlockSpec((None, 1, W_MIX), lambda b, pt: (b, 0, 0))
    pspec = lambda j: pl.BlockSpec((page * H, DH), lambda b, pt: (pt[b * n_pages + j], 0))
    grid_spec = pltpu.PrefetchScalarGridSpec(
        num_scalar_prefetch=1,
        grid=(N,),
        in_specs=[smem, smem, rowspec, rowspec, rowspec, pl.BlockSpec((1, DH), lambda b, pt: (0, 0))]
                 + [pspec(j) for j in range(n_pages)] * 2,
        out_specs=rowspec,
    )
    out = pl.pallas_call(
        functools.partial(_attn_step_kernel, n_pages=n_pages, page=page),
        grid_spec=grid_spec,
        out_shape=jax.ShapeDtypeStruct((N, 1, W_MIX), F32),
        compiler_params=_cp(("arbitrary",)),
        name="attn_step",
    )(page_table.reshape(-1), lam.reshape(1), slopes, bq.reshape(N, 1, W_MIX), k_new.reshape(N, 1, W_MIX),
      v_new.reshape(N, 1, W_MIX), g_subln.reshape(1, DH), *([cache_k] * n_pages), *([cache_v] * n_pages))
    return out.reshape(N, W_MIX)


def _merge_kernel(oa_ref, ob_ref, gt_ref, x_ref, g1_ref, sc2_ref, sh2_ref, gpost_ref, gpre_ref,
                  wa_ref, wb_ref, wo_ref, wr_ref, br_ref, cnt0_ref, x1_ref, h3_ref, rt_ref, cnt_ref, run_ref):
    i = pl.program_id(0)
    tm = x_ref.shape[0]

    @pl.when(i == 0)
    def _():
        run_ref[...] = cnt0_ref[...]

    ga = _sigmoid(gt_ref[:, :D])
    gb = _sigmoid(gt_ref[:, D:])
    merged = ga * _dot(oa_ref[...], wa_ref[...]) + gb * _dot(ob_ref[...], wb_ref[...])
    x1 = x_ref[...] + g1_ref[...] * _rms(_dot(merged, wo_ref[...]), gpost_ref[...])
    x1_ref[...] = x1
    h2 = _rms(x1, gpre_ref[...]) * (1.0 + sc2_ref[...]) + sh2_ref[...]
    _store_row_tiles(h3_ref, h2)

    logits = _dot(h2, wr_ref[...]) + br_ref[...]
    lane = lax.broadcasted_iota(jnp.int32, (tm, N_EXPERTS), 1).astype(F32)
    work = logits
    sels, vals, idxs = [], [], []
    for _ in range(TOP_K):
        mx = jnp.max(work, axis=1, keepdims=True)
        idx = jnp.min(jnp.where(work == mx, lane, float(N_EXPERTS)), axis=1, keepdims=True)
        sel = lane == idx
        work = jnp.where(sel, -jnp.inf, work)
        sels.append(sel)
        vals.append(mx)
        idxs.append(idx)
    ex = [jnp.exp(v - vals[0]) for v in vals]
    inv_den = 1.0 / (ex[0] + ex[1] + ex[2] + ex[3])

    chosen = sum(s.astype(F32) for s in sels)
    before = (lax.broadcasted_iota(jnp.int32, (tm, tm), 0) > lax.broadcasted_iota(jnp.int32, (tm, tm), 1))
    prior = _dot(before.astype(BF16), chosen.astype(BF16)) + run_ref[...]
    run_ref[...] = run_ref[...] + jnp.sum(chosen, axis=0, keepdims=True)
    cnt_ref[...] = run_ref[...]

    rlane = lax.broadcasted_iota(jnp.int32, (tm, LANES), 1)
    rt = jnp.zeros((tm, LANES), F32)
    for kk in range(TOP_K):
        rank = jnp.sum(jnp.where(sels[kk], prior, 0.0), axis=1, keepdims=True)
        rt = jnp.where(rlane == RT_E + kk, idxs[kk], rt)
        rt = jnp.where(rlane == RT_RANK + kk, rank, rt)
        rt = jnp.where(rlane == RT_GATE + kk, ex[kk] * inv_den, rt)
    rt_ref[...] = rt


def _merge(oa, ob, gt, x2d, m3d, tiles_per_mod, g_post, g_pre, wa, wb, wo, wr, br, cnt0, tm):
    R = x2d.shape[0]
    r = m3d.shape[1]
    mod = lambda col: pl.BlockSpec((None, r, D), lambda i: (i // tiles_per_mod, 0, col))
    row = lambda w: pl.BlockSpec((tm, w), lambda i: (i, 0))
    full = lambda a, b: pl.BlockSpec((a, b), lambda i: (0, 0))
    return pl.pallas_call(
        _merge_kernel,
        grid=(R // tm,),
        in_specs=[row(W_MIX), row(W_MIX), row(2 * D), row(D), mod(2), mod(4), mod(3),
                  full(1, D), full(1, D), full(W_MIX, D), full(W_MIX, D), full(D, D),
                  full(D, N_EXPERTS), full(1, N_EXPERTS), full(1, N_EXPERTS)],
        out_specs=[row(D), pl.BlockSpec((tm * SUB, LANES), lambda i: (i, 0)), row(LANES), full(1, N_EXPERTS)],
        out_shape=[jax.ShapeDtypeStruct((R, D), F32),
                   jax.ShapeDtypeStruct((R * SUB, LANES), F32),
                   jax.ShapeDtypeStruct((R, LANES), F32),
                   jax.ShapeDtypeStruct((1, N_EXPERTS), F32)],
        scratch_shapes=[pltpu.VMEM((1, N_EXPERTS), F32)],
        compiler_params=_cp(("arbitrary",)),
        name="merge",
    )(oa, ob, gt, x2d, m3d, m3d, m3d, g_post.reshape(1, D), g_pre.reshape(1, D), wa, wb, wo, wr,
      br.reshape(1, N_EXPERTS), cnt0)


def _row_copy(src, s_row, dst, d_row, sem):
    return pltpu.make_async_copy(src.at[pl.ds(pl.multiple_of(s_row, SUB), SUB)],
                                 dst.at[pl.ds(pl.multiple_of(d_row, SUB), SUB)], sem)


def _dispatch_kernel(lo_ref, hi_ref, nu_ref, da_ref, db_ref, ha_ref, hb_ref, x_hbm, zero_ref, sem):
    i = pl.program_id(0)
    last = pl.num_programs(0) - 1
    unroll = 8

    def scatter_rows(d_ref, h_ref):
        n = h_ref.shape[0] // SUB

        def issue(j, carry):
            for u in range(unroll):
                t = j * unroll + u
                for kk in range(TOP_K):
                    _row_copy(h_ref, t * SUB, x_hbm, d_ref[t * TOP_K + kk], sem).start(priority=kk % 2)
            return carry

        lax.fori_loop(0, n // unroll, issue, 0)
        for _ in range(TOP_K):
            pltpu.make_async_copy(h_ref, x_hbm.at[pl.ds(0, n * SUB)], sem).wait()

    @pl.when(i < last)
    def _():
        scatter_rows(da_ref, ha_ref)

    @pl.when(i == last)
    def _():
        scatter_rows(db_ref, hb_ref)
        zero_ref[...] = jnp.zeros_like(zero_ref)

        def per_expert(e, carry):
            zrow = lambda r: _row_copy(zero_ref, 0, x_hbm, r * SUB, sem)
            lax.fori_loop(lo_ref[e], hi_ref[e], lambda r, c: (zrow(r).start(), c)[1], 0)
            lax.fori_loop(lo_ref[e], hi_ref[e], lambda r, c: (zrow(r).wait(), c)[1], 0)
            return carry

        lax.fori_loop(0, N_EXPERTS, per_expert, 0)

        blk = zero_ref.shape[0]
        zblk = lambda b: pltpu.make_async_copy(zero_ref, x_hbm.at[pl.ds(pl.multiple_of(b * blk, blk), blk)], sem)
        lax.fori_loop(nu_ref[0], x_hbm.shape[0] // blk, lambda b, c: (zblk(b).start(), c)[1], 0)
        lax.fori_loop(nu_ref[0], x_hbm.shape[0] // blk, lambda b, c: (zblk(b).wait(), c)[1], 0)


def _dispatch(h3_a, h3_b, dest8_a, dest8_b, pad_lo, pad_hi, n_used, n_rows):
    tm = DISPATCH_ROWS
    n_a = h3_a.shape[0] // (tm * SUB)
    clamp = lambda i: jnp.minimum(i, n_a - 1)
    grid_spec = pltpu.PrefetchScalarGridSpec(
        num_scalar_prefetch=3,
        grid=(n_a + 1,),
        in_specs=[pl.BlockSpec((tm * TOP_K,), lambda i, *_: (clamp(i),), memory_space=pltpu.SMEM),
                  pl.BlockSpec(dest8_b.shape, lambda i, *_: (0,), memory_space=pltpu.SMEM),
                  pl.BlockSpec((tm * SUB, LANES), lambda i, *_: (clamp(i), 0)),
                  pl.BlockSpec(h3_b.shape, lambda i, *_: (0, 0))],
        out_specs=pl.BlockSpec(memory_space=pl.ANY),
        scratch_shapes=[pltpu.VMEM((MOE_BLOCK * SUB, LANES), F32), pltpu.SemaphoreType.DMA],
    )
    return pl.pallas_call(
        _dispatch_kernel,
        grid_spec=grid_spec,
        out_shape=jax.ShapeDtypeStruct((n_rows * SUB, LANES), F32),
        compiler_params=_cp(("arbitrary",)),
        name="dispatch",
    )(pad_lo, pad_hi, n_used, dest8_a, dest8_b, h3_a, h3_b)


def _expert_kernel(be_ref, nu_ref, x_ref, wgu_ref, bgu_ref, wd_ref, bd_ref, y_ref, wgu_b, wd_b):
    i = pl.program_id(0)

    @pl.when(i < nu_ref[0])
    def _():
        @pl.when((i == 0) | (be_ref[i] != be_ref[jnp.maximum(i - 1, 0)]))
        def _():
            wgu_b[...] = wgu_ref[...].astype(BF16)
            wd_b[...] = wd_ref[...].astype(BF16)

        gu = _dot(_load_row_tiles(x_ref).astype(BF16), wgu_b[...]) + bgu_ref[...]
        glu = jnp.minimum(gu[:, :D], SWIGLU_LIMIT)
        lin = jnp.clip(gu[:, D:], -SWIGLU_LIMIT, SWIGLU_LIMIT)
        act = (lin + 1.0) * glu * _sigmoid(SWIGLU_ALPHA * glu)
        _store_row_tiles(y_ref, _dot(act.astype(BF16), wd_b[...]) + bd_ref[...])

    @pl.when(i >= nu_ref[0])
    def _():
        y_ref[...] = jnp.zeros_like(y_ref)


def _experts(x3, block_expert, n_used, w_gate_up, b_gate_up, w_down, b_down):
    bm = MOE_BLOCK
    n_blocks = x3.shape[0] // (bm * SUB)
    used = lambda i, nu: jnp.minimum(i, nu[0] - 1)
    grid_spec = pltpu.PrefetchScalarGridSpec(
        num_scalar_prefetch=2,
        grid=(n_blocks,),
        in_specs=[pl.BlockSpec((bm * SUB, LANES), lambda i, be, nu: (used(i, nu), 0)),
                  pl.BlockSpec((None, D, 2 * D), lambda i, be, nu: (be[used(i, nu)], 0, 0)),
                  pl.BlockSpec((None, 1, 2 * D), lambda i, be, nu: (be[used(i, nu)], 0, 0)),
                  pl.BlockSpec((None, D, D), lambda i, be, nu: (be[used(i, nu)], 0, 0)),
                  pl.BlockSpec((None, 1, D), lambda i, be, nu: (be[used(i, nu)], 0, 0))],
        out_specs=pl.BlockSpec((bm * SUB, LANES), lambda i, be, nu: (i, 0)),
        scratch_shapes=[pltpu.VMEM((D, 2 * D), BF16), pltpu.VMEM((D, D), BF16)],
    )
    return pl.pallas_call(
        _expert_kernel,
        grid_spec=grid_spec,
        out_shape=jax.ShapeDtypeStruct(x3.shape, F32),
        compiler_params=_cp(("arbitrary",)),
        name="experts",
    )(block_expert, n_used, x3, w_gate_up, b_gate_up.reshape(N_EXPERTS, 1, 2 * D), w_down,
      b_down.reshape(N_EXPERTS, 1, D))


def _final_kernel(dcur_ref, dnext_ref, x1_ref, g2_ref, gpost_ref, rt_ref, y_hbm, o_ref, ybuf, sem):
    i = pl.program_id(0)
    n = pl.num_programs(0)
    tm = x1_ref.shape[0]
    slot = i % 2
    unroll = 8

    def issue(d_ref, s):
        def body(j, carry):
            for u in range(unroll):
                t = j * unroll + u
                for kk in range(TOP_K):
                    _row_copy(y_hbm, d_ref[t * TOP_K + kk], ybuf.at[s, kk], t * SUB, sem.at[s]).start(priority=kk % 2)
            return carry
        lax.fori_loop(0, tm // unroll, body, 0)

    @pl.when(i == 0)
    def _():
        issue(dcur_ref, 0)

    @pl.when(i + 1 < n)
    def _():
        issue(dnext_ref, 1 - slot)

    for kk in range(TOP_K):
        pltpu.make_async_copy(y_hbm.at[pl.ds(0, tm * SUB)], ybuf.at[slot, kk], sem.at[slot]).wait()

    y = jnp.zeros((tm, D), F32)
    for kk in range(TOP_K):
        y = y + rt_ref[:, RT_GATE + kk:RT_GATE + kk + 1] * _load_row_tiles(ybuf.at[slot, kk])
    o_ref[...] = x1_ref[...] + g2_ref[...] * _rms(y, gpost_ref[...])


def _final(x1, m3d, tiles_per_mod, g_post, route, dest_flat, y3, row0, tm):
    R = x1.shape[0]
    r = m3d.shape[1]
    off = row0 // tm
    nsteps = R // tm
    row = pl.BlockSpec((tm, D), lambda i: (i, 0))
    dspec = lambda nxt: pl.BlockSpec((tm * TOP_K,), lambda i: (jnp.minimum(i + nxt, nsteps - 1) + off,),
                                     memory_space=pltpu.SMEM)
    return pl.pallas_call(
        _final_kernel,
        grid=(nsteps,),
        in_specs=[dspec(0), dspec(1), row,
                  pl.BlockSpec((None, r, D), lambda i: (i // tiles_per_mod, 0, 5)),
                  pl.BlockSpec((1, D), lambda i: (0, 0)),
                  pl.BlockSpec((tm, LANES), lambda i: (i, 0)),
                  pl.BlockSpec(memory_space=pl.ANY)],
        out_specs=row,
        out_shape=jax.ShapeDtypeStruct((R, D), F32),
        scratch_shapes=[pltpu.VMEM((2, TOP_K, tm * SUB, LANES), F32), pltpu.SemaphoreType.DMA((2,))],
        compiler_params=_cp(("arbitrary",)),
        name="final",
    )(dest_flat, dest_flat, x1, m3d, g_post.reshape(1, D), route, y3)


def kernel(x_prompt, x_sample, cache_k, cache_v, state_hgrn, page_table, c_prompt, c_sample, w_mod, b_mod, g_pre_mix, g_post_mix, g_pre_ffn, g_post_ffn, w_in, hgrn_lb, g_hgrn, lam_q1, lam_k1, lam_q2, lam_k2, g_subln, w_proj_a, w_proj_b, w_out, w_router, b_router, w_gate_up, b_gate_up, w_down, b_down):
    B, L, _ = x_prompt.shape
    NS = x_sample.shape[0]
    T = B * L
    T_all = T + NS
    l = 0

    lb = jnp.cumsum(jax.nn.softmax(hgrn_lb.astype(F32), axis=0), axis=0)[l]
    lam = (jnp.exp(jnp.sum(lam_q1[l] * lam_k1[l])) - jnp.exp(jnp.sum(lam_q2[l] * lam_k2[l])) + LAM_INIT).astype(F32)
    slopes = jnp.exp2(-8.0 * jnp.arange(1, H + 1, dtype=F32) / H)

    m_all = _modulation(jnp.concatenate([c_prompt, c_sample], axis=0), w_mod[l], b_mod[l])
    m_p = m_all[:B].reshape(B, 1, 6 * D)
    m_s = m_all[B:].reshape(1, NS, 6 * D)

    w_in_b = w_in[l].astype(BF16)
    tm = 256
    tpm = L // tm
    xp = x_prompt.reshape(T, D)
    xs = x_sample.reshape(NS, D)
    merge_w = (g_post_mix[l], g_pre_ffn[l], w_proj_a[l], w_proj_b[l], w_out[l], w_router[l], b_router[l])

    hg_p, bq_p, k_p, v_p, gt_p, k4_p, v4_p = _inproj(xp, m_p, tpm, g_pre_mix[l], w_in_b, tm)
    oa_p, s_p = _hgrn_prompt(hg_p.reshape(B, L, 4 * W_MIX), lb, g_hgrn[l])
    ob_p = _attn_prompt(bq_p, k_p, v_p, lam, slopes, g_subln[l], B, L)
    x1_p, h3_p, rt_p, cnt = _merge(oa_p.reshape(T, W_MIX), ob_p, gt_p, xp, m_p, tpm, *merge_w,
                                   jnp.zeros((1, N_EXPERTS), F32), tm)

    hg_s, bq_s, k_s, v_s, gt_s, k4_s, v4_s = _inproj(xs, m_s, 1, g_pre_mix[l], w_in_b, NS)
    oa_s, s_s = _hgrn_step(hg_s, state_hgrn[l], lb, g_hgrn[l])
    ob_s = _attn_step(bq_s, k_s, v_s, cache_k[l], cache_v[l], page_table, lam, slopes, g_subln[l])
    x1_s, h3_s, rt_s, cnt = _merge(oa_s, ob_s, gt_s, xs, m_s, 1, *merge_w, cnt, NS)

    bm = MOE_BLOCK
    counts = cnt.reshape(N_EXPERTS).astype(jnp.int32)
    padded = (counts + bm - 1) // bm * bm
    pad_end = jnp.cumsum(padded)
    pad_start = pad_end - padded
    n_blocks = -(-T_all * TOP_K // bm) + N_EXPERTS
    route = jnp.concatenate([rt_p[:, :RT_GATE], rt_s[:, :RT_GATE]], axis=0)
    e_idx = route[:, RT_E:RT_E + TOP_K].astype(jnp.int32)
    rank = route[:, RT_RANK:RT_RANK + TOP_K].astype(jnp.int32)
    e_iota = jnp.arange(N_EXPERTS, dtype=jnp.int32)
    dest = rank + jnp.sum(jnp.where(e_idx[..., None] == e_iota, pad_start, 0), axis=-1)
    dest_flat = dest.reshape(-1) * SUB
    block_expert = jnp.minimum(
        jnp.sum((jnp.arange(n_blocks, dtype=jnp.int32)[:, None] * bm >= pad_end[None, :]).astype(jnp.int32), axis=1),
        N_EXPERTS - 1)
    n_used = pad_end[-1:] // bm

    x3 = _dispatch(h3_p, h3_s, dest_flat[:T * TOP_K], dest_flat[T * TOP_K:], pad_start + counts, pad_end, n_used,
                   n_blocks * bm)
    y3 = _experts(x3, block_expert, n_used, w_gate_up[l], b_gate_up[l], w_down[l], b_down[l])
    y_p = _final(x1_p, m_p, tpm, g_post_ffn[l], rt_p, dest_flat, y3, 0, tm)
    y_s = _final(x1_s, m_s, 1, g_post_ffn[l], rt_s, dest_flat, y3, T, NS)

    return (y_p.reshape(B, L, D), y_s.reshape(NS, 1, D),
            k4_p.reshape(1, B, L, H, DH), v4_p.reshape(1, B, L, H, DH), s_p.reshape(1, B, H, DH, DH),
            k4_s.reshape(1, NS, 1, H, DH), v4_s.reshape(1, NS, 1, H, DH), s_s.reshape(1, NS, H, DH, DH))
```

```python
import functools
import math

import jax
import jax.numpy as jnp
from jax import lax
from jax.experimental import pallas as pl
from jax.experimental.pallas import tpu as pltpu

F32 = jnp.float32
BF16 = jnp.bfloat16

D = 1024
LANES = 128
SUB = D // LANES
H = 4
DH = 128
DK = 64
W_MIX = H * DH
HGRN_CHUNK = 64
N_EXPERTS = 32
TOP_K = 4
SWIGLU_LIMIT = 7.0
SWIGLU_ALPHA = 1.702
EPS = 1e-6
NEG_INF = -1e30
LOG2E = 1.4426950408889634
LAM_INIT = 0.8 - 0.6 * math.exp(-0.3 * 0)
C_HG = (0, 4 * W_MIX)
C_BQ = (4 * W_MIX, 5 * W_MIX)
C_BK = (5 * W_MIX, 6 * W_MIX)
C_BV = (6 * W_MIX, 7 * W_MIX)
C_GT = (7 * W_MIX, 7 * W_MIX + 2 * D)
D_IN = 7 * W_MIX + 2 * D
RT_E, RT_RANK, RT_GATE = 0, TOP_K, 2 * TOP_K

VMEM_LIMIT = 56 * 1024 * 1024
MOE_BLOCK = 256
DISPATCH_ROWS = 512
ATT_G = 256
ATT_NG = 4
ATT_TK = 256
ATT_VR = DH + 8


def _cp(sem, vmem=VMEM_LIMIT):
    return pltpu.CompilerParams(dimension_semantics=sem, vmem_limit_bytes=vmem)


def _sigmoid(x):
    return 1.0 / (1.0 + jnp.exp(-x))


def _rms(x, g):
    return x * lax.rsqrt(jnp.mean(x * x, axis=-1, keepdims=True) + EPS) * g


def _dot(a, b):
    return jnp.dot(a, b, preferred_element_type=F32)


def _dot_nt(a, b):
    return lax.dot_general(a, b, (((1,), (1,)), ((), ())), preferred_element_type=F32)


def _dot_tn(a, b):
    return lax.dot_general(a, b, (((0,), (0,)), ((), ())), preferred_element_type=F32)


def _store_row_tiles(ref2, x2d):
    n = x2d.shape[0]
    for j in range(SUB):
        ref2[pl.ds(j, n, stride=SUB), :] = x2d[:, j * LANES:(j + 1) * LANES]


def _load_row_tiles(ref2):
    n = ref2.shape[0] // SUB
    return jnp.concatenate([ref2[pl.ds(j, n, stride=SUB), :] for j in range(SUB)], axis=1)


def _mod_kernel(c_ref, w_ref, b_ref, o_ref):
    c = c_ref[...]
    o_ref[...] = _dot(c * _sigmoid(c), w_ref[...]) + b_ref[...]


def _modulation(c_all, w_mod, b_mod):
    n = c_all.shape[0]
    tn = 1024
    return pl.pallas_call(
        _mod_kernel,
        grid=(6 * D // tn,),
        in_specs=[pl.BlockSpec((n, D), lambda j: (0, 0)),
                  pl.BlockSpec((D, tn), lambda j: (0, j)),
                  pl.BlockSpec((1, tn), lambda j: (0, j))],
        out_specs=pl.BlockSpec((n, tn), lambda j: (0, j)),
        out_shape=jax.ShapeDtypeStruct((n, 6 * D), F32),
        compiler_params=_cp(("arbitrary",)),
        name="modulation",
    )(c_all, w_mod, b_mod.reshape(1, 6 * D))


def _inproj_kernel(x_ref, sc_ref, sh_ref, g_ref, w_ref, hg_ref, bq_ref, k_ref, v_ref, gt_ref, k4_ref, v4_ref):
    h = _rms(x_ref[...], g_ref[...]) * (1.0 + sc_ref[...]) + sh_ref[...]
    hb = h.astype(BF16)
    mm = lambda c: _dot(hb, w_ref[:, c[0]:c[1]])
    hg_ref[...] = mm(C_HG)
    bq_ref[...] = mm(C_BQ)
    gt_ref[...] = mm(C_GT)
    for c, o_ref, o4_ref in ((C_BK, k_ref, k4_ref), (C_BV, v_ref, v4_ref)):
        r = mm(c)
        o_ref[...] = r
        for hh in range(H):
            o4_ref[pl.ds(hh, r.shape[0], stride=H), :] = r[:, hh * DH:(hh + 1) * DH]


def _inproj(x2d, m3d, tiles_per_mod, g_pre, w_in_bf16, tm):
    R = x2d.shape[0]
    r = m3d.shape[1]
    mod = lambda col: pl.BlockSpec((None, r, D), lambda i: (i // tiles_per_mod, 0, col))
    row = lambda w: pl.BlockSpec((tm, w), lambda i: (i, 0))
    row4 = pl.BlockSpec((tm * H, DH), lambda i: (i, 0))
    widths = (4 * W_MIX, W_MIX, W_MIX, W_MIX, 2 * D)
    kv4 = jax.ShapeDtypeStruct((R * H, DH), F32)
    return pl.pallas_call(
        _inproj_kernel,
        grid=(R // tm,),
        in_specs=[row(D), mod(1), mod(0),
                  pl.BlockSpec((1, D), lambda i: (0, 0)),
                  pl.BlockSpec((D, D_IN), lambda i: (0, 0))],
        out_specs=[row(w) for w in widths] + [row4, row4],
        out_shape=[jax.ShapeDtypeStruct((R, w), F32) for w in widths] + [kv4, kv4],
        compiler_params=_cp(("arbitrary",)),
        name="inproj",
    )(x2d, m3d, m3d, g_pre.reshape(1, D), w_in_bf16)


def _hgrn_prompt_kernel(lb_ref, gn_ref, q_ref, f_ref, i_ref, g_ref, o_ref, sout_ref, st_ref):
    c = pl.program_id(0)
    nb = q_ref.shape[0]
    C = HGRN_CHUNK

    @pl.when(c == 0)
    def _():
        st_ref[...] = jnp.zeros_like(st_ref)

    row = lax.broadcasted_iota(jnp.int32, (C, DH), 0)
    causal = lax.broadcasted_iota(jnp.int32, (C, C), 0) >= lax.broadcasted_iota(jnp.int32, (C, C), 1)

    def per_batch(b, carry):
        hs = [slice(h * DH, (h + 1) * DH) for h in range(H)]
        qd, kd, k2, v, eg = [], [], [], [], []
        for h in range(H):
            lb = lb_ref[:, hs[h]]
            f = lb + (1.0 - lb) * _sigmoid(f_ref[b, :, hs[h]])
            k = 1.0 - f
            gc = jnp.log(f)
            for s in (1, 2, 4, 8, 16, 32):
                gc = gc + jnp.where(row >= s, pltpu.roll(gc, s, axis=0), 0.0)
            g_last = gc[C - 1:C, :]
            qd.append(q_ref[b, :, hs[h]] * jnp.exp(gc))
            kd.append(k * jnp.exp(-gc))
            k2.append(k * jnp.exp(g_last - gc))
            eg.append(jnp.exp(g_last))
            v.append(i_ref[b, :, hs[h]])
        st = [st_ref[b, h] for h in range(H)]
        a = [jnp.where(causal, _dot_nt(qd[h], kd[h]), 0.0) for h in range(H)]
        o_inter = [_dot_nt(qd[h], st[h]) for h in range(H)]
        kv_t = [_dot_tn(v[h], k2[h]) for h in range(H)]
        o_intra = [_dot(a[h], v[h]) for h in range(H)]
        for h in range(H):
            st_ref[b, h] = eg[h] * st[h] + kv_t[h]
            o = _rms(o_intra[h] + o_inter[h], gn_ref[:, hs[h]])
            ag = g_ref[b, :, hs[h]]
            o_ref[b, :, hs[h]] = o * (ag * _sigmoid(ag))
        return carry

    lax.fori_loop(0, nb, per_batch, 0)

    @pl.when(c == pl.num_programs(0) - 1)
    def _():
        def fin(b, carry):
            for h in range(H):
                sout_ref[b, h] = st_ref[b, h].T
            return carry
        lax.fori_loop(0, nb, fin, 0)


def _hgrn_prompt(hg3, lb, g_hgrn):
    B, L, _ = hg3.shape
    C = HGRN_CHUNK
    col = lambda j: pl.BlockSpec((B, C, W_MIX), lambda c: (0, c, j))
    vec = pl.BlockSpec((1, W_MIX), lambda c: (0, 0))
    return pl.pallas_call(
        _hgrn_prompt_kernel,
        grid=(L // C,),
        in_specs=[vec, vec, col(0), col(1), col(2), col(3)],
        out_specs=[pl.BlockSpec((B, C, W_MIX), lambda c: (0, c, 0)),
                   pl.BlockSpec((B, H, DH, DH), lambda c: (0, 0, 0, 0))],
        out_shape=[jax.ShapeDtypeStruct((B, L, W_MIX), F32),
                   jax.ShapeDtypeStruct((B, H, DH, DH), F32)],
        scratch_shapes=[pltpu.VMEM((B, H, DH, DH), F32)],
        compiler_params=_cp(("arbitrary",)),
        name="hgrn_prompt",
    )(lb.reshape(1, W_MIX), g_hgrn.reshape(1, W_MIX), hg3, hg3, hg3, hg3)


def _hgrn_step_kernel(lb_ref, gn_ref, x_ref, s_ref, o_ref, sout_ref):
    nt = x_ref.shape[0]
    eye = lax.broadcasted_iota(jnp.int32, (DH, DH), 0) == lax.broadcasted_iota(jnp.int32, (DH, DH), 1)

    def col(r):
        return jnp.sum(jnp.where(eye, jnp.broadcast_to(r, (DH, DH)), 0.0), axis=1, keepdims=True)

    for t in range(nt):
        for h in range(H):
            seg = lambda j: x_ref[t:t + 1, j * W_MIX + h * DH: j * W_MIX + (h + 1) * DH]
            lb = lb_ref[:, h * DH:(h + 1) * DH]
            q, v, ag = seg(0), seg(2), seg(3)
            f = lb + (1.0 - lb) * _sigmoid(seg(1))
            s_new = col(f) * s_ref[t, h] + col(1.0 - f) * v
            sout_ref[t, h] = s_new
            o = jnp.sum(col(q) * s_new, axis=0, keepdims=True)
            o = _rms(o, gn_ref[:, h * DH:(h + 1) * DH])
            o_ref[t:t + 1, h * DH:(h + 1) * DH] = o * (ag * _sigmoid(ag))


def _hgrn_step(hg, state, lb, g_hgrn):
    N = hg.shape[0]
    nt = 8
    vec = pl.BlockSpec((1, W_MIX), lambda i: (0, 0))
    st = pl.BlockSpec((nt, H, DH, DH), lambda i: (i, 0, 0, 0))
    return pl.pallas_call(
        _hgrn_step_kernel,
        grid=(N // nt,),
        in_specs=[vec, vec, pl.BlockSpec((nt, 4 * W_MIX), lambda i: (i, 0)), st],
        out_specs=[pl.BlockSpec((nt, W_MIX), lambda i: (i, 0)), st],
        out_shape=[jax.ShapeDtypeStruct((N, W_MIX), F32),
                   jax.ShapeDtypeStruct((N, H, DH, DH), F32)],
        compiler_params=_cp(("arbitrary",)),
        name="hgrn_step",
    )(lb.reshape(1, W_MIX), g_hgrn.reshape(1, W_MIX), hg, state)


def _attn_prompt_kernel(lam_ref, sl_ref, q_ref, k_ref, v_ref, g_ref, o_ref, qt_ref, vt_ref, *chain_refs):
    G, NG, tk = ATT_G, ATT_NG, ATT_TK
    tq = G * NG
    m_refs = [chain_refs[mp * NG:(mp + 1) * NG] for mp in range(2)]
    acc_refs = [chain_refs[(2 + mp) * NG:(3 + mp) * NG] for mp in range(2)]
    slope = sl_ref[pl.program_id(1)] * LOG2E
    lam = lam_ref[0]
    L = q_ref.shape[0]
    q_t = q_ref[...].T * (DK ** -0.5 * LOG2E)
    sub = lax.broadcasted_iota(jnp.int32, (DH, L), 0)
    for mp in range(2):
        qm_t = jnp.where(sub < DK, q_t, 0.0) if mp == 0 else jnp.where(sub >= DK, q_t, 0.0)
        for t in range(L // G):
            qt_ref[mp, t] = qm_t[:, t * G:(t + 1) * G]
    v_t = v_ref[...].T
    for kb in range(L // tk):
        vt_ref[kb, :DH, :] = v_t[:, kb * tk:(kb + 1) * tk]
        vt_ref[kb, DH:, :] = jnp.ones((ATT_VR - DH, tk), F32)
    idiff = lax.broadcasted_iota(jnp.int32, (tk, G), 1) - lax.broadcasted_iota(jnp.int32, (tk, G), 0)
    dmat = idiff.astype(F32) * slope

    def fold(qi, kb, chains):
        k0 = pl.multiple_of(kb * tk, tk)
        kt = k_ref[pl.ds(k0, tk), :]
        scores = [_dot(kt, qt_ref[mp, qi * NG + g]) for mp, g, _ in chains]
        probs = []
        for (mp, g, thresh), s in zip(chains, scores):
            s = s - dmat
            if thresh is not None:
                s = jnp.where(idiff >= thresh, s, NEG_INF)
            c = slope * ((qi * NG + g) * G - k0).astype(F32)
            m = m_refs[mp][g][...]
            m_new = jnp.maximum(m, jnp.max(s, axis=0, keepdims=True) - c)
            m_refs[mp][g][...] = m_new
            probs.append((jnp.exp2(s - (m_new + c)), jnp.exp2(m - m_new)))
        va = vt_ref[kb]
        pvs = [_dot(va, p) for p, _ in probs]
        for (mp, g, _), (_, alpha), pv in zip(chains, probs, pvs):
            acc_refs[mp][g][...] = alpha * acc_refs[mp][g][...] + pv

    def q_tile(qi, carry):
        for mp in range(2):
            for g in range(NG):
                m_refs[mp][g][...] = jnp.full((1, G), NEG_INF, F32)
                acc_refs[mp][g][...] = jnp.zeros((ATT_VR, G), F32)

        every = [(mp, g, None) for mp in range(2) for g in range(NG)]
        n_full = qi * (tq // tk)
        lax.fori_loop(0, n_full, lambda kb, c: (fold(qi, kb, every), c)[1], 0)
        for j in range(tq // tk):
            chains = []
            for mp in range(2):
                for g in range(NG):
                    thresh = j * tk - G * g
                    if thresh <= G - 1:
                        chains.append((mp, g, None if thresh <= -(tk - 1) else thresh))
            fold(qi, n_full + j, chains)


        for g in range(NG):
            a0, a1 = acc_refs[0][g][...], acc_refs[1][g][...]
            o_t = a0[:DH] / a0[DH:DH + 1] - lam * (a1[:DH] / a1[DH:DH + 1])
            o_t = o_t * lax.rsqrt(jnp.mean(o_t * o_t, axis=0, keepdims=True) + EPS)
            q0 = pl.multiple_of((qi * NG + g) * G, G)
            o_ref[pl.ds(q0, G), :] = o_t.T * (g_ref[...] * (1.0 - LAM_INIT))
        return carry

    lax.fori_loop(0, L // tq, q_tile, 0)


def _attn_prompt(bq, bk, bv, lam, slopes, g_subln, B, L):
    smem = pl.BlockSpec(memory_space=pltpu.SMEM)
    spec = pl.BlockSpec((L, DH), lambda b, h: (b, h))
    return pl.pallas_call(
        _attn_prompt_kernel,
        grid=(B, H),
        in_specs=[smem, smem, spec, spec, spec, pl.BlockSpec((1, DH), lambda b, h: (0, 0))],
        out_specs=spec,
        out_shape=jax.ShapeDtypeStruct((B * L, W_MIX), F32),
        scratch_shapes=[pltpu.VMEM((2, L // ATT_G, DH, ATT_G), F32),
                        pltpu.VMEM((L // ATT_TK, ATT_VR, ATT_TK), F32)]
                       + [pltpu.VMEM((1, ATT_G), F32)] * (2 * ATT_NG)
                       + [pltpu.VMEM((ATT_VR, ATT_G), F32)] * (2 * ATT_NG),
        compiler_params=_cp(("arbitrary", "arbitrary")),
        name="attn_prompt",
    )(lam.reshape(1), slopes, bq, bk, bv, g_subln.reshape(1, DH))


def _attn_step_kernel(pt_ref, lam_ref, sl_ref, q_ref, kn_ref, vn_ref, g_ref, *refs, n_pages, page):
    k_refs, v_refs, o_ref = refs[:n_pages], refs[n_pages:2 * n_pages], refs[2 * n_pages]
    PR = page * H
    past = n_pages * page
    lam = lam_ref[0]
    R = 2 * H
    grp = lax.broadcasted_iota(jnp.int32, (R, W_MIX), 1) // DK
    rid = lax.broadcasted_iota(jnp.int32, (R, W_MIX), 0)
    qbd = jnp.where(grp == rid, jnp.broadcast_to(q_ref[...], (R, W_MIX)), 0.0) * (DK ** -0.5)
    s_new = jnp.sum(qbd * kn_ref[...], axis=1, keepdims=True)
    qx = sum(qbd[:, hh * DH:(hh + 1) * DH] for hh in range(H))
    rhead = lax.broadcasted_iota(jnp.int32, (R, DH), 0) // 2
    vnx = sum(jnp.where(rhead == hh, vn_ref[:, hh * DH:(hh + 1) * DH], 0.0) for hh in range(H))
    slope = jnp.concatenate([jnp.full((2, 1), sl_ref[hh], F32) for hh in range(H)], axis=0)

    col = lax.broadcasted_iota(jnp.int32, (R, n_pages * PR), 1)
    own = (col % H) == (lax.broadcasted_iota(jnp.int32, (R, n_pages * PR), 0) // 2)
    s = jnp.concatenate([_dot_nt(qx, k_refs[j][...]) for j in range(n_pages)], axis=1)
    s = jnp.where(own, s - slope * (past - col // H).astype(F32), NEG_INF)
    m = jnp.maximum(jnp.max(s, axis=1, keepdims=True), s_new)
    p = jnp.exp(s - m)
    p_new = jnp.exp(s_new - m)
    inv_l = 1.0 / (jnp.sum(p, axis=1, keepdims=True) + p_new)
    p = p * inv_l
    o8 = (p_new * inv_l) * vnx
    for j in range(n_pages):
        o8 = o8 + _dot(p[:, j * PR:(j + 1) * PR], v_refs[j][...])
    for hh in range(H):
        o = o8[2 * hh:2 * hh + 1, :] - lam * o8[2 * hh + 1:2 * hh + 2, :]
        o_ref[:, hh * DH:(hh + 1) * DH] = _rms(o, g_ref[...]) * (1.0 - LAM_INIT)


def _attn_step(bq, k_new, v_new, cache_k, cache_v, page_table, lam, slopes, g_subln):
    N = bq.shape[0]
    n_pool, page = cache_k.shape[0], cache_k.shape[1]
    n_pages = page_table.shape[1]
    cache_k = cache_k.reshape(n_pool * page * H, DH)
    cache_v = cache_v.reshape(n_pool * page * H, DH)
    smem = pl.BlockSpec(memory_space=pltpu.SMEM)
    rowspec = pl.BlockSpec((None, 1, W_MIX), lambda b, pt: (b, 0, 0))
    pspec = lambda j: pl.BlockSpec((page * H, DH), lambda b, pt: (pt[b * n_pages + j], 0))
    grid_spec = pltpu.PrefetchScalarGridSpec(
        num_scalar_prefetch=1,
        grid=(N,),
        in_specs=[smem, smem, rowspec, rowspec, rowspec, pl.BlockSpec((1, DH), lambda b, pt: (0, 0))]
                 + [pspec(j) for j in range(n_pages)] * 2,
        out_specs=rowspec,
    )
    out = pl.pallas_call(
        functools.partial(_attn_step_kernel, n_pages=n_pages, page=page),
        grid_spec=grid_spec,
        out_shape=jax.ShapeDtypeStruct((N, 1, W_MIX), F32),
        compiler_params=_cp(("arbitrary",)),
        name="attn_step",
    )(page_table.reshape(-1), lam.reshape(1), slopes, bq.reshape(N, 1, W_MIX), k_new.reshape(N, 1, W_MIX),
      v_new.reshape(N, 1, W_MIX), g_subln.reshape(1, DH), *([cache_k] * n_pages), *([cache_v] * n_pages))
    return out.reshape(N, W_MIX)


def _merge_kernel(oa_ref, ob_ref, gt_ref, x_ref, g1_ref, sc2_ref, sh2_ref, gpost_ref, gpre_ref,
                  wa_ref, wb_ref, wo_ref, wr_ref, br_ref, cnt0_ref, x1_ref, h3_ref, rt_ref, cnt_ref, run_ref):
    i = pl.program_id(0)
    tm = x_ref.shape[0]

    @pl.when(i == 0)
    def _():
        run_ref[...] = cnt0_ref[...]

    ga = _sigmoid(gt_ref[:, :D])
    gb = _sigmoid(gt_ref[:, D:])
    merged = (ga * _dot(oa_ref[...].astype(BF16), wa_ref[...])
              + gb * _dot(ob_ref[...].astype(BF16), wb_ref[...]))
    x1 = x_ref[...] + g1_ref[...] * _rms(_dot(merged.astype(BF16), wo_ref[...]), gpost_ref[...])
    x1_ref[...] = x1
    h2 = _rms(x1, gpre_ref[...]) * (1.0 + sc2_ref[...]) + sh2_ref[...]
    _store_row_tiles(h3_ref, h2)

    logits = _dot(h2, wr_ref[...]) + br_ref[...]
    lane = lax.broadcasted_iota(jnp.int32, (tm, N_EXPERTS), 1).astype(F32)
    work = logits
    sels, vals, idxs = [], [], []
    for _ in range(TOP_K):
        mx = jnp.max(work, axis=1, keepdims=True)
        idx = jnp.min(jnp.where(work == mx, lane, float(N_EXPERTS)), axis=1, keepdims=True)
        sel = lane == idx
        work = jnp.where(sel, -jnp.inf, work)
        sels.append(sel)
        vals.append(mx)
        idxs.append(idx)
    ex = [jnp.exp(v - vals[0]) for v in vals]
    inv_den = 1.0 / (ex[0] + ex[1] + ex[2] + ex[3])

    chosen = sum(s.astype(F32) for s in sels)
    before = (lax.broadcasted_iota(jnp.int32, (tm, tm), 0) > lax.broadcasted_iota(jnp.int32, (tm, tm), 1))
    prior = _dot(before.astype(BF16), chosen.astype(BF16)) + run_ref[...]
    run_ref[...] = run_ref[...] + jnp.sum(chosen, axis=0, keepdims=True)
    cnt_ref[...] = run_ref[...]

    rlane = lax.broadcasted_iota(jnp.int32, (tm, LANES), 1)
    rt = jnp.zeros((tm, LANES), F32)
    for kk in range(TOP_K):
        rank = jnp.sum(jnp.where(sels[kk], prior, 0.0), axis=1, keepdims=True)
        rt = jnp.where(rlane == RT_E + kk, idxs[kk], rt)
        rt = jnp.where(rlane == RT_RANK + kk, rank, rt)
        rt = jnp.where(rlane == RT_GATE + kk, ex[kk] * inv_den, rt)
    rt_ref[...] = rt


def _merge(oa, ob, gt, x2d, m3d, tiles_per_mod, g_post, g_pre, wa, wb, wo, wr, br, cnt0, tm):
    R = x2d.shape[0]
    r = m3d.shape[1]
    mod = lambda col: pl.BlockSpec((None, r, D), lambda i: (i // tiles_per_mod, 0, col))
    row = lambda w: pl.BlockSpec((tm, w), lambda i: (i, 0))
    full = lambda a, b: pl.BlockSpec((a, b), lambda i: (0, 0))
    return pl.pallas_call(
        _merge_kernel,
        grid=(R // tm,),
        in_specs=[row(W_MIX), row(W_MIX), row(2 * D), row(D), mod(2), mod(4), mod(3),
                  full(1, D), full(1, D), full(W_MIX, D), full(W_MIX, D), full(D, D),
                  full(D, N_EXPERTS), full(1, N_EXPERTS), full(1, N_EXPERTS)],
        out_specs=[row(D), pl.BlockSpec((tm * SUB, LANES), lambda i: (i, 0)), row(LANES), full(1, N_EXPERTS)],
        out_shape=[jax.ShapeDtypeStruct((R, D), F32),
                   jax.ShapeDtypeStruct((R * SUB, LANES), F32),
                   jax.ShapeDtypeStruct((R, LANES), F32),
                   jax.ShapeDtypeStruct((1, N_EXPERTS), F32)],
        scratch_shapes=[pltpu.VMEM((1, N_EXPERTS), F32)],
        compiler_params=_cp(("arbitrary",)),
        name="merge",
    )(oa, ob, gt, x2d, m3d, m3d, m3d, g_post.reshape(1, D), g_pre.reshape(1, D), wa, wb, wo, wr,
      br.reshape(1, N_EXPERTS), cnt0)


def _row_copy(src, s_row, dst, d_row, sem):
    return pltpu.make_async_copy(src.at[pl.ds(pl.multiple_of(s_row, SUB), SUB)],
                                 dst.at[pl.ds(pl.multiple_of(d_row, SUB), SUB)], sem)


def _dispatch_kernel(lo_ref, hi_ref, nu_ref, da_ref, db_ref, ha_ref, hb_ref, x_hbm, zero_ref, sem):
    i = pl.program_id(0)
    last = pl.num_programs(0) - 1
    unroll = 8

    def scatter_rows(d_ref, h_ref):
        n = h_ref.shape[0] // SUB

        def issue(j, carry):
            for u in range(unroll):
                t = j * unroll + u
                for kk in range(TOP_K):
                    _row_copy(h_ref, t * SUB, x_hbm, d_ref[t * TOP_K + kk], sem).start(priority=kk % 2)
            return carry

        lax.fori_loop(0, n // unroll, issue, 0)
        for _ in range(TOP_K):
            pltpu.make_async_copy(h_ref, x_hbm.at[pl.ds(0, n * SUB)], sem).wait()

    @pl.when(i < last)
    def _():
        scatter_rows(da_ref, ha_ref)

    @pl.when(i == last)
    def _():
        scatter_rows(db_ref, hb_ref)
        zero_ref[...] = jnp.zeros_like(zero_ref)

        def per_expert(e, carry):
            zrow = lambda r: _row_copy(zero_ref, 0, x_hbm, r * SUB, sem)
            lax.fori_loop(lo_ref[e], hi_ref[e], lambda r, c: (zrow(r).start(), c)[1], 0)
            lax.fori_loop(lo_ref[e], hi_ref[e], lambda r, c: (zrow(r).wait(), c)[1], 0)
            return carry

        lax.fori_loop(0, N_EXPERTS, per_expert, 0)

        blk = zero_ref.shape[0]
        zblk = lambda b: pltpu.make_async_copy(zero_ref, x_hbm.at[pl.ds(pl.multiple_of(b * blk, blk), blk)], sem)
        lax.fori_loop(nu_ref[0], x_hbm.shape[0] // blk, lambda b, c: (zblk(b).start(), c)[1], 0)
        lax.fori_loop(nu_ref[0], x_hbm.shape[0] // blk, lambda b, c: (zblk(b).wait(), c)[1], 0)


def _dispatch(h3_a, h3_b, dest8_a, dest8_b, pad_lo, pad_hi, n_used, n_rows):
    tm = DISPATCH_ROWS
    n_a = h3_a.shape[0] // (tm * SUB)
    clamp = lambda i: jnp.minimum(i, n_a - 1)
    grid_spec = pltpu.PrefetchScalarGridSpec(
        num_scalar_prefetch=3,
        grid=(n_a + 1,),
        in_specs=[pl.BlockSpec((tm * TOP_K,), lambda i, *_: (clamp(i),), memory_space=pltpu.SMEM),
                  pl.BlockSpec(dest8_b.shape, lambda i, *_: (0,), memory_space=pltpu.SMEM),
                  pl.BlockSpec((tm * SUB, LANES), lambda i, *_: (clamp(i), 0)),
                  pl.BlockSpec(h3_b.shape, lambda i, *_: (0, 0))],
        out_specs=pl.BlockSpec(memory_space=pl.ANY),
        scratch_shapes=[pltpu.VMEM((MOE_BLOCK * SUB, LANES), F32), pltpu.SemaphoreType.DMA],
    )
    return pl.pallas_call(
        _dispatch_kernel,
        grid_spec=grid_spec,
        out_shape=jax.ShapeDtypeStruct((n_rows * SUB, LANES), F32),
        compiler_params=_cp(("arbitrary",)),
        name="dispatch",
    )(pad_lo, pad_hi, n_used, dest8_a, dest8_b, h3_a, h3_b)


def _expert_kernel(b0_ref, nb_ref, nu_ref, x_hbm, wgu_ref, bgu_ref, wd_ref, bd_ref, y_hbm,
                   xbuf, ybuf, wgu_b, wd_b, xsem, ysem):
    e = pl.program_id(0)
    rows = xbuf.shape[1]
    b0, nb = b0_ref[e], nb_ref[e]
    x_copy = lambda j, slot: pltpu.make_async_copy(
        x_hbm.at[pl.ds(pl.multiple_of((b0 + j) * rows, rows), rows)], xbuf.at[slot], xsem.at[slot])
    y_copy = lambda j, slot: pltpu.make_async_copy(
        ybuf.at[slot], y_hbm.at[pl.ds(pl.multiple_of((b0 + j) * rows, rows), rows)], ysem.at[slot])

    @pl.when(nb > 0)
    def _():
        x_copy(0, 0).start()
        wgu_b[...] = wgu_ref[...].astype(BF16)
        wd_b[...] = wd_ref[...].astype(BF16)

        def block(j, carry):
            slot = j % 2
            x_copy(j, slot).wait()

            @pl.when(j + 1 < nb)
            def _():
                x_copy(j + 1, 1 - slot).start()

            @pl.when(j >= 2)
            def _():
                y_copy(j - 2, slot).wait()

            gu = _dot(_load_row_tiles(xbuf.at[slot]).astype(BF16), wgu_b[...]) + bgu_ref[...]
            glu = jnp.minimum(gu[:, :D], SWIGLU_LIMIT)
            lin = jnp.clip(gu[:, D:], -SWIGLU_LIMIT, SWIGLU_LIMIT)
            act = (lin + 1.0) * glu * _sigmoid(SWIGLU_ALPHA * glu)
            _store_row_tiles(ybuf.at[slot], _dot(act.astype(BF16), wd_b[...]) + bd_ref[...])
            y_copy(j, slot).start()
            return carry

        lax.fori_loop(0, nb, block, 0)

        @pl.when(nb >= 2)
        def _():
            y_copy(nb - 2, nb % 2).wait()
        y_copy(nb - 1, (nb - 1) % 2).wait()

    @pl.when(e == pl.num_programs(0) - 1)
    def _():
        ybuf[0] = jnp.zeros(ybuf.shape[1:], F32)
        zblk = lambda b: pltpu.make_async_copy(
            ybuf.at[0], y_hbm.at[pl.ds(pl.multiple_of(b * rows, rows), rows)], ysem.at[0])
        n_blocks = y_hbm.shape[0] // rows
        lax.fori_loop(nu_ref[0], n_blocks, lambda b, c: (zblk(b).start(), c)[1], 0)
        lax.fori_loop(nu_ref[0], n_blocks, lambda b, c: (zblk(b).wait(), c)[1], 0)


def _experts(x3, blk_start, blk_count, n_used, w_gate_up, b_gate_up, w_down, b_down):
    rows = MOE_BLOCK * SUB
    wspec = lambda a, b: pl.BlockSpec((None, a, b), lambda e, *_: (e, 0, 0))
    grid_spec = pltpu.PrefetchScalarGridSpec(
        num_scalar_prefetch=3,
        grid=(N_EXPERTS,),
        in_specs=[pl.BlockSpec(memory_space=pl.ANY), wspec(D, 2 * D), wspec(1, 2 * D), wspec(D, D), wspec(1, D)],
        out_specs=pl.BlockSpec(memory_space=pl.ANY),
        scratch_shapes=[pltpu.VMEM((2, rows, LANES), F32), pltpu.VMEM((2, rows, LANES), F32),
                        pltpu.VMEM((D, 2 * D), BF16), pltpu.VMEM((D, D), BF16),
                        pltpu.SemaphoreType.DMA((2,)), pltpu.SemaphoreType.DMA((2,))],
    )
    return pl.pallas_call(
        _expert_kernel,
        grid_spec=grid_spec,
        out_shape=jax.ShapeDtypeStruct(x3.shape, F32),
        compiler_params=_cp(("arbitrary",)),
        name="experts",
    )(blk_start, blk_count, n_used, x3, w_gate_up, b_gate_up.reshape(N_EXPERTS, 1, 2 * D), w_down,
      b_down.reshape(N_EXPERTS, 1, D))


def _final_kernel(dcur_ref, dnext_ref, x1_ref, g2_ref, gpost_ref, rt_ref, y_hbm, o_ref, ybuf, sem):
    i = pl.program_id(0)
    n = pl.num_programs(0)
    tm = x1_ref.shape[0]
    slot = i % 2
    unroll = 8

    def issue(d_ref, s):
        def body(j, carry):
            for u in range(unroll):
                t = j * unroll + u
                for kk in range(TOP_K):
                    _row_copy(y_hbm, d_ref[t * TOP_K + kk], ybuf.at[s, kk], t * SUB, sem.at[s]).start(priority=kk % 2)
            return carry
        lax.fori_loop(0, tm // unroll, body, 0)

    @pl.when(i == 0)
    def _():
        issue(dcur_ref, 0)

    @pl.when(i + 1 < n)
    def _():
        issue(dnext_ref, 1 - slot)

    for kk in range(TOP_K):
        pltpu.make_async_copy(y_hbm.at[pl.ds(0, tm * SUB)], ybuf.at[slot, kk], sem.at[slot]).wait()

    y = jnp.zeros((tm, D), F32)
    for kk in range(TOP_K):
        y = y + rt_ref[:, RT_GATE + kk:RT_GATE + kk + 1] * _load_row_tiles(ybuf.at[slot, kk])
    o_ref[...] = x1_ref[...] + g2_ref[...] * _rms(y, gpost_ref[...])


def _final(x1, m3d, tiles_per_mod, g_post, route, dest_flat, y3, row0, tm):
    R = x1.shape[0]
    r = m3d.shape[1]
    off = row0 // tm
    nsteps = R // tm
    row = pl.BlockSpec((tm, D), lambda i: (i, 0))
    dspec = lambda nxt: pl.BlockSpec((tm * TOP_K,), lambda i: (jnp.minimum(i + nxt, nsteps - 1) + off,),
                                     memory_space=pltpu.SMEM)
    return pl.pallas_call(
        _final_kernel,
        grid=(nsteps,),
        in_specs=[dspec(0), dspec(1), row,
                  pl.BlockSpec((None, r, D), lambda i: (i // tiles_per_mod, 0, 5)),
                  pl.BlockSpec((1, D), lambda i: (0, 0)),
                  pl.BlockSpec((tm, LANES), lambda i: (i, 0)),
                  pl.BlockSpec(memory_space=pl.ANY)],
        out_specs=row,
        out_shape=jax.ShapeDtypeStruct((R, D), F32),
        scratch_shapes=[pltpu.VMEM((2, TOP_K, tm * SUB, LANES), F32), pltpu.SemaphoreType.DMA((2,))],
        compiler_params=_cp(("arbitrary",)),
        name="final",
    )(dest_flat, dest_flat, x1, m3d, g_post.reshape(1, D), route, y3)


def kernel(x_prompt, x_sample, cache_k, cache_v, state_hgrn, page_table, c_prompt, c_sample, w_mod, b_mod, g_pre_mix, g_post_mix, g_pre_ffn, g_post_ffn, w_in, hgrn_lb, g_hgrn, lam_q1, lam_k1, lam_q2, lam_k2, g_subln, w_proj_a, w_proj_b, w_out, w_router, b_router, w_gate_up, b_gate_up, w_down, b_down):
    B, L, _ = x_prompt.shape
    NS = x_sample.shape[0]
    T = B * L
    T_all = T + NS
    l = 0

    lb = jnp.cumsum(jax.nn.softmax(hgrn_lb.astype(F32), axis=0), axis=0)[l]
    lam = (jnp.exp(jnp.sum(lam_q1[l] * lam_k1[l])) - jnp.exp(jnp.sum(lam_q2[l] * lam_k2[l])) + LAM_INIT).astype(F32)
    slopes = jnp.exp2(-8.0 * jnp.arange(1, H + 1, dtype=F32) / H)

    m_all = _modulation(jnp.concatenate([c_prompt, c_sample], axis=0), w_mod[l], b_mod[l])
    m_p = m_all[:B].reshape(B, 1, 6 * D)
    m_s = m_all[B:].reshape(1, NS, 6 * D)

    w_in_b = w_in[l].astype(BF16)
    tm = 256
    tpm = L // tm
    xp = x_prompt.reshape(T, D)
    xs = x_sample.reshape(NS, D)
    merge_w = (g_post_mix[l], g_pre_ffn[l], w_proj_a[l].astype(BF16), w_proj_b[l].astype(BF16),
               w_out[l].astype(BF16), w_router[l], b_router[l])

    hg_p, bq_p, k_p, v_p, gt_p, k4_p, v4_p = _inproj(xp, m_p, tpm, g_pre_mix[l], w_in_b, tm)
    oa_p, s_p = _hgrn_prompt(hg_p.reshape(B, L, 4 * W_MIX), lb, g_hgrn[l])
    ob_p = _attn_prompt(bq_p, k_p, v_p, lam, slopes, g_subln[l], B, L)
    x1_p, h3_p, rt_p, cnt = _merge(oa_p.reshape(T, W_MIX), ob_p, gt_p, xp, m_p, tpm, *merge_w,
                                   jnp.zeros((1, N_EXPERTS), F32), tm)

    hg_s, bq_s, k_s, v_s, gt_s, k4_s, v4_s = _inproj(xs, m_s, 1, g_pre_mix[l], w_in_b, NS)
    oa_s, s_s = _hgrn_step(hg_s, state_hgrn[l], lb, g_hgrn[l])
    ob_s = _attn_step(bq_s, k_s, v_s, cache_k[l], cache_v[l], page_table, lam, slopes, g_subln[l])
    x1_s, h3_s, rt_s, cnt = _merge(oa_s, ob_s, gt_s, xs, m_s, 1, *merge_w, cnt, NS)

    bm = MOE_BLOCK
    counts = cnt.reshape(N_EXPERTS).astype(jnp.int32)
    padded = (counts + bm - 1) // bm * bm
    pad_end = jnp.cumsum(padded)
    pad_start = pad_end - padded
    n_blocks = -(-T_all * TOP_K // bm) + N_EXPERTS
    route = jnp.concatenate([rt_p[:, :RT_GATE], rt_s[:, :RT_GATE]], axis=0)
    e_idx = route[:, RT_E:RT_E + TOP_K].astype(jnp.int32)
    rank = route[:, RT_RANK:RT_RANK + TOP_K].astype(jnp.int32)
    e_iota = jnp.arange(N_EXPERTS, dtype=jnp.int32)
    dest = rank + jnp.sum(jnp.where(e_idx[..., None] == e_iota, pad_start, 0), axis=-1)
    dest_flat = dest.reshape(-1) * SUB
    n_used = pad_end[-1:] // bm

    x3 = _dispatch(h3_p, h3_s, dest_flat[:T * TOP_K], dest_flat[T * TOP_K:], pad_start + counts, pad_end, n_used,
                   n_blocks * bm)
    y3 = _experts(x3, pad_start // bm, padded // bm, n_used, w_gate_up[l], b_gate_up[l], w_down[l], b_down[l])
    y_p = _final(x1_p, m_p, tpm, g_post_ffn[l], rt_p, dest_flat, y3, 0, tm)
    y_s = _final(x1_s, m_s, 1, g_post_ffn[l], rt_s, dest_flat, y3, T, NS)

    return (y_p.reshape(B, L, D), y_s.reshape(NS, 1, D),
            k4_p.reshape(1, B, L, H, DH), v4_p.reshape(1, B, L, H, DH), s_p.reshape(1, B, H, DH, DH),
            k4_s.reshape(1, NS, 1, H, DH), v4_s.reshape(1, NS, 1, H, DH), s_s.reshape(1, NS, H, DH, DH))
```

```python
import functools
import math

import jax
import jax.numpy as jnp
from jax import lax
from jax.experimental import pallas as pl
from jax.experimental.pallas import tpu as pltpu

F32 = jnp.float32
BF16 = jnp.bfloat16

D = 1024
LANES = 128
SUB = D // LANES
H = 4
DH = 128
DK = 64
W_MIX = H * DH
HGRN_CHUNK = 64
N_EXPERTS = 32
TOP_K = 4
SWIGLU_LIMIT = 7.0
SWIGLU_ALPHA = 1.702
EPS = 1e-6
NEG_INF = -1e30
LOG2E = 1.4426950408889634
LAM_INIT = 0.8 - 0.6 * math.exp(-0.3 * 0)
C_HG = (0, 4 * W_MIX)
C_BQ = (4 * W_MIX, 5 * W_MIX)
C_BK = (5 * W_MIX, 6 * W_MIX)
C_BV = (6 * W_MIX, 7 * W_MIX)
C_GT = (7 * W_MIX, 7 * W_MIX + 2 * D)
D_IN = 7 * W_MIX + 2 * D
RT_E, RT_RANK, RT_GATE = 0, TOP_K, 2 * TOP_K

VMEM_LIMIT = 56 * 1024 * 1024
MOE_BLOCK = 256
DISPATCH_ROWS = 512
MERGE_ROWS = 512
ATT_G = 256
ATT_NG = 4
ATT_TK = 256
ATT_VR = DH + 8


def _cp(sem, vmem=VMEM_LIMIT):
    return pltpu.CompilerParams(dimension_semantics=sem, vmem_limit_bytes=vmem)


def _sigmoid(x):
    return 1.0 / (1.0 + jnp.exp(-x))


def _rms(x, g):
    return x * lax.rsqrt(jnp.mean(x * x, axis=-1, keepdims=True) + EPS) * g


def _dot(a, b):
    return jnp.dot(a, b, preferred_element_type=F32)


def _dot_nt(a, b):
    return lax.dot_general(a, b, (((1,), (1,)), ((), ())), preferred_element_type=F32)


def _dot_tn(a, b):
    return lax.dot_general(a, b, (((0,), (0,)), ((), ())), preferred_element_type=F32)


def _store_row_tiles(ref2, x2d, row0=0):
    n = x2d.shape[0]
    for j in range(SUB):
        ref2[pl.ds(row0 * SUB + j, n, stride=SUB), :] = x2d[:, j * LANES:(j + 1) * LANES]


def _load_row_tiles(ref2):
    n = ref2.shape[0] // SUB
    return jnp.concatenate([ref2[pl.ds(j, n, stride=SUB), :] for j in range(SUB)], axis=1)


def _mod_kernel(c_ref, w_ref, b_ref, o_ref):
    c = c_ref[...]
    o_ref[...] = _dot(c * _sigmoid(c), w_ref[...]) + b_ref[...]


def _modulation(c_all, w_mod, b_mod):
    n = c_all.shape[0]
    tn = 1024
    return pl.pallas_call(
        _mod_kernel,
        grid=(6 * D // tn,),
        in_specs=[pl.BlockSpec((n, D), lambda j: (0, 0)),
                  pl.BlockSpec((D, tn), lambda j: (0, j)),
                  pl.BlockSpec((1, tn), lambda j: (0, j))],
        out_specs=pl.BlockSpec((n, tn), lambda j: (0, j)),
        out_shape=jax.ShapeDtypeStruct((n, 6 * D), F32),
        compiler_params=_cp(("arbitrary",)),
        name="modulation",
    )(c_all, w_mod, b_mod.reshape(1, 6 * D))


def _inproj_kernel(x_ref, sc_ref, sh_ref, g_ref, w_ref, hg_ref, bq_ref, k_ref, v_ref, gt_ref, k4_ref, v4_ref):
    h = _rms(x_ref[...], g_ref[...]) * (1.0 + sc_ref[...]) + sh_ref[...]
    hb = h.astype(BF16)
    mm = lambda c: _dot(hb, w_ref[:, c[0]:c[1]])
    hg_ref[...] = mm(C_HG)
    bq_ref[...] = mm(C_BQ)
    gt_ref[...] = mm(C_GT)
    for c, o_ref, o4_ref in ((C_BK, k_ref, k4_ref), (C_BV, v_ref, v4_ref)):
        r = mm(c)
        o_ref[...] = r
        for hh in range(H):
            o4_ref[pl.ds(hh, r.shape[0], stride=H), :] = r[:, hh * DH:(hh + 1) * DH]


def _inproj(x2d, m3d, tiles_per_mod, g_pre, w_in_bf16, tm):
    R = x2d.shape[0]
    r = m3d.shape[1]
    mod = lambda col: pl.BlockSpec((None, r, D), lambda i: (i // tiles_per_mod, 0, col))
    row = lambda w: pl.BlockSpec((tm, w), lambda i: (i, 0))
    row4 = pl.BlockSpec((tm * H, DH), lambda i: (i, 0))
    widths = (4 * W_MIX, W_MIX, W_MIX, W_MIX, 2 * D)
    kv4 = jax.ShapeDtypeStruct((R * H, DH), F32)
    return pl.pallas_call(
        _inproj_kernel,
        grid=(R // tm,),
        in_specs=[row(D), mod(1), mod(0),
                  pl.BlockSpec((1, D), lambda i: (0, 0)),
                  pl.BlockSpec((D, D_IN), lambda i: (0, 0))],
        out_specs=[row(w) for w in widths] + [row4, row4],
        out_shape=[jax.ShapeDtypeStruct((R, w), F32) for w in widths] + [kv4, kv4],
        compiler_params=_cp(("arbitrary",)),
        name="inproj",
    )(x2d, m3d, m3d, g_pre.reshape(1, D), w_in_bf16)


def _hgrn_prompt_kernel(lb_ref, gn_ref, q_ref, f_ref, i_ref, g_ref, o_ref, sout_ref, st_ref):
    c = pl.program_id(0)
    nb = q_ref.shape[0]
    C = HGRN_CHUNK

    @pl.when(c == 0)
    def _():
        st_ref[...] = jnp.zeros_like(st_ref)

    row = lax.broadcasted_iota(jnp.int32, (C, DH), 0)
    causal = lax.broadcasted_iota(jnp.int32, (C, C), 0) >= lax.broadcasted_iota(jnp.int32, (C, C), 1)

    def per_batch(b, carry):
        hs = [slice(h * DH, (h + 1) * DH) for h in range(H)]
        qd, kd, k2, v, eg = [], [], [], [], []
        for h in range(H):
            lb = lb_ref[:, hs[h]]
            f = lb + (1.0 - lb) * _sigmoid(f_ref[b, :, hs[h]])
            k = 1.0 - f
            gc = jnp.log(f)
            for s in (1, 2, 4, 8, 16, 32):
                gc = gc + jnp.where(row >= s, pltpu.roll(gc, s, axis=0), 0.0)
            g_last = gc[C - 1:C, :]
            qd.append(q_ref[b, :, hs[h]] * jnp.exp(gc))
            kd.append(k * jnp.exp(-gc))
            k2.append(k * jnp.exp(g_last - gc))
            eg.append(jnp.exp(g_last))
            v.append(i_ref[b, :, hs[h]])
        st = [st_ref[b, h] for h in range(H)]
        a = [jnp.where(causal, _dot_nt(qd[h], kd[h]), 0.0) for h in range(H)]
        o_inter = [_dot_nt(qd[h], st[h]) for h in range(H)]
        kv_t = [_dot_tn(v[h], k2[h]) for h in range(H)]
        o_intra = [_dot(a[h], v[h]) for h in range(H)]
        for h in range(H):
            st_ref[b, h] = eg[h] * st[h] + kv_t[h]
            o = _rms(o_intra[h] + o_inter[h], gn_ref[:, hs[h]])
            ag = g_ref[b, :, hs[h]]
            o_ref[b, :, hs[h]] = o * (ag * _sigmoid(ag))
        return carry

    lax.fori_loop(0, nb, per_batch, 0)

    @pl.when(c == pl.num_programs(0) - 1)
    def _():
        def fin(b, carry):
            for h in range(H):
                sout_ref[b, h] = st_ref[b, h].T
            return carry
        lax.fori_loop(0, nb, fin, 0)


def _hgrn_prompt(hg3, lb, g_hgrn):
    B, L, _ = hg3.shape
    C = HGRN_CHUNK
    col = lambda j: pl.BlockSpec((B, C, W_MIX), lambda c: (0, c, j))
    vec = pl.BlockSpec((1, W_MIX), lambda c: (0, 0))
    return pl.pallas_call(
        _hgrn_prompt_kernel,
        grid=(L // C,),
        in_specs=[vec, vec, col(0), col(1), col(2), col(3)],
        out_specs=[pl.BlockSpec((B, C, W_MIX), lambda c: (0, c, 0)),
                   pl.BlockSpec((B, H, DH, DH), lambda c: (0, 0, 0, 0))],
        out_shape=[jax.ShapeDtypeStruct((B, L, W_MIX), F32),
                   jax.ShapeDtypeStruct((B, H, DH, DH), F32)],
        scratch_shapes=[pltpu.VMEM((B, H, DH, DH), F32)],
        compiler_params=_cp(("arbitrary",)),
        name="hgrn_prompt",
    )(lb.reshape(1, W_MIX), g_hgrn.reshape(1, W_MIX), hg3, hg3, hg3, hg3)


def _hgrn_step_kernel(lb_ref, gn_ref, x_ref, s_ref, o_ref, sout_ref):
    nt = x_ref.shape[0]
    eye = lax.broadcasted_iota(jnp.int32, (DH, DH), 0) == lax.broadcasted_iota(jnp.int32, (DH, DH), 1)

    def col(r):
        return jnp.sum(jnp.where(eye, jnp.broadcast_to(r, (DH, DH)), 0.0), axis=1, keepdims=True)

    for t in range(nt):
        for h in range(H):
            seg = lambda j: x_ref[t:t + 1, j * W_MIX + h * DH: j * W_MIX + (h + 1) * DH]
            lb = lb_ref[:, h * DH:(h + 1) * DH]
            q, v, ag = seg(0), seg(2), seg(3)
            f = lb + (1.0 - lb) * _sigmoid(seg(1))
            s_new = col(f) * s_ref[t, h] + col(1.0 - f) * v
            sout_ref[t, h] = s_new
            o = jnp.sum(col(q) * s_new, axis=0, keepdims=True)
            o = _rms(o, gn_ref[:, h * DH:(h + 1) * DH])
            o_ref[t:t + 1, h * DH:(h + 1) * DH] = o * (ag * _sigmoid(ag))


def _hgrn_step(hg, state, lb, g_hgrn):
    N = hg.shape[0]
    nt = 8
    vec = pl.BlockSpec((1, W_MIX), lambda i: (0, 0))
    st = pl.BlockSpec((nt, H, DH, DH), lambda i: (i, 0, 0, 0))
    return pl.pallas_call(
        _hgrn_step_kernel,
        grid=(N // nt,),
        in_specs=[vec, vec, pl.BlockSpec((nt, 4 * W_MIX), lambda i: (i, 0)), st],
        out_specs=[pl.BlockSpec((nt, W_MIX), lambda i: (i, 0)), st],
        out_shape=[jax.ShapeDtypeStruct((N, W_MIX), F32),
                   jax.ShapeDtypeStruct((N, H, DH, DH), F32)],
        compiler_params=_cp(("arbitrary",)),
        name="hgrn_step",
    )(lb.reshape(1, W_MIX), g_hgrn.reshape(1, W_MIX), hg, state)


def _attn_prompt_kernel(lam_ref, sl_ref, q_ref, k_ref, v_ref, g_ref, o_ref, qt_ref, vt_ref, *chain_refs):
    G, NG, tk = ATT_G, ATT_NG, ATT_TK
    tq = G * NG
    m_refs = [chain_refs[mp * NG:(mp + 1) * NG] for mp in range(2)]
    acc_refs = [chain_refs[(2 + mp) * NG:(3 + mp) * NG] for mp in range(2)]
    slope = sl_ref[pl.program_id(1)] * LOG2E
    lam = lam_ref[0]
    L = q_ref.shape[0]
    q_t = q_ref[...].T * (DK ** -0.5 * LOG2E)
    sub = lax.broadcasted_iota(jnp.int32, (DH, L), 0)
    for mp in range(2):
        qm_t = jnp.where(sub < DK, q_t, 0.0) if mp == 0 else jnp.where(sub >= DK, q_t, 0.0)
        for t in range(L // G):
            qt_ref[mp, t] = qm_t[:, t * G:(t + 1) * G]
    v_t = v_ref[...].T
    for kb in range(L // tk):
        vt_ref[kb, :DH, :] = v_t[:, kb * tk:(kb + 1) * tk]
        vt_ref[kb, DH:, :] = jnp.ones((ATT_VR - DH, tk), F32)
    idiff = lax.broadcasted_iota(jnp.int32, (tk, G), 1) - lax.broadcasted_iota(jnp.int32, (tk, G), 0)
    dmat = idiff.astype(F32) * slope

    def fold(qi, kb, chains):
        k0 = pl.multiple_of(kb * tk, tk)
        kt = k_ref[pl.ds(k0, tk), :]
        scores = [_dot(kt, qt_ref[mp, qi * NG + g]) for mp, g, _ in chains]
        probs = []
        for (mp, g, thresh), s in zip(chains, scores):
            s = s - dmat
            if thresh is not None:
                s = jnp.where(idiff >= thresh, s, NEG_INF)
            c = slope * ((qi * NG + g) * G - k0).astype(F32)
            m = m_refs[mp][g][...]
            m_new = jnp.maximum(m, jnp.max(s, axis=0, keepdims=True) - c)
            m_refs[mp][g][...] = m_new
            probs.append((jnp.exp2(s - (m_new + c)), jnp.exp2(m - m_new)))
        va = vt_ref[kb]
        pvs = [_dot(va, p) for p, _ in probs]
        for (mp, g, _), (_, alpha), pv in zip(chains, probs, pvs):
            acc_refs[mp][g][...] = alpha * acc_refs[mp][g][...] + pv

    def q_tile(qi, carry):
        for mp in range(2):
            for g in range(NG):
                m_refs[mp][g][...] = jnp.full((1, G), NEG_INF, F32)
                acc_refs[mp][g][...] = jnp.zeros((ATT_VR, G), F32)

        every = [(mp, g, None) for mp in range(2) for g in range(NG)]
        n_full = qi * (tq // tk)
        lax.fori_loop(0, n_full, lambda kb, c: (fold(qi, kb, every), c)[1], 0)
        for j in range(tq // tk):
            chains = []
            for mp in range(2):
                for g in range(NG):
                    thresh = j * tk - G * g
                    if thresh <= G - 1:
                        chains.append((mp, g, None if thresh <= -(tk - 1) else thresh))
            fold(qi, n_full + j, chains)


        for g in range(NG):
            a0, a1 = acc_refs[0][g][...], acc_refs[1][g][...]
            o_t = a0[:DH] / a0[DH:DH + 1] - lam * (a1[:DH] / a1[DH:DH + 1])
            o_t = o_t * lax.rsqrt(jnp.mean(o_t * o_t, axis=0, keepdims=True) + EPS)
            q0 = pl.multiple_of((qi * NG + g) * G, G)
            o_ref[pl.ds(q0, G), :] = o_t.T * (g_ref[...] * (1.0 - LAM_INIT))
        return carry

    lax.fori_loop(0, L // tq, q_tile, 0)


def _attn_prompt(bq, bk, bv, lam, slopes, g_subln, B, L):
    smem = pl.BlockSpec(memory_space=pltpu.SMEM)
    spec = pl.BlockSpec((L, DH), lambda b, h: (b, h))
    return pl.pallas_call(
        _attn_prompt_kernel,
        grid=(B, H),
        in_specs=[smem, smem, spec, spec, spec, pl.BlockSpec((1, DH), lambda b, h: (0, 0))],
        out_specs=spec,
        out_shape=jax.ShapeDtypeStruct((B * L, W_MIX), F32),
        scratch_shapes=[pltpu.VMEM((2, L // ATT_G, DH, ATT_G), F32),
                        pltpu.VMEM((L // ATT_TK, ATT_VR, ATT_TK), F32)]
                       + [pltpu.VMEM((1, ATT_G), F32)] * (2 * ATT_NG)
                       + [pltpu.VMEM((ATT_VR, ATT_G), F32)] * (2 * ATT_NG),
        compiler_params=_cp(("arbitrary", "arbitrary")),
        name="attn_prompt",
    )(lam.reshape(1), slopes, bq, bk, bv, g_subln.reshape(1, DH))


def _attn_step_kernel(pt_ref, lam_ref, sl_ref, q_ref, kn_ref, vn_ref, g_ref, *refs, n_pages, page):
    k_refs, v_refs, o_ref = refs[:n_pages], refs[n_pages:2 * n_pages], refs[2 * n_pages]
    PR = page * H
    past = n_pages * page
    lam = lam_ref[0]
    R = 2 * H
    grp = lax.broadcasted_iota(jnp.int32, (R, W_MIX), 1) // DK
    rid = lax.broadcasted_iota(jnp.int32, (R, W_MIX), 0)
    qbd = jnp.where(grp == rid, jnp.broadcast_to(q_ref[...], (R, W_MIX)), 0.0) * (DK ** -0.5)
    s_new = jnp.sum(qbd * kn_ref[...], axis=1, keepdims=True)
    qx = sum(qbd[:, hh * DH:(hh + 1) * DH] for hh in range(H))
    rhead = lax.broadcasted_iota(jnp.int32, (R, DH), 0) // 2
    vnx = sum(jnp.where(rhead == hh, vn_ref[:, hh * DH:(hh + 1) * DH], 0.0) for hh in range(H))
    slope = jnp.concatenate([jnp.full((2, 1), sl_ref[hh], F32) for hh in range(H)], axis=0)

    col = lax.broadcasted_iota(jnp.int32, (R, n_pages * PR), 1)
    own = (col % H) == (lax.broadcasted_iota(jnp.int32, (R, n_pages * PR), 0) // 2)
    s = jnp.concatenate([_dot_nt(qx, k_refs[j][...]) for j in range(n_pages)], axis=1)
    s = jnp.where(own, s - slope * (past - col // H).astype(F32), NEG_INF)
    m = jnp.maximum(jnp.max(s, axis=1, keepdims=True), s_new)
    p = jnp.exp(s - m)
    p_new = jnp.exp(s_new - m)
    inv_l = 1.0 / (jnp.sum(p, axis=1, keepdims=True) + p_new)
    p = p * inv_l
    o8 = (p_new * inv_l) * vnx
    for j in range(n_pages):
        o8 = o8 + _dot(p[:, j * PR:(j + 1) * PR], v_refs[j][...])
    for hh in range(H):
        o = o8[2 * hh:2 * hh + 1, :] - lam * o8[2 * hh + 1:2 * hh + 2, :]
        o_ref[:, hh * DH:(hh + 1) * DH] = _rms(o, g_ref[...]) * (1.0 - LAM_INIT)


def _attn_step(bq, k_new, v_new, cache_k, cache_v, page_table, lam, slopes, g_subln):
    N = bq.shape[0]
    n_pool, page = cache_k.shape[0], cache_k.shape[1]
    n_pages = page_table.shape[1]
    cache_k = cache_k.reshape(n_pool * page * H, DH)
    cache_v = cache_v.reshape(n_pool * page * H, DH)
    smem = pl.BlockSpec(memory_space=pltpu.SMEM)
    rowspec = pl.BlockSpec((None, 1, W_MIX), lambda b, pt: (b, 0, 0))
    pspec = lambda j: pl.BlockSpec((page * H, DH), lambda b, pt: (pt[b * n_pages + j], 0))
    grid_spec = pltpu.PrefetchScalarGridSpec(
        num_scalar_prefetch=1,
        grid=(N,),
        in_specs=[smem, smem, rowspec, rowspec, rowspec, pl.BlockSpec((1, DH), lambda b, pt: (0, 0))]
                 + [pspec(j) for j in range(n_pages)] * 2,
        out_specs=rowspec,
    )
    out = pl.pallas_call(
        functools.partial(_attn_step_kernel, n_pages=n_pages, page=page),
        grid_spec=grid_spec,
        out_shape=jax.ShapeDtypeStruct((N, 1, W_MIX), F32),
        compiler_params=_cp(("arbitrary",)),
        name="attn_step",
    )(page_table.reshape(-1), lam.reshape(1), slopes, bq.reshape(N, 1, W_MIX), k_new.reshape(N, 1, W_MIX),
      v_new.reshape(N, 1, W_MIX), g_subln.reshape(1, DH), *([cache_k] * n_pages), *([cache_v] * n_pages))
    return out.reshape(N, W_MIX)


def _merge_kernel(oa_ref, ob_ref, gt_ref, x_ref, g1_ref, sc2_ref, sh2_ref, gpost_ref, gpre_ref,
                  wa_ref, wb_ref, wo_ref, wr_ref, br_ref, cnt0_ref, x1_ref, h3_ref, rt_ref, cnt_ref, run_ref,
                  *, n_part):
    i = pl.program_id(0)
    tm = x_ref.shape[0]
    tp = tm // n_part
    parts = [slice(p * tp, (p + 1) * tp) for p in range(n_part)]
    rows = lambda ref, r: ref[...] if ref.shape[0] == 1 else ref[r, :]

    @pl.when(i == 0)
    def _():
        run_ref[...] = cnt0_ref[...]

    pa = [_dot(oa_ref[r, :].astype(BF16), wa_ref[...]) for r in parts]
    pb = [_dot(ob_ref[r, :].astype(BF16), wb_ref[...]) for r in parts]
    merged = [_sigmoid(gt_ref[r, :D]) * pa[p] + _sigmoid(gt_ref[r, D:]) * pb[p] for p, r in enumerate(parts)]
    z = [_dot(m.astype(BF16), wo_ref[...]) for m in merged]
    h2 = []
    for p, r in enumerate(parts):
        x1 = x_ref[r, :] + rows(g1_ref, r) * _rms(z[p], gpost_ref[...])
        x1_ref[r, :] = x1
        h2.append(_rms(x1, gpre_ref[...]) * (1.0 + rows(sc2_ref, r)) + rows(sh2_ref, r))
        _store_row_tiles(h3_ref, h2[p], p * tp)
    logits = [_dot(h, wr_ref[...]) + br_ref[...] for h in h2]

    lane = lax.broadcasted_iota(jnp.int32, (tp, N_EXPERTS), 1).astype(F32)
    sels, idxs, gates, chosen = [], [], [], []
    for p in range(n_part):
        work = logits[p]
        sel_p, idx_p, val_p = [], [], []
        for _ in range(TOP_K):
            mx = jnp.max(work, axis=1, keepdims=True)
            idx = jnp.min(jnp.where(work == mx, lane, float(N_EXPERTS)), axis=1, keepdims=True)
            sel = lane == idx
            work = jnp.where(sel, -jnp.inf, work)
            sel_p.append(sel)
            idx_p.append(idx)
            val_p.append(mx)
        ex = [jnp.exp(v - val_p[0]) for v in val_p]
        inv_den = 1.0 / (ex[0] + ex[1] + ex[2] + ex[3])
        sels.append(sel_p)
        idxs.append(idx_p)
        gates.append([e * inv_den for e in ex])
        chosen.append(sum(s.astype(F32) for s in sel_p))

    chosen_all = jnp.concatenate(chosen, axis=0) if n_part > 1 else chosen[0]
    before = (lax.broadcasted_iota(jnp.int32, (tm, tm), 0) > lax.broadcasted_iota(jnp.int32, (tm, tm), 1))
    prior = _dot(before.astype(BF16), chosen_all.astype(BF16)) + run_ref[...]
    run_ref[...] = run_ref[...] + jnp.sum(chosen_all, axis=0, keepdims=True)
    cnt_ref[...] = run_ref[...]

    rlane = lax.broadcasted_iota(jnp.int32, (tp, LANES), 1)
    for p, r in enumerate(parts):
        rt = jnp.zeros((tp, LANES), F32)
        for kk in range(TOP_K):
            rank = jnp.sum(jnp.where(sels[p][kk], prior[r, :], 0.0), axis=1, keepdims=True)
            rt = jnp.where(rlane == RT_E + kk, idxs[p][kk], rt)
            rt = jnp.where(rlane == RT_RANK + kk, rank, rt)
            rt = jnp.where(rlane == RT_GATE + kk, gates[p][kk], rt)
        rt_ref[r, :] = rt


def _merge(oa, ob, gt, x2d, m3d, tiles_per_mod, g_post, g_pre, wa, wb, wo, wr, br, cnt0, tm):
    R = x2d.shape[0]
    r = m3d.shape[1]
    mod = lambda col: pl.BlockSpec((None, r, D), lambda i: (i // tiles_per_mod, 0, col))
    row = lambda w: pl.BlockSpec((tm, w), lambda i: (i, 0))
    full = lambda a, b: pl.BlockSpec((a, b), lambda i: (0, 0))
    return pl.pallas_call(
        functools.partial(_merge_kernel, n_part=2 if tm >= 256 else 1),
        grid=(R // tm,),
        in_specs=[row(W_MIX), row(W_MIX), row(2 * D), row(D), mod(2), mod(4), mod(3),
                  full(1, D), full(1, D), full(W_MIX, D), full(W_MIX, D), full(D, D),
                  full(D, N_EXPERTS), full(1, N_EXPERTS), full(1, N_EXPERTS)],
        out_specs=[row(D), pl.BlockSpec((tm * SUB, LANES), lambda i: (i, 0)), row(LANES), full(1, N_EXPERTS)],
        out_shape=[jax.ShapeDtypeStruct((R, D), F32),
                   jax.ShapeDtypeStruct((R * SUB, LANES), F32),
                   jax.ShapeDtypeStruct((R, LANES), F32),
                   jax.ShapeDtypeStruct((1, N_EXPERTS), F32)],
        scratch_shapes=[pltpu.VMEM((1, N_EXPERTS), F32)],
        compiler_params=_cp(("arbitrary",)),
        name="merge",
    )(oa, ob, gt, x2d, m3d, m3d, m3d, g_post.reshape(1, D), g_pre.reshape(1, D), wa, wb, wo, wr,
      br.reshape(1, N_EXPERTS), cnt0)


def _row_copy(src, s_row, dst, d_row, sem):
    return pltpu.make_async_copy(src.at[pl.ds(pl.multiple_of(s_row, SUB), SUB)],
                                 dst.at[pl.ds(pl.multiple_of(d_row, SUB), SUB)], sem)


def _dispatch_kernel(lo_ref, hi_ref, nu_ref, da_ref, db_ref, ha_ref, hb_ref, x_hbm, zero_ref, sem):
    i = pl.program_id(0)
    last = pl.num_programs(0) - 1
    unroll = 8

    def scatter_rows(d_ref, h_ref):
        n = h_ref.shape[0] // SUB

        def issue(j, carry):
            for u in range(unroll):
                t = j * unroll + u
                for kk in range(TOP_K):
                    _row_copy(h_ref, t * SUB, x_hbm, d_ref[t * TOP_K + kk], sem).start(priority=kk % 2)
            return carry

        lax.fori_loop(0, n // unroll, issue, 0)
        for _ in range(TOP_K):
            pltpu.make_async_copy(h_ref, x_hbm.at[pl.ds(0, n * SUB)], sem).wait()

    @pl.when(i < last)
    def _():
        scatter_rows(da_ref, ha_ref)

    @pl.when(i == last)
    def _():
        scatter_rows(db_ref, hb_ref)
        zero_ref[...] = jnp.zeros_like(zero_ref)

        def per_expert(e, carry):
            zrow = lambda r: _row_copy(zero_ref, 0, x_hbm, r * SUB, sem)
            lax.fori_loop(lo_ref[e], hi_ref[e], lambda r, c: (zrow(r).start(), c)[1], 0)
            lax.fori_loop(lo_ref[e], hi_ref[e], lambda r, c: (zrow(r).wait(), c)[1], 0)
            return carry

        lax.fori_loop(0, N_EXPERTS, per_expert, 0)

        blk = zero_ref.shape[0]
        zblk = lambda b: pltpu.make_async_copy(zero_ref, x_hbm.at[pl.ds(pl.multiple_of(b * blk, blk), blk)], sem)
        lax.fori_loop(nu_ref[0], x_hbm.shape[0] // blk, lambda b, c: (zblk(b).start(), c)[1], 0)
        lax.fori_loop(nu_ref[0], x_hbm.shape[0] // blk, lambda b, c: (zblk(b).wait(), c)[1], 0)


def _dispatch(h3_a, h3_b, dest8_a, dest8_b, pad_lo, pad_hi, n_used, n_rows):
    tm = DISPATCH_ROWS
    n_a = h3_a.shape[0] // (tm * SUB)
    clamp = lambda i: jnp.minimum(i, n_a - 1)
    grid_spec = pltpu.PrefetchScalarGridSpec(
        num_scalar_prefetch=3,
        grid=(n_a + 1,),
        in_specs=[pl.BlockSpec((tm * TOP_K,), lambda i, *_: (clamp(i),), memory_space=pltpu.SMEM),
                  pl.BlockSpec(dest8_b.shape, lambda i, *_: (0,), memory_space=pltpu.SMEM),
                  pl.BlockSpec((tm * SUB, LANES), lambda i, *_: (clamp(i), 0)),
                  pl.BlockSpec(h3_b.shape, lambda i, *_: (0, 0))],
        out_specs=pl.BlockSpec(memory_space=pl.ANY),
        scratch_shapes=[pltpu.VMEM((MOE_BLOCK * SUB, LANES), F32), pltpu.SemaphoreType.DMA],
    )
    return pl.pallas_call(
        _dispatch_kernel,
        grid_spec=grid_spec,
        out_shape=jax.ShapeDtypeStruct((n_rows * SUB, LANES), F32),
        compiler_params=_cp(("arbitrary",)),
        name="dispatch",
    )(pad_lo, pad_hi, n_used, dest8_a, dest8_b, h3_a, h3_b)


def _expert_kernel(b0_ref, nb_ref, nu_ref, x_hbm, wgu_ref, bgu_ref, wd_ref, bd_ref, y_hbm,
                   xbuf, ybuf, wgu_b, wd_b, xsem, ysem):
    e = pl.program_id(0)
    rows = xbuf.shape[1]
    b0, nb = b0_ref[e], nb_ref[e]
    x_copy = lambda j, slot: pltpu.make_async_copy(
        x_hbm.at[pl.ds(pl.multiple_of((b0 + j) * rows, rows), rows)], xbuf.at[slot], xsem.at[slot])
    y_copy = lambda j, slot: pltpu.make_async_copy(
        ybuf.at[slot], y_hbm.at[pl.ds(pl.multiple_of((b0 + j) * rows, rows), rows)], ysem.at[slot])

    @pl.when(nb > 0)
    def _():
        x_copy(0, 0).start(priority=1)
        wgu_b[...] = wgu_ref[...].astype(BF16)
        wd_b[...] = wd_ref[...].astype(BF16)

        def block(j, carry):
            slot = j % 2
            x_copy(j, slot).wait()

            @pl.when(j + 1 < nb)
            def _():
                x_copy(j + 1, 1 - slot).start(priority=1)

            @pl.when(j >= 2)
            def _():
                y_copy(j - 2, slot).wait()

            gu = _dot(_load_row_tiles(xbuf.at[slot]).astype(BF16), wgu_b[...]) + bgu_ref[...]
            glu = jnp.minimum(gu[:, :D], SWIGLU_LIMIT)
            lin = jnp.clip(gu[:, D:], -SWIGLU_LIMIT, SWIGLU_LIMIT)
            act = (lin + 1.0) * glu * _sigmoid(SWIGLU_ALPHA * glu)
            _store_row_tiles(ybuf.at[slot], _dot(act.astype(BF16), wd_b[...]) + bd_ref[...])
            y_copy(j, slot).start(priority=1)
            return carry

        lax.fori_loop(0, nb, block, 0)

        @pl.when(nb >= 2)
        def _():
            y_copy(nb - 2, nb % 2).wait()
        y_copy(nb - 1, (nb - 1) % 2).wait()

    @pl.when(e == pl.num_programs(0) - 1)
    def _():
        ybuf[0] = jnp.zeros(ybuf.shape[1:], F32)
        zblk = lambda b: pltpu.make_async_copy(
            ybuf.at[0], y_hbm.at[pl.ds(pl.multiple_of(b * rows, rows), rows)], ysem.at[0])
        n_blocks = y_hbm.shape[0] // rows
        lax.fori_loop(nu_ref[0], n_blocks, lambda b, c: (zblk(b).start(), c)[1], 0)
        lax.fori_loop(nu_ref[0], n_blocks, lambda b, c: (zblk(b).wait(), c)[1], 0)


def _experts(x3, blk_start, blk_count, n_used, w_gate_up, b_gate_up, w_down, b_down):
    rows = MOE_BLOCK * SUB
    wspec = lambda a, b: pl.BlockSpec((None, a, b), lambda e, *_: (e, 0, 0))
    grid_spec = pltpu.PrefetchScalarGridSpec(
        num_scalar_prefetch=3,
        grid=(N_EXPERTS,),
        in_specs=[pl.BlockSpec(memory_space=pl.ANY), wspec(D, 2 * D), wspec(1, 2 * D), wspec(D, D), wspec(1, D)],
        out_specs=pl.BlockSpec(memory_space=pl.ANY),
        scratch_shapes=[pltpu.VMEM((2, rows, LANES), F32), pltpu.VMEM((2, rows, LANES), F32),
                        pltpu.VMEM((D, 2 * D), BF16), pltpu.VMEM((D, D), BF16),
                        pltpu.SemaphoreType.DMA((2,)), pltpu.SemaphoreType.DMA((2,))],
    )
    return pl.pallas_call(
        _expert_kernel,
        grid_spec=grid_spec,
        out_shape=jax.ShapeDtypeStruct(x3.shape, F32),
        compiler_params=_cp(("arbitrary",)),
        name="experts",
    )(blk_start, blk_count, n_used, x3, w_gate_up, b_gate_up.reshape(N_EXPERTS, 1, 2 * D), w_down,
      b_down.reshape(N_EXPERTS, 1, D))


def _final_kernel(dcur_ref, dnext_ref, x1_ref, g2_ref, gpost_ref, rt_ref, y_hbm, o_ref, ybuf, sem):
    i = pl.program_id(0)
    n = pl.num_programs(0)
    tm = x1_ref.shape[0]
    slot = i % 2
    unroll = 8

    def issue(d_ref, s):
        def body(j, carry):
            for u in range(unroll):
                t = j * unroll + u
                for kk in range(TOP_K):
                    _row_copy(y_hbm, d_ref[t * TOP_K + kk], ybuf.at[s, kk], t * SUB, sem.at[s]).start(priority=kk % 2)
            return carry
        lax.fori_loop(0, tm // unroll, body, 0)

    @pl.when(i == 0)
    def _():
        issue(dcur_ref, 0)

    @pl.when(i + 1 < n)
    def _():
        issue(dnext_ref, 1 - slot)

    for kk in range(TOP_K):
        pltpu.make_async_copy(y_hbm.at[pl.ds(0, tm * SUB)], ybuf.at[slot, kk], sem.at[slot]).wait()

    y = jnp.zeros((tm, D), F32)
    for kk in range(TOP_K):
        y = y + rt_ref[:, RT_GATE + kk:RT_GATE + kk + 1] * _load_row_tiles(ybuf.at[slot, kk])
    o_ref[...] = x1_ref[...] + g2_ref[...] * _rms(y, gpost_ref[...])


def _final(x1, m3d, tiles_per_mod, g_post, route, dest_flat, y3, row0, tm):
    R = x1.shape[0]
    r = m3d.shape[1]
    off = row0 // tm
    nsteps = R // tm
    row = pl.BlockSpec((tm, D), lambda i: (i, 0))
    dspec = lambda nxt: pl.BlockSpec((tm * TOP_K,), lambda i: (jnp.minimum(i + nxt, nsteps - 1) + off,),
                                     memory_space=pltpu.SMEM)
    return pl.pallas_call(
        _final_kernel,
        grid=(nsteps,),
        in_specs=[dspec(0), dspec(1), row,
                  pl.BlockSpec((None, r, D), lambda i: (i // tiles_per_mod, 0, 5)),
                  pl.BlockSpec((1, D), lambda i: (0, 0)),
                  pl.BlockSpec((tm, LANES), lambda i: (i, 0)),
                  pl.BlockSpec(memory_space=pl.ANY)],
        out_specs=row,
        out_shape=jax.ShapeDtypeStruct((R, D), F32),
        scratch_shapes=[pltpu.VMEM((2, TOP_K, tm * SUB, LANES), F32), pltpu.SemaphoreType.DMA((2,))],
        compiler_params=_cp(("arbitrary",)),
        name="final",
    )(dest_flat, dest_flat, x1, m3d, g_post.reshape(1, D), route, y3)


def kernel(x_prompt, x_sample, cache_k, cache_v, state_hgrn, page_table, c_prompt, c_sample, w_mod, b_mod, g_pre_mix, g_post_mix, g_pre_ffn, g_post_ffn, w_in, hgrn_lb, g_hgrn, lam_q1, lam_k1, lam_q2, lam_k2, g_subln, w_proj_a, w_proj_b, w_out, w_router, b_router, w_gate_up, b_gate_up, w_down, b_down):
    B, L, _ = x_prompt.shape
    NS = x_sample.shape[0]
    T = B * L
    T_all = T + NS
    l = 0

    lb = jnp.cumsum(jax.nn.softmax(hgrn_lb.astype(F32), axis=0), axis=0)[l]
    lam = (jnp.exp(jnp.sum(lam_q1[l] * lam_k1[l])) - jnp.exp(jnp.sum(lam_q2[l] * lam_k2[l])) + LAM_INIT).astype(F32)
    slopes = jnp.exp2(-8.0 * jnp.arange(1, H + 1, dtype=F32) / H)

    m_all = _modulation(jnp.concatenate([c_prompt, c_sample], axis=0), w_mod[l], b_mod[l])
    m_p = m_all[:B].reshape(B, 1, 6 * D)
    m_s = m_all[B:].reshape(1, NS, 6 * D)

    w_in_b = w_in[l].astype(BF16)
    tm = 256
    tpm = L // tm
    xp = x_prompt.reshape(T, D)
    xs = x_sample.reshape(NS, D)
    merge_w = (g_post_mix[l], g_pre_ffn[l], w_proj_a[l].astype(BF16), w_proj_b[l].astype(BF16),
               w_out[l].astype(BF16), w_router[l], b_router[l])

    hg_p, bq_p, k_p, v_p, gt_p, k4_p, v4_p = _inproj(xp, m_p, tpm, g_pre_mix[l], w_in_b, tm)
    oa_p, s_p = _hgrn_prompt(hg_p.reshape(B, L, 4 * W_MIX), lb, g_hgrn[l])
    ob_p = _attn_prompt(bq_p, k_p, v_p, lam, slopes, g_subln[l], B, L)
    x1_p, h3_p, rt_p, cnt = _merge(oa_p.reshape(T, W_MIX), ob_p, gt_p, xp, m_p, L // MERGE_ROWS, *merge_w,
                                   jnp.zeros((1, N_EXPERTS), F32), MERGE_ROWS)

    hg_s, bq_s, k_s, v_s, gt_s, k4_s, v4_s = _inproj(xs, m_s, 1, g_pre_mix[l], w_in_b, NS)
    oa_s, s_s = _hgrn_step(hg_s, state_hgrn[l], lb, g_hgrn[l])
    ob_s = _attn_step(bq_s, k_s, v_s, cache_k[l], cache_v[l], page_table, lam, slopes, g_subln[l])
    x1_s, h3_s, rt_s, cnt = _merge(oa_s, ob_s, gt_s, xs, m_s, 1, *merge_w, cnt, NS)

    bm = MOE_BLOCK
    counts = cnt.reshape(N_EXPERTS).astype(jnp.int32)
    padded = (counts + bm - 1) // bm * bm
    pad_end = jnp.cumsum(padded)
    pad_start = pad_end - padded
    n_blocks = -(-T_all * TOP_K // bm) + N_EXPERTS
    route = jnp.concatenate([rt_p[:, :RT_GATE], rt_s[:, :RT_GATE]], axis=0)
    e_idx = route[:, RT_E:RT_E + TOP_K].astype(jnp.int32)
    rank = route[:, RT_RANK:RT_RANK + TOP_K].astype(jnp.int32)
    e_iota = jnp.arange(N_EXPERTS, dtype=jnp.int32)
    dest = rank + jnp.sum(jnp.where(e_idx[..., None] == e_iota, pad_start, 0), axis=-1)
    dest_flat = dest.reshape(-1) * SUB
    n_used = pad_end[-1:] // bm

    x3 = _dispatch(h3_p, h3_s, dest_flat[:T * TOP_K], dest_flat[T * TOP_K:], pad_start + counts, pad_end, n_used,
                   n_blocks * bm)
    y3 = _experts(x3, pad_start // bm, padded // bm, n_used, w_gate_up[l], b_gate_up[l], w_down[l], b_down[l])
    y_p = _final(x1_p, m_p, tpm, g_post_ffn[l], rt_p, dest_flat, y3, 0, tm)
    y_s = _final(x1_s, m_s, 1, g_post_ffn[l], rt_s, dest_flat, y3, T, NS)

    return (y_p.reshape(B, L, D), y_s.reshape(NS, 1, D),
            k4_p.reshape(1, B, L, H, DH), v4_p.reshape(1, B, L, H, DH), s_p.reshape(1, B, H, DH, DH),
            k4_s.reshape(1, NS, 1, H, DH), v4_s.reshape(1, NS, 1, H, DH), s_s.reshape(1, NS, H, DH, DH))
```

```python
import functools
import math

import jax
import jax.numpy as jnp
from jax import lax
from jax.experimental import pallas as pl
from jax.experimental.pallas import tpu as pltpu

F32 = jnp.float32
BF16 = jnp.bfloat16

D = 1024
LANES = 128
SUB = D // LANES
H = 4
DH = 128
DK = 64
W_MIX = H * DH
HGRN_CHUNK = 64
N_EXPERTS = 32
TOP_K = 4
SWIGLU_LIMIT = 7.0
SWIGLU_ALPHA = 1.702
EPS = 1e-6
NEG_INF = -1e30
LOG2E = 1.4426950408889634
LAM_INIT = 0.8 - 0.6 * math.exp(-0.3 * 0)
C_HG = (0, 4 * W_MIX)
C_BQ = (4 * W_MIX, 5 * W_MIX)
C_BK = (5 * W_MIX, 6 * W_MIX)
C_BV = (6 * W_MIX, 7 * W_MIX)
C_GT = (7 * W_MIX, 7 * W_MIX + 2 * D)
D_IN = 7 * W_MIX + 2 * D
RT_E, RT_RANK, RT_GATE = 0, TOP_K, 2 * TOP_K

VMEM_LIMIT = 56 * 1024 * 1024
MOE_BLOCK = 256
DISPATCH_ROWS = 512
MERGE_ROWS = 512
ATTN_STEP_TOKENS = 2
ATT_G = 256
ATT_NG = 8
ATT_TK = 256
ATT_VR = DH + 8


def _cp(sem, vmem=VMEM_LIMIT):
    return pltpu.CompilerParams(dimension_semantics=sem, vmem_limit_bytes=vmem)


def _sigmoid(x):
    return 1.0 / (1.0 + jnp.exp(-x))


def _rms(x, g):
    return x * lax.rsqrt(jnp.mean(x * x, axis=-1, keepdims=True) + EPS) * g


def _dot(a, b):
    return jnp.dot(a, b, preferred_element_type=F32)


def _dot_nt(a, b):
    return lax.dot_general(a, b, (((1,), (1,)), ((), ())), preferred_element_type=F32)


def _dot_tn(a, b):
    return lax.dot_general(a, b, (((0,), (0,)), ((), ())), preferred_element_type=F32)


def _store_row_tiles(ref2, x2d, row0=0):
    n = x2d.shape[0]
    for j in range(SUB):
        ref2[pl.ds(row0 * SUB + j, n, stride=SUB), :] = x2d[:, j * LANES:(j + 1) * LANES]


def _load_row_tiles(ref2):
    n = ref2.shape[0] // SUB
    return jnp.concatenate([ref2[pl.ds(j, n, stride=SUB), :] for j in range(SUB)], axis=1)


def _mod_kernel(c_ref, w_ref, b_ref, o_ref):
    c = c_ref[...]
    o_ref[...] = _dot(c * _sigmoid(c), w_ref[...]) + b_ref[...]


def _modulation(c_all, w_mod, b_mod):
    n = c_all.shape[0]
    tn = 1024
    return pl.pallas_call(
        _mod_kernel,
        grid=(6 * D // tn,),
        in_specs=[pl.BlockSpec((n, D), lambda j: (0, 0)),
                  pl.BlockSpec((D, tn), lambda j: (0, j)),
                  pl.BlockSpec((1, tn), lambda j: (0, j))],
        out_specs=pl.BlockSpec((n, tn), lambda j: (0, j)),
        out_shape=jax.ShapeDtypeStruct((n, 6 * D), F32),
        compiler_params=_cp(("arbitrary",)),
        name="modulation",
    )(c_all, w_mod, b_mod.reshape(1, 6 * D))


def _inproj_kernel(x_ref, sc_ref, sh_ref, g_ref, w_ref, hg_ref, bq_ref, k_ref, v_ref, gt_ref, k4_ref, v4_ref):
    h = _rms(x_ref[...], g_ref[...]) * (1.0 + sc_ref[...]) + sh_ref[...]
    hb = h.astype(BF16)
    mm = lambda c: _dot(hb, w_ref[:, c[0]:c[1]])
    hg_ref[...] = mm(C_HG)
    bq_ref[...] = mm(C_BQ)
    gt_ref[...] = mm(C_GT)
    for c, o_ref, o4_ref in ((C_BK, k_ref, k4_ref), (C_BV, v_ref, v4_ref)):
        r = mm(c)
        o_ref[...] = r
        for hh in range(H):
            o4_ref[pl.ds(hh, r.shape[0], stride=H), :] = r[:, hh * DH:(hh + 1) * DH]


def _inproj(x2d, m3d, tiles_per_mod, g_pre, w_in_bf16, tm):
    R = x2d.shape[0]
    r = m3d.shape[1]
    mod = lambda col: pl.BlockSpec((None, r, D), lambda i: (i // tiles_per_mod, 0, col))
    row = lambda w: pl.BlockSpec((tm, w), lambda i: (i, 0))
    row4 = pl.BlockSpec((tm * H, DH), lambda i: (i, 0))
    widths = (4 * W_MIX, W_MIX, W_MIX, W_MIX, 2 * D)
    kv4 = jax.ShapeDtypeStruct((R * H, DH), F32)
    return pl.pallas_call(
        _inproj_kernel,
        grid=(R // tm,),
        in_specs=[row(D), mod(1), mod(0),
                  pl.BlockSpec((1, D), lambda i: (0, 0)),
                  pl.BlockSpec((D, D_IN), lambda i: (0, 0))],
        out_specs=[row(w) for w in widths] + [row4, row4],
        out_shape=[jax.ShapeDtypeStruct((R, w), F32) for w in widths] + [kv4, kv4],
        compiler_params=_cp(("arbitrary",)),
        name="inproj",
    )(x2d, m3d, m3d, g_pre.reshape(1, D), w_in_bf16)


def _hgrn_prompt_kernel(lb_ref, gn_ref, q_ref, f_ref, i_ref, g_ref, o_ref, sout_ref, st_ref):
    c = pl.program_id(0)
    nb = q_ref.shape[0]
    C = HGRN_CHUNK

    @pl.when(c == 0)
    def _():
        st_ref[...] = jnp.zeros_like(st_ref)

    row = lax.broadcasted_iota(jnp.int32, (C, DH), 0)
    causal = lax.broadcasted_iota(jnp.int32, (C, C), 0) >= lax.broadcasted_iota(jnp.int32, (C, C), 1)

    def per_batch(b, carry):
        hs = [slice(h * DH, (h + 1) * DH) for h in range(H)]
        qd, kd, k2, v, eg = [], [], [], [], []
        for h in range(H):
            lb = lb_ref[:, hs[h]]
            f = lb + (1.0 - lb) * _sigmoid(f_ref[b, :, hs[h]])
            k = 1.0 - f
            gc = jnp.log(f)
            for s in (1, 2, 4, 8, 16, 32):
                gc = gc + jnp.where(row >= s, pltpu.roll(gc, s, axis=0), 0.0)
            g_last = gc[C - 1:C, :]
            qd.append(q_ref[b, :, hs[h]] * jnp.exp(gc))
            kd.append(k * jnp.exp(-gc))
            k2.append(k * jnp.exp(g_last - gc))
            eg.append(jnp.exp(g_last))
            v.append(i_ref[b, :, hs[h]])
        st = [st_ref[b, h] for h in range(H)]
        a = [jnp.where(causal, _dot_nt(qd[h], kd[h]), 0.0) for h in range(H)]
        o_inter = [_dot_nt(qd[h], st[h]) for h in range(H)]
        kv_t = [_dot_tn(v[h], k2[h]) for h in range(H)]
        o_intra = [_dot(a[h], v[h]) for h in range(H)]
        for h in range(H):
            st_ref[b, h] = eg[h] * st[h] + kv_t[h]
            o = _rms(o_intra[h] + o_inter[h], gn_ref[:, hs[h]])
            ag = g_ref[b, :, hs[h]]
            o_ref[b, :, hs[h]] = o * (ag * _sigmoid(ag))
        return carry

    lax.fori_loop(0, nb, per_batch, 0)

    @pl.when(c == pl.num_programs(0) - 1)
    def _():
        def fin(b, carry):
            for h in range(H):
                sout_ref[b, h] = st_ref[b, h].T
            return carry
        lax.fori_loop(0, nb, fin, 0)


def _hgrn_prompt(hg3, lb, g_hgrn):
    B, L, _ = hg3.shape
    C = HGRN_CHUNK
    col = lambda j: pl.BlockSpec((B, C, W_MIX), lambda c: (0, c, j))
    vec = pl.BlockSpec((1, W_MIX), lambda c: (0, 0))
    return pl.pallas_call(
        _hgrn_prompt_kernel,
        grid=(L // C,),
        in_specs=[vec, vec, col(0), col(1), col(2), col(3)],
        out_specs=[pl.BlockSpec((B, C, W_MIX), lambda c: (0, c, 0)),
                   pl.BlockSpec((B, H, DH, DH), lambda c: (0, 0, 0, 0))],
        out_shape=[jax.ShapeDtypeStruct((B, L, W_MIX), F32),
                   jax.ShapeDtypeStruct((B, H, DH, DH), F32)],
        scratch_shapes=[pltpu.VMEM((B, H, DH, DH), F32)],
        compiler_params=_cp(("arbitrary",)),
        name="hgrn_prompt",
    )(lb.reshape(1, W_MIX), g_hgrn.reshape(1, W_MIX), hg3, hg3, hg3, hg3)


def _hgrn_step_kernel(lb_ref, gn_ref, x_ref, s_ref, o_ref, sout_ref):
    nt = x_ref.shape[0]
    ones = jnp.ones((8, DH), F32)
    pad = jnp.zeros((6, DH), F32)

    def col(r):
        hi = r.astype(BF16).astype(F32)
        return _dot_tn(jnp.concatenate([hi, r - hi, pad], axis=0), ones)

    for t in range(nt):
        for h in range(H):
            seg = lambda j: x_ref[t:t + 1, j * W_MIX + h * DH: j * W_MIX + (h + 1) * DH]
            lb = lb_ref[:, h * DH:(h + 1) * DH]
            q, v, ag = seg(0), seg(2), seg(3)
            f = lb + (1.0 - lb) * _sigmoid(seg(1))
            f_col = col(f)
            s_new = f_col * s_ref[t, h] + (1.0 - f_col) * v
            sout_ref[t, h] = s_new
            o = jnp.sum(col(q) * s_new, axis=0, keepdims=True)
            o = _rms(o, gn_ref[:, h * DH:(h + 1) * DH])
            o_ref[t:t + 1, h * DH:(h + 1) * DH] = o * (ag * _sigmoid(ag))


def _hgrn_step(hg, state, lb, g_hgrn):
    N = hg.shape[0]
    nt = 8
    vec = pl.BlockSpec((1, W_MIX), lambda i: (0, 0))
    st = pl.BlockSpec((nt, H, DH, DH), lambda i: (i, 0, 0, 0))
    return pl.pallas_call(
        _hgrn_step_kernel,
        grid=(N // nt,),
        in_specs=[vec, vec, pl.BlockSpec((nt, 4 * W_MIX), lambda i: (i, 0)), st],
        out_specs=[pl.BlockSpec((nt, W_MIX), lambda i: (i, 0)), st],
        out_shape=[jax.ShapeDtypeStruct((N, W_MIX), F32),
                   jax.ShapeDtypeStruct((N, H, DH, DH), F32)],
        compiler_params=_cp(("arbitrary",)),
        name="hgrn_step",
    )(lb.reshape(1, W_MIX), g_hgrn.reshape(1, W_MIX), hg, state)


def _attn_prompt_kernel(lam_ref, sl_ref, q_ref, k_ref, v_ref, g_ref, o_ref, qt_ref, vt_ref, *chain_refs):
    G, NG, tk = ATT_G, ATT_NG, ATT_TK
    tq = G * NG
    m_refs = [chain_refs[mp * NG:(mp + 1) * NG] for mp in range(2)]
    acc_refs = [chain_refs[(2 + mp) * NG:(3 + mp) * NG] for mp in range(2)]
    slope = sl_ref[pl.program_id(1)] * LOG2E
    lam = lam_ref[0]
    L = q_ref.shape[0]
    q_t = q_ref[...].T * (DK ** -0.5 * LOG2E)
    sub = lax.broadcasted_iota(jnp.int32, (DH, L), 0)
    for mp in range(2):
        qm_t = jnp.where(sub < DK, q_t, 0.0) if mp == 0 else jnp.where(sub >= DK, q_t, 0.0)
        for t in range(L // G):
            qt_ref[mp, t] = qm_t[:, t * G:(t + 1) * G]
    v_t = v_ref[...].T
    for kb in range(L // tk):
        vt_ref[kb, :DH, :] = v_t[:, kb * tk:(kb + 1) * tk]
        vt_ref[kb, DH:, :] = jnp.ones((ATT_VR - DH, tk), F32)
    idiff = lax.broadcasted_iota(jnp.int32, (tk, G), 1) - lax.broadcasted_iota(jnp.int32, (tk, G), 0)
    dmat = idiff.astype(F32) * slope

    def fold(qi, kb, chains):
        k0 = pl.multiple_of(kb * tk, tk)
        kt = k_ref[pl.ds(k0, tk), :]
        scores = [_dot(kt, qt_ref[mp, qi * NG + g]) for mp, g, _ in chains]
        probs = []
        for (mp, g, thresh), s in zip(chains, scores):
            s = s - dmat
            if thresh is not None:
                s = jnp.where(idiff >= thresh, s, NEG_INF)
            c = slope * ((qi * NG + g) * G - k0).astype(F32)
            m = m_refs[mp][g][...]
            m_new = jnp.maximum(m, jnp.max(s, axis=0, keepdims=True) - c)
            m_refs[mp][g][...] = m_new
            probs.append((jnp.exp2(s - (m_new + c)), jnp.exp2(m - m_new)))
        va = vt_ref[kb]
        pvs = [_dot(va, p) for p, _ in probs]
        for (mp, g, _), (_, alpha), pv in zip(chains, probs, pvs):
            acc_refs[mp][g][...] = alpha * acc_refs[mp][g][...] + pv

    def q_tile(qi, carry):
        for mp in range(2):
            for g in range(NG):
                m_refs[mp][g][...] = jnp.full((1, G), NEG_INF, F32)
                acc_refs[mp][g][...] = jnp.zeros((ATT_VR, G), F32)

        every = [(mp, g, None) for mp in range(2) for g in range(NG)]
        n_full = qi * (tq // tk)
        lax.fori_loop(0, n_full, lambda kb, c: (fold(qi, kb, every), c)[1], 0)
        for j in range(tq // tk):
            chains = []
            for mp in range(2):
                for g in range(NG):
                    thresh = j * tk - G * g
                    if thresh <= G - 1:
                        chains.append((mp, g, None if thresh <= -(tk - 1) else thresh))
            fold(qi, n_full + j, chains)


        for g in range(NG):
            a0, a1 = acc_refs[0][g][...], acc_refs[1][g][...]
            o_t = a0[:DH] / a0[DH:DH + 1] - lam * (a1[:DH] / a1[DH:DH + 1])
            o_t = o_t * lax.rsqrt(jnp.mean(o_t * o_t, axis=0, keepdims=True) + EPS)
            q0 = pl.multiple_of((qi * NG + g) * G, G)
            o_ref[pl.ds(q0, G), :] = o_t.T * (g_ref[...] * (1.0 - LAM_INIT))
        return carry

    lax.fori_loop(0, L // tq, q_tile, 0)


def _attn_prompt(bq, bk, bv, lam, slopes, g_subln, B, L):
    smem = pl.BlockSpec(memory_space=pltpu.SMEM)
    spec = pl.BlockSpec((L, DH), lambda b, h: (b, h))
    return pl.pallas_call(
        _attn_prompt_kernel,
        grid=(B, H),
        in_specs=[smem, smem, spec, spec, spec, pl.BlockSpec((1, DH), lambda b, h: (0, 0))],
        out_specs=spec,
        out_shape=jax.ShapeDtypeStruct((B * L, W_MIX), F32),
        scratch_shapes=[pltpu.VMEM((2, L // ATT_G, DH, ATT_G), F32),
                        pltpu.VMEM((L // ATT_TK, ATT_VR, ATT_TK), F32)]
                       + [pltpu.VMEM((1, ATT_G), F32)] * (2 * ATT_NG)
                       + [pltpu.VMEM((ATT_VR, ATT_G), F32)] * (2 * ATT_NG),
        compiler_params=_cp(("arbitrary", "arbitrary")),
        name="attn_prompt",
    )(lam.reshape(1), slopes, bq, bk, bv, g_subln.reshape(1, DH))


def _attn_step_kernel(pt_ref, lam_ref, sl_ref, q_ref, kn_ref, vn_ref, g_ref, *refs, n_pages, page, n_tok):
    o_ref = refs[2 * n_tok * n_pages]
    PR = page * H
    past = n_pages * page
    lam = lam_ref[0]
    R = 2 * H
    grp = lax.broadcasted_iota(jnp.int32, (R, W_MIX), 1) // DK
    rid = lax.broadcasted_iota(jnp.int32, (R, W_MIX), 0)
    rhead = lax.broadcasted_iota(jnp.int32, (R, DH), 0) // 2
    slope = jnp.concatenate([jnp.full((2, 1), sl_ref[hh], F32) for hh in range(H)], axis=0)
    col = lax.broadcasted_iota(jnp.int32, (R, n_pages * PR), 1)
    own = (col % H) == (lax.broadcasted_iota(jnp.int32, (R, n_pages * PR), 0) // 2)
    bias = jnp.where(own, -slope * (past - col // H).astype(F32), NEG_INF)

    for t in range(n_tok):
        k_refs = refs[t * n_pages:(t + 1) * n_pages]
        v_refs = refs[(n_tok + t) * n_pages:(n_tok + t + 1) * n_pages]
        qbd = jnp.where(grp == rid, jnp.broadcast_to(q_ref[t], (R, W_MIX)), 0.0) * (DK ** -0.5)
        s_new = jnp.sum(qbd * kn_ref[t], axis=1, keepdims=True)
        qx = sum(qbd[:, hh * DH:(hh + 1) * DH] for hh in range(H))
        vnx = sum(jnp.where(rhead == hh, vn_ref[t, :, hh * DH:(hh + 1) * DH], 0.0) for hh in range(H))
        s = jnp.concatenate([_dot_nt(qx, k_refs[j][...]) for j in range(n_pages)], axis=1)
        s = s + bias
        m = jnp.maximum(jnp.max(s, axis=1, keepdims=True), s_new)
        p = jnp.exp(s - m)
        p_new = jnp.exp(s_new - m)
        inv_l = 1.0 / (jnp.sum(p, axis=1, keepdims=True) + p_new)
        p = p * inv_l
        o8 = (p_new * inv_l) * vnx
        for j in range(n_pages):
            o8 = o8 + _dot(p[:, j * PR:(j + 1) * PR], v_refs[j][...])
        for hh in range(H):
            o = o8[2 * hh:2 * hh + 1, :] - lam * o8[2 * hh + 1:2 * hh + 2, :]
            o_ref[t, :, hh * DH:(hh + 1) * DH] = _rms(o, g_ref[...]) * (1.0 - LAM_INIT)


def _attn_step(bq, k_new, v_new, cache_k, cache_v, page_table, lam, slopes, g_subln):
    N = bq.shape[0]
    n_pool, page = cache_k.shape[0], cache_k.shape[1]
    n_pages = page_table.shape[1]
    n_tok = ATTN_STEP_TOKENS
    cache_k = cache_k.reshape(n_pool * page * H, DH)
    cache_v = cache_v.reshape(n_pool * page * H, DH)
    smem = pl.BlockSpec(memory_space=pltpu.SMEM)
    rowspec = pl.BlockSpec((n_tok, 1, W_MIX), lambda b, pt: (b, 0, 0))
    pspec = lambda t, j: pl.BlockSpec((page * H, DH), lambda b, pt: (pt[(b * n_tok + t) * n_pages + j], 0))
    pages = [pspec(t, j) for t in range(n_tok) for j in range(n_pages)]
    grid_spec = pltpu.PrefetchScalarGridSpec(
        num_scalar_prefetch=1,
        grid=(N // n_tok,),
        in_specs=[smem, smem, rowspec, rowspec, rowspec, pl.BlockSpec((1, DH), lambda b, pt: (0, 0))] + pages * 2,
        out_specs=rowspec,
    )
    out = pl.pallas_call(
        functools.partial(_attn_step_kernel, n_pages=n_pages, page=page, n_tok=n_tok),
        grid_spec=grid_spec,
        out_shape=jax.ShapeDtypeStruct((N, 1, W_MIX), F32),
        compiler_params=_cp(("arbitrary",)),
        name="attn_step",
    )(page_table.reshape(-1), lam.reshape(1), slopes, bq.reshape(N, 1, W_MIX), k_new.reshape(N, 1, W_MIX),
      v_new.reshape(N, 1, W_MIX), g_subln.reshape(1, DH), *([cache_k] * (n_tok * n_pages)),
      *([cache_v] * (n_tok * n_pages)))
    return out.reshape(N, W_MIX)


def _merge_kernel(oa_ref, ob_ref, gt_ref, x_ref, g1_ref, sc2_ref, sh2_ref, gpost_ref, gpre_ref,
                  wa_ref, wb_ref, wo_ref, wr_ref, br_ref, cnt0_ref, x1_ref, h3_ref, rt_ref, cnt_ref, run_ref,
                  *, n_part):
    i = pl.program_id(0)
    tm = x_ref.shape[0]
    tp = tm // n_part
    parts = [slice(p * tp, (p + 1) * tp) for p in range(n_part)]
    rows = lambda ref, r: ref[...] if ref.shape[0] == 1 else ref[r, :]

    @pl.when(i == 0)
    def _():
        run_ref[...] = cnt0_ref[...]

    pa = [_dot(oa_ref[r, :].astype(BF16), wa_ref[...]) for r in parts]
    pb = [_dot(ob_ref[r, :].astype(BF16), wb_ref[...]) for r in parts]
    merged = [_sigmoid(gt_ref[r, :D]) * pa[p] + _sigmoid(gt_ref[r, D:]) * pb[p] for p, r in enumerate(parts)]
    z = [_dot(m.astype(BF16), wo_ref[...]) for m in merged]
    h2 = []
    for p, r in enumerate(parts):
        x1 = x_ref[r, :] + rows(g1_ref, r) * _rms(z[p], gpost_ref[...])
        x1_ref[r, :] = x1
        h2.append(_rms(x1, gpre_ref[...]) * (1.0 + rows(sc2_ref, r)) + rows(sh2_ref, r))
        _store_row_tiles(h3_ref, h2[p], p * tp)
    logits = [_dot(h, wr_ref[...]) + br_ref[...] for h in h2]

    lane = lax.broadcasted_iota(jnp.int32, (tp, N_EXPERTS), 1).astype(F32)
    sels, idxs, gates, chosen = [], [], [], []
    for p in range(n_part):
        work = logits[p]
        sel_p, idx_p, val_p = [], [], []
        for _ in range(TOP_K):
            mx = jnp.max(work, axis=1, keepdims=True)
            idx = jnp.min(jnp.where(work == mx, lane, float(N_EXPERTS)), axis=1, keepdims=True)
            sel = lane == idx
            work = jnp.where(sel, -jnp.inf, work)
            sel_p.append(sel)
            idx_p.append(idx)
            val_p.append(mx)
        ex = [jnp.exp(v - val_p[0]) for v in val_p]
        inv_den = 1.0 / (ex[0] + ex[1] + ex[2] + ex[3])
        sels.append(sel_p)
        idxs.append(idx_p)
        gates.append([e * inv_den for e in ex])
        chosen.append(sum(s.astype(F32) for s in sel_p))

    chosen_all = jnp.concatenate(chosen, axis=0) if n_part > 1 else chosen[0]
    before = (lax.broadcasted_iota(jnp.int32, (tm, tm), 0) > lax.broadcasted_iota(jnp.int32, (tm, tm), 1))
    prior = _dot(before.astype(BF16), chosen_all.astype(BF16)) + run_ref[...]
    run_ref[...] = run_ref[...] + jnp.sum(chosen_all, axis=0, keepdims=True)
    cnt_ref[...] = run_ref[...]

    rlane = lax.broadcasted_iota(jnp.int32, (tp, LANES), 1)
    for p, r in enumerate(parts):
        rt = jnp.zeros((tp, LANES), F32)
        for kk in range(TOP_K):
            rank = jnp.sum(jnp.where(sels[p][kk], prior[r, :], 0.0), axis=1, keepdims=True)
            rt = jnp.where(rlane == RT_E + kk, idxs[p][kk], rt)
            rt = jnp.where(rlane == RT_RANK + kk, rank, rt)
            rt = jnp.where(rlane == RT_GATE + kk, gates[p][kk], rt)
        rt_ref[r, :] = rt


def _merge(oa, ob, gt, x2d, m3d, tiles_per_mod, g_post, g_pre, wa, wb, wo, wr, br, cnt0, tm):
    R = x2d.shape[0]
    r = m3d.shape[1]
    mod = lambda col: pl.BlockSpec((None, r, D), lambda i: (i // tiles_per_mod, 0, col))
    row = lambda w: pl.BlockSpec((tm, w), lambda i: (i, 0))
    full = lambda a, b: pl.BlockSpec((a, b), lambda i: (0, 0))
    return pl.pallas_call(
        functools.partial(_merge_kernel, n_part=2 if tm >= 256 else 1),
        grid=(R // tm,),
        in_specs=[row(W_MIX), row(W_MIX), row(2 * D), row(D), mod(2), mod(4), mod(3),
                  full(1, D), full(1, D), full(W_MIX, D), full(W_MIX, D), full(D, D),
                  full(D, N_EXPERTS), full(1, N_EXPERTS), full(1, N_EXPERTS)],
        out_specs=[row(D), pl.BlockSpec((tm * SUB, LANES), lambda i: (i, 0)), row(LANES), full(1, N_EXPERTS)],
        out_shape=[jax.ShapeDtypeStruct((R, D), F32),
                   jax.ShapeDtypeStruct((R * SUB, LANES), F32),
                   jax.ShapeDtypeStruct((R, LANES), F32),
                   jax.ShapeDtypeStruct((1, N_EXPERTS), F32)],
        scratch_shapes=[pltpu.VMEM((1, N_EXPERTS), F32)],
        compiler_params=_cp(("arbitrary",)),
        name="merge",
    )(oa, ob, gt, x2d, m3d, m3d, m3d, g_post.reshape(1, D), g_pre.reshape(1, D), wa, wb, wo, wr,
      br.reshape(1, N_EXPERTS), cnt0)


def _row_copy(src, s_row, dst, d_row, sem):
    return pltpu.make_async_copy(src.at[pl.ds(pl.multiple_of(s_row, SUB), SUB)],
                                 dst.at[pl.ds(pl.multiple_of(d_row, SUB), SUB)], sem)


def _dispatch_kernel(lo_ref, hi_ref, nu_ref, da_ref, db_ref, ha_ref, hb_ref, x_hbm, zero_ref, sem):
    i = pl.program_id(0)
    last = pl.num_programs(0) - 1
    unroll = 8

    def scatter_rows(d_ref, h_ref):
        n = h_ref.shape[0] // SUB

        def issue(j, carry):
            for u in range(unroll):
                t = j * unroll + u
                for kk in range(TOP_K):
                    _row_copy(h_ref, t * SUB, x_hbm, d_ref[t * TOP_K + kk], sem).start(priority=kk % 2)
            return carry

        lax.fori_loop(0, n // unroll, issue, 0)
        for _ in range(TOP_K):
            pltpu.make_async_copy(h_ref, x_hbm.at[pl.ds(0, n * SUB)], sem).wait()

    @pl.when(i < last)
    def _():
        scatter_rows(da_ref, ha_ref)

    @pl.when(i == last)
    def _():
        scatter_rows(db_ref, hb_ref)
        zero_ref[...] = jnp.zeros_like(zero_ref)

        def per_expert(e, carry):
            zrow = lambda r: _row_copy(zero_ref, 0, x_hbm, r * SUB, sem)
            lax.fori_loop(lo_ref[e], hi_ref[e], lambda r, c: (zrow(r).start(), c)[1], 0)
            lax.fori_loop(lo_ref[e], hi_ref[e], lambda r, c: (zrow(r).wait(), c)[1], 0)
            return carry

        lax.fori_loop(0, N_EXPERTS, per_expert, 0)

        blk = zero_ref.shape[0]
        zblk = lambda b: pltpu.make_async_copy(zero_ref, x_hbm.at[pl.ds(pl.multiple_of(b * blk, blk), blk)], sem)
        lax.fori_loop(nu_ref[0], x_hbm.shape[0] // blk, lambda b, c: (zblk(b).start(), c)[1], 0)
        lax.fori_loop(nu_ref[0], x_hbm.shape[0] // blk, lambda b, c: (zblk(b).wait(), c)[1], 0)


def _dispatch(h3_a, h3_b, dest8_a, dest8_b, pad_lo, pad_hi, n_used, n_rows):
    tm = DISPATCH_ROWS
    n_a = h3_a.shape[0] // (tm * SUB)
    clamp = lambda i: jnp.minimum(i, n_a - 1)
    grid_spec = pltpu.PrefetchScalarGridSpec(
        num_scalar_prefetch=3,
        grid=(n_a + 1,),
        in_specs=[pl.BlockSpec((tm * TOP_K,), lambda i, *_: (clamp(i),), memory_space=pltpu.SMEM),
                  pl.BlockSpec(dest8_b.shape, lambda i, *_: (0,), memory_space=pltpu.SMEM),
                  pl.BlockSpec((tm * SUB, LANES), lambda i, *_: (clamp(i), 0)),
                  pl.BlockSpec(h3_b.shape, lambda i, *_: (0, 0))],
        out_specs=pl.BlockSpec(memory_space=pl.ANY),
        scratch_shapes=[pltpu.VMEM((MOE_BLOCK * SUB, LANES), F32), pltpu.SemaphoreType.DMA],
    )
    return pl.pallas_call(
        _dispatch_kernel,
        grid_spec=grid_spec,
        out_shape=jax.ShapeDtypeStruct((n_rows * SUB, LANES), F32),
        compiler_params=_cp(("arbitrary",)),
        name="dispatch",
    )(pad_lo, pad_hi, n_used, dest8_a, dest8_b, h3_a, h3_b)


def _expert_kernel(b0_ref, nb_ref, nu_ref, x_hbm, wgu_ref, bgu_ref, wd_ref, bd_ref, y_hbm,
                   xbuf, ybuf, wgu_b, wd_b, xsem, ysem):
    e = pl.program_id(0)
    rows = xbuf.shape[1]
    b0, nb = b0_ref[e], nb_ref[e]
    x_copy = lambda j, slot: pltpu.make_async_copy(
        x_hbm.at[pl.ds(pl.multiple_of((b0 + j) * rows, rows), rows)], xbuf.at[slot], xsem.at[slot])
    y_copy = lambda j, slot: pltpu.make_async_copy(
        ybuf.at[slot], y_hbm.at[pl.ds(pl.multiple_of((b0 + j) * rows, rows), rows)], ysem.at[slot])

    @pl.when(nb > 0)
    def _():
        x_copy(0, 0).start(priority=1)
        wgu_b[...] = wgu_ref[...].astype(BF16)
        wd_b[...] = wd_ref[...].astype(BF16)

        def block(j, carry):
            slot = j % 2
            x_copy(j, slot).wait()

            @pl.when(j + 1 < nb)
            def _():
                x_copy(j + 1, 1 - slot).start(priority=1)

            @pl.when(j >= 2)
            def _():
                y_copy(j - 2, slot).wait()

            gu = _dot(_load_row_tiles(xbuf.at[slot]).astype(BF16), wgu_b[...]) + bgu_ref[...]
            glu = jnp.minimum(gu[:, :D], SWIGLU_LIMIT)
            lin = jnp.clip(gu[:, D:], -SWIGLU_LIMIT, SWIGLU_LIMIT)
            act = (lin + 1.0) * glu * _sigmoid(SWIGLU_ALPHA * glu)
            _store_row_tiles(ybuf.at[slot], _dot(act.astype(BF16), wd_b[...]) + bd_ref[...])
            y_copy(j, slot).start(priority=1)
            return carry

        lax.fori_loop(0, nb, block, 0)

        @pl.when(nb >= 2)
        def _():
            y_copy(nb - 2, nb % 2).wait()
        y_copy(nb - 1, (nb - 1) % 2).wait()

    @pl.when(e == pl.num_programs(0) - 1)
    def _():
        ybuf[0] = jnp.zeros(ybuf.shape[1:], F32)
        zblk = lambda b: pltpu.make_async_copy(
            ybuf.at[0], y_hbm.at[pl.ds(pl.multiple_of(b * rows, rows), rows)], ysem.at[0])
        n_blocks = y_hbm.shape[0] // rows
        lax.fori_loop(nu_ref[0], n_blocks, lambda b, c: (zblk(b).start(), c)[1], 0)
        lax.fori_loop(nu_ref[0], n_blocks, lambda b, c: (zblk(b).wait(), c)[1], 0)


def _experts(x3, blk_start, blk_count, n_used, w_gate_up, b_gate_up, w_down, b_down):
    rows = MOE_BLOCK * SUB
    wspec = lambda a, b: pl.BlockSpec((None, a, b), lambda e, *_: (e, 0, 0))
    grid_spec = pltpu.PrefetchScalarGridSpec(
        num_scalar_prefetch=3,
        grid=(N_EXPERTS,),
        in_specs=[pl.BlockSpec(memory_space=pl.ANY), wspec(D, 2 * D), wspec(1, 2 * D), wspec(D, D), wspec(1, D)],
        out_specs=pl.BlockSpec(memory_space=pl.ANY),
        scratch_shapes=[pltpu.VMEM((2, rows, LANES), F32), pltpu.VMEM((2, rows, LANES), F32),
                        pltpu.VMEM((D, 2 * D), BF16), pltpu.VMEM((D, D), BF16),
                        pltpu.SemaphoreType.DMA((2,)), pltpu.SemaphoreType.DMA((2,))],
    )
    return pl.pallas_call(
        _expert_kernel,
        grid_spec=grid_spec,
        out_shape=jax.ShapeDtypeStruct(x3.shape, F32),
        compiler_params=_cp(("arbitrary",)),
        name="experts",
    )(blk_start, blk_count, n_used, x3, w_gate_up, b_gate_up.reshape(N_EXPERTS, 1, 2 * D), w_down,
      b_down.reshape(N_EXPERTS, 1, D))


def _final_kernel(dcur_ref, dnext_ref, x1_ref, g2_ref, gpost_ref, rt_ref, y_hbm, o_ref, ybuf, sem):
    i = pl.program_id(0)
    n = pl.num_programs(0)
    tm = x1_ref.shape[0]
    slot = i % 2
    unroll = 8

    def issue(d_ref, s):
        def body(j, carry):
            for u in range(unroll):
                t = j * unroll + u
                for kk in range(TOP_K):
                    _row_copy(y_hbm, d_ref[t * TOP_K + kk], ybuf.at[s, kk], t * SUB, sem.at[s]).start(priority=kk % 2)
            return carry
        lax.fori_loop(0, tm // unroll, body, 0)

    @pl.when(i == 0)
    def _():
        issue(dcur_ref, 0)

    @pl.when(i + 1 < n)
    def _():
        issue(dnext_ref, 1 - slot)

    for kk in range(TOP_K):
        pltpu.make_async_copy(y_hbm.at[pl.ds(0, tm * SUB)], ybuf.at[slot, kk], sem.at[slot]).wait()

    y = jnp.zeros((tm, D), F32)
    for kk in range(TOP_K):
        y = y + rt_ref[:, RT_GATE + kk:RT_GATE + kk + 1] * _load_row_tiles(ybuf.at[slot, kk])
    o_ref[...] = x1_ref[...] + g2_ref[...] * _rms(y, gpost_ref[...])


def _final(x1, m3d, tiles_per_mod, g_post, route, dest_flat, y3, row0, tm):
    R = x1.shape[0]
    r = m3d.shape[1]
    off = row0 // tm
    nsteps = R // tm
    row = pl.BlockSpec((tm, D), lambda i: (i, 0))
    dspec = lambda nxt: pl.BlockSpec((tm * TOP_K,), lambda i: (jnp.minimum(i + nxt, nsteps - 1) + off,),
                                     memory_space=pltpu.SMEM)
    return pl.pallas_call(
        _final_kernel,
        grid=(nsteps,),
        in_specs=[dspec(0), dspec(1), row,
                  pl.BlockSpec((None, r, D), lambda i: (i // tiles_per_mod, 0, 5)),
                  pl.BlockSpec((1, D), lambda i: (0, 0)),
                  pl.BlockSpec((tm, LANES), lambda i: (i, 0)),
                  pl.BlockSpec(memory_space=pl.ANY)],
        out_specs=row,
        out_shape=jax.ShapeDtypeStruct((R, D), F32),
        scratch_shapes=[pltpu.VMEM((2, TOP_K, tm * SUB, LANES), F32), pltpu.SemaphoreType.DMA((2,))],
        compiler_params=_cp(("arbitrary",)),
        name="final",
    )(dest_flat, dest_flat, x1, m3d, g_post.reshape(1, D), route, y3)


def kernel(x_prompt, x_sample, cache_k, cache_v, state_hgrn, page_table, c_prompt, c_sample, w_mod, b_mod, g_pre_mix, g_post_mix, g_pre_ffn, g_post_ffn, w_in, hgrn_lb, g_hgrn, lam_q1, lam_k1, lam_q2, lam_k2, g_subln, w_proj_a, w_proj_b, w_out, w_router, b_router, w_gate_up, b_gate_up, w_down, b_down):
    B, L, _ = x_prompt.shape
    NS = x_sample.shape[0]
    T = B * L
    T_all = T + NS
    l = 0

    lb = jnp.cumsum(jax.nn.softmax(hgrn_lb.astype(F32), axis=0), axis=0)[l]
    lam = (jnp.exp(jnp.sum(lam_q1[l] * lam_k1[l])) - jnp.exp(jnp.sum(lam_q2[l] * lam_k2[l])) + LAM_INIT).astype(F32)
    slopes = jnp.exp2(-8.0 * jnp.arange(1, H + 1, dtype=F32) / H)

    m_all = _modulation(jnp.concatenate([c_prompt, c_sample], axis=0), w_mod[l], b_mod[l])
    m_p = m_all[:B].reshape(B, 1, 6 * D)
    m_s = m_all[B:].reshape(1, NS, 6 * D)

    w_in_b = w_in[l].astype(BF16)
    tm = 256
    tpm = L // tm
    xp = x_prompt.reshape(T, D)
    xs = x_sample.reshape(NS, D)
    merge_w = (g_post_mix[l], g_pre_ffn[l], w_proj_a[l].astype(BF16), w_proj_b[l].astype(BF16),
               w_out[l].astype(BF16), w_router[l], b_router[l])

    hg_p, bq_p, k_p, v_p, gt_p, k4_p, v4_p = _inproj(xp, m_p, tpm, g_pre_mix[l], w_in_b, tm)
    oa_p, s_p = _hgrn_prompt(hg_p.reshape(B, L, 4 * W_MIX), lb, g_hgrn[l])
    ob_p = _attn_prompt(bq_p, k_p, v_p, lam, slopes, g_subln[l], B, L)
    x1_p, h3_p, rt_p, cnt = _merge(oa_p.reshape(T, W_MIX), ob_p, gt_p, xp, m_p, L // MERGE_ROWS, *merge_w,
                                   jnp.zeros((1, N_EXPERTS), F32), MERGE_ROWS)

    hg_s, bq_s, k_s, v_s, gt_s, k4_s, v4_s = _inproj(xs, m_s, 1, g_pre_mix[l], w_in_b, NS)
    oa_s, s_s = _hgrn_step(hg_s, state_hgrn[l], lb, g_hgrn[l])
    ob_s = _attn_step(bq_s, k_s, v_s, cache_k[l], cache_v[l], page_table, lam, slopes, g_subln[l])
    x1_s, h3_s, rt_s, cnt = _merge(oa_s, ob_s, gt_s, xs, m_s, 1, *merge_w, cnt, NS)

    bm = MOE_BLOCK
    counts = cnt.reshape(N_EXPERTS).astype(jnp.int32)
    padded = (counts + bm - 1) // bm * bm
    pad_end = jnp.cumsum(padded)
    pad_start = pad_end - padded
    n_blocks = -(-T_all * TOP_K // bm) + N_EXPERTS
    route = jnp.concatenate([rt_p[:, :RT_GATE], rt_s[:, :RT_GATE]], axis=0)
    e_idx = route[:, RT_E:RT_E + TOP_K].astype(jnp.int32)
    rank = route[:, RT_RANK:RT_RANK + TOP_K].astype(jnp.int32)
    e_iota = jnp.arange(N_EXPERTS, dtype=jnp.int32)
    dest = rank + jnp.sum(jnp.where(e_idx[..., None] == e_iota, pad_start, 0), axis=-1)
    dest_flat = dest.reshape(-1) * SUB
    n_used = pad_end[-1:] // bm

    x3 = _dispatch(h3_p, h3_s, dest_flat[:T * TOP_K], dest_flat[T * TOP_K:], pad_start + counts, pad_end, n_used,
                   n_blocks * bm)
    y3 = _experts(x3, pad_start // bm, padded // bm, n_used, w_gate_up[l], b_gate_up[l], w_down[l], b_down[l])
    y_p = _final(x1_p, m_p, tpm, g_post_ffn[l], rt_p, dest_flat, y3, 0, tm)
    y_s = _final(x1_s, m_s, 1, g_post_ffn[l], rt_s, dest_flat, y3, T, NS)

    return (y_p.reshape(B, L, D), y_s.reshape(NS, 1, D),
            k4_p.reshape(1, B, L, H, DH), v4_p.reshape(1, B, L, H, DH), s_p.reshape(1, B, H, DH, DH),
            k4_s.reshape(1, NS, 1, H, DH), v4_s.reshape(1, NS, 1, H, DH), s_s.reshape(1, NS, H, DH, DH))
```

```python
import functools
import math

import jax
import jax.numpy as jnp
from jax import lax
from jax.experimental import pallas as pl
from jax.experimental.pallas import tpu as pltpu

F32 = jnp.float32
BF16 = jnp.bfloat16

D = 1024
LANES = 128
SUB = D // LANES
H = 4
DH = 128
DK = 64
W_MIX = H * DH
HGRN_CHUNK = 64
N_EXPERTS = 32
TOP_K = 4
SWIGLU_LIMIT = 7.0
SWIGLU_ALPHA = 1.702
EPS = 1e-6
NEG_INF = -1e30
LOG2E = 1.4426950408889634
LAM_INIT = 0.8 - 0.6 * math.exp(-0.3 * 0)
C_HG = (0, 4 * W_MIX)
C_BQ = (4 * W_MIX, 5 * W_MIX)
C_BK = (5 * W_MIX, 6 * W_MIX)
C_BV = (6 * W_MIX, 7 * W_MIX)
C_GT = (7 * W_MIX, 7 * W_MIX + 2 * D)
D_IN = 7 * W_MIX + 2 * D
RT_E, RT_RANK, RT_GATE = 0, TOP_K, 2 * TOP_K

VMEM_LIMIT = 56 * 1024 * 1024
MOE_BLOCK = 256
DISPATCH_ROWS = 512
MERGE_ROWS = 512
ATTN_STEP_TOKENS = 2
ATT_G = 256
ATT_NG = 8
ATT_TK = 256
ATT_VR = DH + 8


def _cp(sem, vmem=VMEM_LIMIT):
    return pltpu.CompilerParams(dimension_semantics=sem, vmem_limit_bytes=vmem)


def _sigmoid(x):
    return 1.0 / (1.0 + jnp.exp(-x))


def _rms(x, g):
    return x * lax.rsqrt(jnp.mean(x * x, axis=-1, keepdims=True) + EPS) * g


def _dot(a, b):
    return jnp.dot(a, b, preferred_element_type=F32)


def _dot_nt(a, b):
    return lax.dot_general(a, b, (((1,), (1,)), ((), ())), preferred_element_type=F32)


def _dot_tn(a, b):
    return lax.dot_general(a, b, (((0,), (0,)), ((), ())), preferred_element_type=F32)


def _store_row_tiles(ref2, x2d, row0=0):
    n = x2d.shape[0]
    for j in range(SUB):
        ref2[pl.ds(row0 * SUB + j, n, stride=SUB), :] = x2d[:, j * LANES:(j + 1) * LANES]


def _load_row_tiles(ref2):
    n = ref2.shape[0] // SUB
    return jnp.concatenate([ref2[pl.ds(j, n, stride=SUB), :] for j in range(SUB)], axis=1)


def _mod_kernel(c_ref, w_ref, b_ref, o_ref):
    c = c_ref[...]
    o_ref[...] = _dot(c * _sigmoid(c), w_ref[...]) + b_ref[...]


def _modulation(c_all, w_mod, b_mod):
    n = c_all.shape[0]
    tn = 1024
    return pl.pallas_call(
        _mod_kernel,
        grid=(6 * D // tn,),
        in_specs=[pl.BlockSpec((n, D), lambda j: (0, 0)),
                  pl.BlockSpec((D, tn), lambda j: (0, j)),
                  pl.BlockSpec((1, tn), lambda j: (0, j))],
        out_specs=pl.BlockSpec((n, tn), lambda j: (0, j)),
        out_shape=jax.ShapeDtypeStruct((n, 6 * D), F32),
        compiler_params=_cp(("arbitrary",)),
        name="modulation",
    )(c_all, w_mod, b_mod.reshape(1, 6 * D))


def _inproj_kernel(x_ref, sc_ref, sh_ref, g_ref, w_ref, hg_ref, bq_ref, k_ref, v_ref, gt_ref, k4_ref, v4_ref):
    h = _rms(x_ref[...], g_ref[...]) * (1.0 + sc_ref[...]) + sh_ref[...]
    hb = h.astype(BF16)
    mm = lambda c: _dot(hb, w_ref[:, c[0]:c[1]])
    hg_ref[...] = mm(C_HG)
    bq_ref[...] = mm(C_BQ)
    gt_ref[...] = mm(C_GT)
    for c, o_ref, o4_ref in ((C_BK, k_ref, k4_ref), (C_BV, v_ref, v4_ref)):
        r = mm(c)
        o_ref[...] = r
        o4 = o4_ref.reshape(r.shape[0] * H, DH)
        for hh in range(H):
            o4[pl.ds(hh, r.shape[0], stride=H), :] = r[:, hh * DH:(hh + 1) * DH]


def _inproj(x2d, m3d, tiles_per_mod, g_pre, w_in_bf16, tm):
    R = x2d.shape[0]
    r = m3d.shape[1]
    mod = lambda col: pl.BlockSpec((None, r, D), lambda i: (i // tiles_per_mod, 0, col))
    row = lambda w: pl.BlockSpec((tm, w), lambda i: (i, 0))
    row4 = pl.BlockSpec((tm, H, DH), lambda i: (i, 0, 0))
    widths = (4 * W_MIX, W_MIX, W_MIX, W_MIX, 2 * D)
    kv4 = jax.ShapeDtypeStruct((R, H, DH), F32)
    return pl.pallas_call(
        _inproj_kernel,
        grid=(R // tm,),
        in_specs=[row(D), mod(1), mod(0),
                  pl.BlockSpec((1, D), lambda i: (0, 0)),
                  pl.BlockSpec((D, D_IN), lambda i: (0, 0))],
        out_specs=[row(w) for w in widths] + [row4, row4],
        out_shape=[jax.ShapeDtypeStruct((R, w), F32) for w in widths] + [kv4, kv4],
        compiler_params=_cp(("arbitrary",)),
        name="inproj",
    )(x2d, m3d, m3d, g_pre.reshape(1, D), w_in_bf16)


def _hgrn_prompt_kernel(lb_ref, gn_ref, q_ref, f_ref, i_ref, g_ref, o_ref, sout_ref, st_ref):
    c = pl.program_id(0)
    nb = q_ref.shape[0]
    C = HGRN_CHUNK

    @pl.when(c == 0)
    def _():
        st_ref[...] = jnp.zeros_like(st_ref)

    row = lax.broadcasted_iota(jnp.int32, (C, DH), 0)
    causal = lax.broadcasted_iota(jnp.int32, (C, C), 0) >= lax.broadcasted_iota(jnp.int32, (C, C), 1)

    def per_batch_pair(bp, carry):
        ch = [(2 * bp + d, h, slice(h * DH, (h + 1) * DH)) for d in range(2) for h in range(H)]
        qd, kd, k2, v, eg = [], [], [], [], []
        for b, h, hs in ch:
            lb = lb_ref[:, hs]
            f = lb + (1.0 - lb) * _sigmoid(f_ref[b, :, hs])
            k = 1.0 - f
            gc = jnp.log(f)
            for s in (1, 2, 4, 8, 16, 32):
                gc = gc + jnp.where(row >= s, pltpu.roll(gc, s, axis=0), 0.0)
            g_last = gc[C - 1:C, :]
            qd.append(q_ref[b, :, hs] * jnp.exp(gc))
            kd.append(k * jnp.exp(-gc))
            k2.append(k * jnp.exp(g_last - gc))
            eg.append(jnp.exp(g_last))
            v.append(i_ref[b, :, hs])
        n = len(ch)
        st = [st_ref[b, h] for b, h, _ in ch]
        a = [jnp.where(causal, _dot_nt(qd[i], kd[i]), 0.0) for i in range(n)]
        o_inter = [_dot_nt(qd[i], st[i]) for i in range(n)]
        kv_t = [_dot_tn(v[i], k2[i]) for i in range(n)]
        o_intra = [_dot(a[i], v[i]) for i in range(n)]
        for i, (b, h, hs) in enumerate(ch):
            st_ref[b, h] = eg[i] * st[i] + kv_t[i]
            o = _rms(o_intra[i] + o_inter[i], gn_ref[:, hs])
            ag = g_ref[b, :, hs]
            o_ref[b, :, hs] = o * (ag * _sigmoid(ag))
        return carry

    lax.fori_loop(0, nb // 2, per_batch_pair, 0)

    @pl.when(c == pl.num_programs(0) - 1)
    def _():
        def fin(b, carry):
            for h in range(H):
                sout_ref[b, h] = st_ref[b, h].T
            return carry
        lax.fori_loop(0, nb, fin, 0)


def _hgrn_prompt(hg3, lb, g_hgrn):
    B, L, _ = hg3.shape
    C = HGRN_CHUNK
    col = lambda j: pl.BlockSpec((B, C, W_MIX), lambda c: (0, c, j))
    vec = pl.BlockSpec((1, W_MIX), lambda c: (0, 0))
    return pl.pallas_call(
        _hgrn_prompt_kernel,
        grid=(L // C,),
        in_specs=[vec, vec, col(0), col(1), col(2), col(3)],
        out_specs=[pl.BlockSpec((B, C, W_MIX), lambda c: (0, c, 0)),
                   pl.BlockSpec((B, H, DH, DH), lambda c: (0, 0, 0, 0))],
        out_shape=[jax.ShapeDtypeStruct((B, L, W_MIX), F32),
                   jax.ShapeDtypeStruct((B, H, DH, DH), F32)],
        scratch_shapes=[pltpu.VMEM((B, H, DH, DH), F32)],
        compiler_params=_cp(("arbitrary",)),
        name="hgrn_prompt",
    )(lb.reshape(1, W_MIX), g_hgrn.reshape(1, W_MIX), hg3, hg3, hg3, hg3)


def _hgrn_step_kernel(lb_ref, gn_ref, x_ref, s_ref, o_ref, sout_ref):
    nt = x_ref.shape[0]
    ones = jnp.ones((8, DH), F32)
    pad = jnp.zeros((6, DH), F32)

    def col(r):
        hi = r.astype(BF16).astype(F32)
        return _dot_tn(jnp.concatenate([hi, r - hi, pad], axis=0), ones)

    for t in range(nt):
        for h in range(H):
            seg = lambda j: x_ref[t:t + 1, j * W_MIX + h * DH: j * W_MIX + (h + 1) * DH]
            lb = lb_ref[:, h * DH:(h + 1) * DH]
            q, v, ag = seg(0), seg(2), seg(3)
            f = lb + (1.0 - lb) * _sigmoid(seg(1))
            f_col = col(f)
            s_new = f_col * s_ref[t, h] + (1.0 - f_col) * v
            sout_ref[t, h] = s_new
            o = jnp.sum(col(q) * s_new, axis=0, keepdims=True)
            o = _rms(o, gn_ref[:, h * DH:(h + 1) * DH])
            o_ref[t:t + 1, h * DH:(h + 1) * DH] = o * (ag * _sigmoid(ag))


def _hgrn_step(hg, state, lb, g_hgrn):
    N = hg.shape[0]
    nt = 8
    vec = pl.BlockSpec((1, W_MIX), lambda i: (0, 0))
    st = pl.BlockSpec((nt, H, DH, DH), lambda i: (i, 0, 0, 0))
    return pl.pallas_call(
        _hgrn_step_kernel,
        grid=(N // nt,),
        in_specs=[vec, vec, pl.BlockSpec((nt, 4 * W_MIX), lambda i: (i, 0)), st],
        out_specs=[pl.BlockSpec((nt, W_MIX), lambda i: (i, 0)), st],
        out_shape=[jax.ShapeDtypeStruct((N, W_MIX), F32),
                   jax.ShapeDtypeStruct((N, H, DH, DH), F32)],
        compiler_params=_cp(("arbitrary",)),
        name="hgrn_step",
    )(lb.reshape(1, W_MIX), g_hgrn.reshape(1, W_MIX), hg, state)


def _attn_prompt_kernel(lam_ref, sl_ref, q_ref, k_ref, v_ref, g_ref, o_ref, qt_ref, vt_ref, *chain_refs):
    G, NG, tk = ATT_G, ATT_NG, ATT_TK
    tq = G * NG
    m_refs = [chain_refs[mp * NG:(mp + 1) * NG] for mp in range(2)]
    acc_refs = [chain_refs[(2 + mp) * NG:(3 + mp) * NG] for mp in range(2)]
    slope = sl_ref[pl.program_id(1)] * LOG2E
    lam = lam_ref[0]
    L = q_ref.shape[0]
    q_t = q_ref[...].T * (DK ** -0.5 * LOG2E)
    sub = lax.broadcasted_iota(jnp.int32, (DH, L), 0)
    for mp in range(2):
        qm_t = jnp.where(sub < DK, q_t, 0.0) if mp == 0 else jnp.where(sub >= DK, q_t, 0.0)
        for t in range(L // G):
            qt_ref[mp, t] = qm_t[:, t * G:(t + 1) * G]
    v_t = v_ref[...].T
    for kb in range(L // tk):
        vt_ref[kb, :DH, :] = v_t[:, kb * tk:(kb + 1) * tk]
        vt_ref[kb, DH:, :] = jnp.ones((ATT_VR - DH, tk), F32)
    idiff = lax.broadcasted_iota(jnp.int32, (tk, G), 1) - lax.broadcasted_iota(jnp.int32, (tk, G), 0)
    dmat = idiff.astype(F32) * slope

    def fold(qi, kb, chains):
        k0 = pl.multiple_of(kb * tk, tk)
        kt = k_ref[pl.ds(k0, tk), :]
        scores = [_dot(kt, qt_ref[mp, qi * NG + g]) for mp, g, _ in chains]
        probs = []
        for (mp, g, thresh), s in zip(chains, scores):
            s = s - dmat
            if thresh is not None:
                s = jnp.where(idiff >= thresh, s, NEG_INF)
            c = slope * ((qi * NG + g) * G - k0).astype(F32)
            m = m_refs[mp][g][...]
            m_new = jnp.maximum(m, jnp.max(s, axis=0, keepdims=True) - c)
            m_refs[mp][g][...] = m_new
            probs.append((jnp.exp2(s - (m_new + c)), jnp.exp2(m - m_new)))
        va = vt_ref[kb]
        pvs = [_dot(va, p) for p, _ in probs]
        for (mp, g, _), (_, alpha), pv in zip(chains, probs, pvs):
            acc_refs[mp][g][...] = alpha * acc_refs[mp][g][...] + pv

    def q_tile(qi, carry):
        for mp in range(2):
            for g in range(NG):
                m_refs[mp][g][...] = jnp.full((1, G), NEG_INF, F32)
                acc_refs[mp][g][...] = jnp.zeros((ATT_VR, G), F32)

        every = [(mp, g, None) for mp in range(2) for g in range(NG)]
        n_full = qi * (tq // tk)
        lax.fori_loop(0, n_full, lambda kb, c: (fold(qi, kb, every), c)[1], 0)
        for j in range(tq // tk):
            chains = []
            for mp in range(2):
                for g in range(NG):
                    thresh = j * tk - G * g
                    if thresh <= G - 1:
                        chains.append((mp, g, None if thresh <= -(tk - 1) else thresh))
            fold(qi, n_full + j, chains)


        for g in range(NG):
            a0, a1 = acc_refs[0][g][...], acc_refs[1][g][...]
            o_t = a0[:DH] / a0[DH:DH + 1] - lam * (a1[:DH] / a1[DH:DH + 1])
            o_t = o_t * lax.rsqrt(jnp.mean(o_t * o_t, axis=0, keepdims=True) + EPS)
            q0 = pl.multiple_of((qi * NG + g) * G, G)
            o_ref[pl.ds(q0, G), :] = o_t.T * (g_ref[...] * (1.0 - LAM_INIT))
        return carry

    lax.fori_loop(0, L // tq, q_tile, 0)


def _attn_prompt(bq, bk, bv, lam, slopes, g_subln, B, L):
    smem = pl.BlockSpec(memory_space=pltpu.SMEM)
    spec = pl.BlockSpec((L, DH), lambda b, h: (b, h))
    return pl.pallas_call(
        _attn_prompt_kernel,
        grid=(B, H),
        in_specs=[smem, smem, spec, spec, spec, pl.BlockSpec((1, DH), lambda b, h: (0, 0))],
        out_specs=spec,
        out_shape=jax.ShapeDtypeStruct((B * L, W_MIX), F32),
        scratch_shapes=[pltpu.VMEM((2, L // ATT_G, DH, ATT_G), F32),
                        pltpu.VMEM((L // ATT_TK, ATT_VR, ATT_TK), F32)]
                       + [pltpu.VMEM((1, ATT_G), F32)] * (2 * ATT_NG)
                       + [pltpu.VMEM((ATT_VR, ATT_G), F32)] * (2 * ATT_NG),
        compiler_params=_cp(("arbitrary", "arbitrary")),
        name="attn_prompt",
    )(lam.reshape(1), slopes, bq, bk, bv, g_subln.reshape(1, DH))


def _attn_step_kernel(pt_ref, lam_ref, sl_ref, q_ref, kn_ref, vn_ref, g_ref, *refs, n_pages, page, n_tok):
    o_ref = refs[2 * n_tok * n_pages]
    PR = page * H
    past = n_pages * page
    lam = lam_ref[0]
    R = 2 * H
    grp = lax.broadcasted_iota(jnp.int32, (R, W_MIX), 1) // DK
    rid = lax.broadcasted_iota(jnp.int32, (R, W_MIX), 0)
    rhead = lax.broadcasted_iota(jnp.int32, (R, DH), 0) // 2
    slope = jnp.concatenate([jnp.full((2, 1), sl_ref[hh], F32) for hh in range(H)], axis=0)
    col = lax.broadcasted_iota(jnp.int32, (R, n_pages * PR), 1)
    own = (col % H) == (lax.broadcasted_iota(jnp.int32, (R, n_pages * PR), 0) // 2)
    bias = jnp.where(own, -slope * (past - col // H).astype(F32), NEG_INF)

    for t in range(n_tok):
        k_refs = refs[t * n_pages:(t + 1) * n_pages]
        v_refs = refs[(n_tok + t) * n_pages:(n_tok + t + 1) * n_pages]
        qbd = jnp.where(grp == rid, jnp.broadcast_to(q_ref[t], (R, W_MIX)), 0.0) * (DK ** -0.5)
        s_new = jnp.sum(qbd * kn_ref[t], axis=1, keepdims=True)
        qx = sum(qbd[:, hh * DH:(hh + 1) * DH] for hh in range(H))
        vnx = sum(jnp.where(rhead == hh, vn_ref[t, :, hh * DH:(hh + 1) * DH], 0.0) for hh in range(H))
        s = jnp.concatenate([_dot_nt(qx, k_refs[j][...]) for j in range(n_pages)], axis=1)
        s = s + bias
        m = jnp.maximum(jnp.max(s, axis=1, keepdims=True), s_new)
        p = jnp.exp(s - m)
        p_new = jnp.exp(s_new - m)
        inv_l = 1.0 / (jnp.sum(p, axis=1, keepdims=True) + p_new)
        p = p * inv_l
        o8 = (p_new * inv_l) * vnx
        for j in range(n_pages):
            o8 = o8 + _dot(p[:, j * PR:(j + 1) * PR], v_refs[j][...])
        for hh in range(H):
            o = o8[2 * hh:2 * hh + 1, :] - lam * o8[2 * hh + 1:2 * hh + 2, :]
            o_ref[t, :, hh * DH:(hh + 1) * DH] = _rms(o, g_ref[...]) * (1.0 - LAM_INIT)


def _attn_step(bq, k_new, v_new, cache_k, cache_v, page_table, lam, slopes, g_subln):
    N = bq.shape[0]
    n_pool, page = cache_k.shape[0], cache_k.shape[1]
    n_pages = page_table.shape[1]
    n_tok = ATTN_STEP_TOKENS
    cache_k = cache_k.reshape(n_pool * page * H, DH)
    cache_v = cache_v.reshape(n_pool * page * H, DH)
    smem = pl.BlockSpec(memory_space=pltpu.SMEM)
    rowspec = pl.BlockSpec((n_tok, 1, W_MIX), lambda b, pt: (b, 0, 0))
    pspec = lambda t, j: pl.BlockSpec((page * H, DH), lambda b, pt: (pt[(b * n_tok + t) * n_pages + j], 0))
    pages = [pspec(t, j) for t in range(n_tok) for j in range(n_pages)]
    grid_spec = pltpu.PrefetchScalarGridSpec(
        num_scalar_prefetch=1,
        grid=(N // n_tok,),
        in_specs=[smem, smem, rowspec, rowspec, rowspec, pl.BlockSpec((1, DH), lambda b, pt: (0, 0))] + pages * 2,
        out_specs=rowspec,
    )
    out = pl.pallas_call(
        functools.partial(_attn_step_kernel, n_pages=n_pages, page=page, n_tok=n_tok),
        grid_spec=grid_spec,
        out_shape=jax.ShapeDtypeStruct((N, 1, W_MIX), F32),
        compiler_params=_cp(("arbitrary",)),
        name="attn_step",
    )(page_table.reshape(-1), lam.reshape(1), slopes, bq.reshape(N, 1, W_MIX), k_new.reshape(N, 1, W_MIX),
      v_new.reshape(N, 1, W_MIX), g_subln.reshape(1, DH), *([cache_k] * (n_tok * n_pages)),
      *([cache_v] * (n_tok * n_pages)))
    return out.reshape(N, W_MIX)


def _merge_kernel(oa_ref, ob_ref, gt_ref, x_ref, g1_ref, sc2_ref, sh2_ref, gpost_ref, gpre_ref,
                  wa_ref, wb_ref, wo_ref, wr_ref, br_ref, cnt0_ref, x1_ref, h3_ref, rt_ref, cnt_ref, run_ref,
                  *, n_part):
    i = pl.program_id(0)
    tm = x_ref.shape[0]
    tp = tm // n_part
    parts = [slice(p * tp, (p + 1) * tp) for p in range(n_part)]
    rows = lambda ref, r: ref[...] if ref.shape[0] == 1 else ref[r, :]

    @pl.when(i == 0)
    def _():
        run_ref[...] = cnt0_ref[...]

    pa = [_dot(oa_ref[r, :].astype(BF16), wa_ref[...]) for r in parts]
    pb = [_dot(ob_ref[r, :].astype(BF16), wb_ref[...]) for r in parts]
    merged = [_sigmoid(gt_ref[r, :D]) * pa[p] + _sigmoid(gt_ref[r, D:]) * pb[p] for p, r in enumerate(parts)]
    z = [_dot(m.astype(BF16), wo_ref[...]) for m in merged]
    h2 = []
    for p, r in enumerate(parts):
        x1 = x_ref[r, :] + rows(g1_ref, r) * _rms(z[p], gpost_ref[...])
        x1_ref[r, :] = x1
        h2.append(_rms(x1, gpre_ref[...]) * (1.0 + rows(sc2_ref, r)) + rows(sh2_ref, r))
        _store_row_tiles(h3_ref, h2[p], p * tp)
    logits = [_dot(h, wr_ref[...]) + br_ref[...] for h in h2]

    lane = lax.broadcasted_iota(jnp.int32, (tp, N_EXPERTS), 1).astype(F32)
    sels, idxs, gates, chosen = [], [], [], []
    for p in range(n_part):
        work = logits[p]
        sel_p, idx_p, val_p = [], [], []
        for _ in range(TOP_K):
            mx = jnp.max(work, axis=1, keepdims=True)
            idx = jnp.min(jnp.where(work == mx, lane, float(N_EXPERTS)), axis=1, keepdims=True)
            sel = lane == idx
            work = jnp.where(sel, -jnp.inf, work)
            sel_p.append(sel)
            idx_p.append(idx)
            val_p.append(mx)
        ex = [jnp.exp(v - val_p[0]) for v in val_p]
        inv_den = 1.0 / (ex[0] + ex[1] + ex[2] + ex[3])
        sels.append(sel_p)
        idxs.append(idx_p)
        gates.append([e * inv_den for e in ex])
        chosen.append(sum(s.astype(F32) for s in sel_p))

    chosen_all = jnp.concatenate(chosen, axis=0) if n_part > 1 else chosen[0]
    before = (lax.broadcasted_iota(jnp.int32, (tm, tm), 0) > lax.broadcasted_iota(jnp.int32, (tm, tm), 1))
    prior = _dot(before.astype(BF16), chosen_all.astype(BF16)) + run_ref[...]
    run_ref[...] = run_ref[...] + jnp.sum(chosen_all, axis=0, keepdims=True)
    cnt_ref[...] = run_ref[...]

    rlane = lax.broadcasted_iota(jnp.int32, (tp, LANES), 1)
    for p, r in enumerate(parts):
        rt = jnp.zeros((tp, LANES), F32)
        for kk in range(TOP_K):
            rank = jnp.sum(jnp.where(sels[p][kk], prior[r, :], 0.0), axis=1, keepdims=True)
            rt = jnp.where(rlane == RT_E + kk, idxs[p][kk], rt)
            rt = jnp.where(rlane == RT_RANK + kk, rank, rt)
            rt = jnp.where(rlane == RT_GATE + kk, gates[p][kk], rt)
        rt_ref[r, :] = rt


def _merge(oa, ob, gt, x2d, m3d, tiles_per_mod, g_post, g_pre, wa, wb, wo, wr, br, cnt0, tm):
    R = x2d.shape[0]
    r = m3d.shape[1]
    mod = lambda col: pl.BlockSpec((None, r, D), lambda i: (i // tiles_per_mod, 0, col))
    row = lambda w: pl.BlockSpec((tm, w), lambda i: (i, 0))
    full = lambda a, b: pl.BlockSpec((a, b), lambda i: (0, 0))
    return pl.pallas_call(
        functools.partial(_merge_kernel, n_part=4 if tm >= 512 else 1),
        grid=(R // tm,),
        in_specs=[row(W_MIX), row(W_MIX), row(2 * D), row(D), mod(2), mod(4), mod(3),
                  full(1, D), full(1, D), full(W_MIX, D), full(W_MIX, D), full(D, D),
                  full(D, N_EXPERTS), full(1, N_EXPERTS), full(1, N_EXPERTS)],
        out_specs=[row(D), pl.BlockSpec((tm * SUB, LANES), lambda i: (i, 0)), row(LANES), full(1, N_EXPERTS)],
        out_shape=[jax.ShapeDtypeStruct((R, D), F32),
                   jax.ShapeDtypeStruct((R * SUB, LANES), F32),
                   jax.ShapeDtypeStruct((R, LANES), F32),
                   jax.ShapeDtypeStruct((1, N_EXPERTS), F32)],
        scratch_shapes=[pltpu.VMEM((1, N_EXPERTS), F32)],
        compiler_params=_cp(("arbitrary",)),
        name="merge",
    )(oa, ob, gt, x2d, m3d, m3d, m3d, g_post.reshape(1, D), g_pre.reshape(1, D), wa, wb, wo, wr,
      br.reshape(1, N_EXPERTS), cnt0)


def _row_copy(src, s_row, dst, d_row, sem):
    return pltpu.make_async_copy(src.at[pl.ds(pl.multiple_of(s_row, SUB), SUB)],
                                 dst.at[pl.ds(pl.multiple_of(d_row, SUB), SUB)], sem)


def _dispatch_kernel(lo_ref, hi_ref, nu_ref, da_ref, db_ref, ha_ref, hb_ref, x_hbm, zero_ref, sem):
    i = pl.program_id(0)
    last = pl.num_programs(0) - 1
    unroll = 8

    def scatter_rows(d_ref, h_ref):
        n = h_ref.shape[0] // SUB

        def issue(j, carry):
            for u in range(unroll):
                t = j * unroll + u
                for kk in range(TOP_K):
                    _row_copy(h_ref, t * SUB, x_hbm, d_ref[t * TOP_K + kk], sem).start(priority=kk % 2)
            return carry

        lax.fori_loop(0, n // unroll, issue, 0)
        for _ in range(TOP_K):
            pltpu.make_async_copy(h_ref, x_hbm.at[pl.ds(0, n * SUB)], sem).wait()

    @pl.when(i < last)
    def _():
        scatter_rows(da_ref, ha_ref)

    @pl.when(i == last)
    def _():
        scatter_rows(db_ref, hb_ref)
        zero_ref[...] = jnp.zeros_like(zero_ref)

        def per_expert(e, carry):
            zrow = lambda r: _row_copy(zero_ref, 0, x_hbm, r * SUB, sem)
            lax.fori_loop(lo_ref[e], hi_ref[e], lambda r, c: (zrow(r).start(), c)[1], 0)
            lax.fori_loop(lo_ref[e], hi_ref[e], lambda r, c: (zrow(r).wait(), c)[1], 0)
            return carry

        lax.fori_loop(0, N_EXPERTS, per_expert, 0)

        blk = zero_ref.shape[0]
        zblk = lambda b: pltpu.make_async_copy(zero_ref, x_hbm.at[pl.ds(pl.multiple_of(b * blk, blk), blk)], sem)
        lax.fori_loop(nu_ref[0], x_hbm.shape[0] // blk, lambda b, c: (zblk(b).start(), c)[1], 0)
        lax.fori_loop(nu_ref[0], x_hbm.shape[0] // blk, lambda b, c: (zblk(b).wait(), c)[1], 0)


def _dispatch(h3_a, h3_b, dest8_a, dest8_b, pad_lo, pad_hi, n_used, n_rows):
    tm = DISPATCH_ROWS
    n_a = h3_a.shape[0] // (tm * SUB)
    clamp = lambda i: jnp.minimum(i, n_a - 1)
    grid_spec = pltpu.PrefetchScalarGridSpec(
        num_scalar_prefetch=3,
        grid=(n_a + 1,),
        in_specs=[pl.BlockSpec((tm * TOP_K,), lambda i, *_: (clamp(i),), memory_space=pltpu.SMEM),
                  pl.BlockSpec(dest8_b.shape, lambda i, *_: (0,), memory_space=pltpu.SMEM),
                  pl.BlockSpec((tm * SUB, LANES), lambda i, *_: (clamp(i), 0)),
                  pl.BlockSpec(h3_b.shape, lambda i, *_: (0, 0))],
        out_specs=pl.BlockSpec(memory_space=pl.ANY),
        scratch_shapes=[pltpu.VMEM((MOE_BLOCK * SUB, LANES), F32), pltpu.SemaphoreType.DMA],
    )
    return pl.pallas_call(
        _dispatch_kernel,
        grid_spec=grid_spec,
        out_shape=jax.ShapeDtypeStruct((n_rows * SUB, LANES), F32),
        compiler_params=_cp(("arbitrary",)),
        name="dispatch",
    )(pad_lo, pad_hi, n_used, dest8_a, dest8_b, h3_a, h3_b)


def _expert_kernel(b0_ref, nb_ref, nu_ref, x_hbm, wgu_ref, bgu_ref, wd_ref, bd_ref, y_hbm,
                   xbuf, ybuf, wgu_b, wd_b, xsem, ysem):
    e = pl.program_id(0)
    rows = xbuf.shape[1]
    b0, nb = b0_ref[e], nb_ref[e]
    x_copy = lambda j, slot: pltpu.make_async_copy(
        x_hbm.at[pl.ds(pl.multiple_of((b0 + j) * rows, rows), rows)], xbuf.at[slot], xsem.at[slot])
    y_copy = lambda j, slot: pltpu.make_async_copy(
        ybuf.at[slot], y_hbm.at[pl.ds(pl.multiple_of((b0 + j) * rows, rows), rows)], ysem.at[slot])

    @pl.when(nb > 0)
    def _():
        x_copy(0, 0).start(priority=1)
        wgu_b[...] = wgu_ref[...].astype(BF16)
        wd_b[...] = wd_ref[...].astype(BF16)

        def block(j, carry):
            slot = j % 2
            x_copy(j, slot).wait()

            @pl.when(j + 1 < nb)
            def _():
                x_copy(j + 1, 1 - slot).start(priority=1)

            @pl.when(j >= 2)
            def _():
                y_copy(j - 2, slot).wait()

            gu = _dot(_load_row_tiles(xbuf.at[slot]).astype(BF16), wgu_b[...]) + bgu_ref[...]
            glu = jnp.minimum(gu[:, :D], SWIGLU_LIMIT)
            lin = jnp.clip(gu[:, D:], -SWIGLU_LIMIT, SWIGLU_LIMIT)
            act = (lin + 1.0) * glu * _sigmoid(SWIGLU_ALPHA * glu)
            _store_row_tiles(ybuf.at[slot], _dot(act.astype(BF16), wd_b[...]) + bd_ref[...])
            y_copy(j, slot).start(priority=1)
            return carry

        lax.fori_loop(0, nb, block, 0)

        @pl.when(nb >= 2)
        def _():
            y_copy(nb - 2, nb % 2).wait()
        y_copy(nb - 1, (nb - 1) % 2).wait()

    @pl.when(e == pl.num_programs(0) - 1)
    def _():
        ybuf[0] = jnp.zeros(ybuf.shape[1:], F32)
        zblk = lambda b: pltpu.make_async_copy(
            ybuf.at[0], y_hbm.at[pl.ds(pl.multiple_of(b * rows, rows), rows)], ysem.at[0])
        n_blocks = y_hbm.shape[0] // rows
        lax.fori_loop(nu_ref[0], n_blocks, lambda b, c: (zblk(b).start(), c)[1], 0)
        lax.fori_loop(nu_ref[0], n_blocks, lambda b, c: (zblk(b).wait(), c)[1], 0)


def _experts(x3, blk_start, blk_count, n_used, w_gate_up, b_gate_up, w_down, b_down):
    rows = MOE_BLOCK * SUB
    wspec = lambda a, b: pl.BlockSpec((None, a, b), lambda e, *_: (e, 0, 0))
    grid_spec = pltpu.PrefetchScalarGridSpec(
        num_scalar_prefetch=3,
        grid=(N_EXPERTS,),
        in_specs=[pl.BlockSpec(memory_space=pl.ANY), wspec(D, 2 * D), wspec(1, 2 * D), wspec(D, D), wspec(1, D)],
        out_specs=pl.BlockSpec(memory_space=pl.ANY),
        scratch_shapes=[pltpu.VMEM((2, rows, LANES), F32), pltpu.VMEM((2, rows, LANES), F32),
                        pltpu.VMEM((D, 2 * D), BF16), pltpu.VMEM((D, D), BF16),
                        pltpu.SemaphoreType.DMA((2,)), pltpu.SemaphoreType.DMA((2,))],
    )
    return pl.pallas_call(
        _expert_kernel,
        grid_spec=grid_spec,
        out_shape=jax.ShapeDtypeStruct(x3.shape, F32),
        compiler_params=_cp(("arbitrary",)),
        name="experts",
    )(blk_start, blk_count, n_used, x3, w_gate_up, b_gate_up.reshape(N_EXPERTS, 1, 2 * D), w_down,
      b_down.reshape(N_EXPERTS, 1, D))


def _final_kernel(dcur_ref, dnext_ref, x1_ref, g2_ref, gpost_ref, rt_ref, y_hbm, o_ref, ybuf, sem):
    i = pl.program_id(0)
    n = pl.num_programs(0)
    tm = x1_ref.shape[0]
    slot = i % 2
    unroll = 8

    def issue(d_ref, s):
        def body(j, carry):
            for u in range(unroll):
                t = j * unroll + u
                for kk in range(TOP_K):
                    _row_copy(y_hbm, d_ref[t * TOP_K + kk], ybuf.at[s, kk], t * SUB, sem.at[s]).start(priority=kk % 2)
            return carry
        lax.fori_loop(0, tm // unroll, body, 0)

    @pl.when(i == 0)
    def _():
        issue(dcur_ref, 0)

    @pl.when(i + 1 < n)
    def _():
        issue(dnext_ref, 1 - slot)

    for kk in range(TOP_K):
        pltpu.make_async_copy(y_hbm.at[pl.ds(0, tm * SUB)], ybuf.at[slot, kk], sem.at[slot]).wait()

    y = jnp.zeros((tm, D), F32)
    for kk in range(TOP_K):
        y = y + rt_ref[:, RT_GATE + kk:RT_GATE + kk + 1] * _load_row_tiles(ybuf.at[slot, kk])
    o_ref[...] = x1_ref[...] + g2_ref[...] * _rms(y, gpost_ref[...])


def _final(x1, m3d, tiles_per_mod, g_post, route, dest_flat, y3, row0, tm):
    R = x1.shape[0]
    r = m3d.shape[1]
    off = row0 // tm
    nsteps = R // tm
    row = pl.BlockSpec((tm, D), lambda i: (i, 0))
    dspec = lambda nxt: pl.BlockSpec((tm * TOP_K,), lambda i: (jnp.minimum(i + nxt, nsteps - 1) + off,),
                                     memory_space=pltpu.SMEM)
    return pl.pallas_call(
        _final_kernel,
        grid=(nsteps,),
        in_specs=[dspec(0), dspec(1), row,
                  pl.BlockSpec((None, r, D), lambda i: (i // tiles_per_mod, 0, 5)),
                  pl.BlockSpec((1, D), lambda i: (0, 0)),
                  pl.BlockSpec((tm, LANES), lambda i: (i, 0)),
                  pl.BlockSpec(memory_space=pl.ANY)],
        out_specs=row,
        out_shape=jax.ShapeDtypeStruct((R, D), F32),
        scratch_shapes=[pltpu.VMEM((2, TOP_K, tm * SUB, LANES), F32), pltpu.SemaphoreType.DMA((2,))],
        compiler_params=_cp(("arbitrary",)),
        name="final",
    )(dest_flat, dest_flat, x1, m3d, g_post.reshape(1, D), route, y3)


def kernel(x_prompt, x_sample, cache_k, cache_v, state_hgrn, page_table, c_prompt, c_sample, w_mod, b_mod, g_pre_mix, g_post_mix, g_pre_ffn, g_post_ffn, w_in, hgrn_lb, g_hgrn, lam_q1, lam_k1, lam_q2, lam_k2, g_subln, w_proj_a, w_proj_b, w_out, w_router, b_router, w_gate_up, b_gate_up, w_down, b_down):
    B, L, _ = x_prompt.shape
    NS = x_sample.shape[0]
    T = B * L
    T_all = T + NS
    l = 0

    lb = jnp.cumsum(jax.nn.softmax(hgrn_lb.astype(F32), axis=0), axis=0)[l]
    lam = (jnp.exp(jnp.sum(lam_q1[l] * lam_k1[l])) - jnp.exp(jnp.sum(lam_q2[l] * lam_k2[l])) + LAM_INIT).astype(F32)
    slopes = jnp.exp2(-8.0 * jnp.arange(1, H + 1, dtype=F32) / H)

    m_all = _modulation(jnp.concatenate([c_prompt, c_sample], axis=0), w_mod[l], b_mod[l])
    m_p = m_all[:B].reshape(B, 1, 6 * D)
    m_s = m_all[B:].reshape(1, NS, 6 * D)

    w_in_b = w_in[l].astype(BF16)
    tm = 256
    tpm = L // tm
    xp = x_prompt.reshape(T, D)
    xs = x_sample.reshape(NS, D)
    merge_w = (g_post_mix[l], g_pre_ffn[l], w_proj_a[l].astype(BF16), w_proj_b[l].astype(BF16),
               w_out[l].astype(BF16), w_router[l], b_router[l])

    hg_p, bq_p, k_p, v_p, gt_p, k4_p, v4_p = _inproj(xp, m_p, tpm, g_pre_mix[l], w_in_b, tm)
    oa_p, s_p = _hgrn_prompt(hg_p.reshape(B, L, 4 * W_MIX), lb, g_hgrn[l])
    ob_p = _attn_prompt(bq_p, k_p, v_p, lam, slopes, g_subln[l], B, L)
    x1_p, h3_p, rt_p, cnt = _merge(oa_p.reshape(T, W_MIX), ob_p, gt_p, xp, m_p, L // MERGE_ROWS, *merge_w,
                                   jnp.zeros((1, N_EXPERTS), F32), MERGE_ROWS)

    hg_s, bq_s, k_s, v_s, gt_s, k4_s, v4_s = _inproj(xs, m_s, 1, g_pre_mix[l], w_in_b, NS)
    oa_s, s_s = _hgrn_step(hg_s, state_hgrn[l], lb, g_hgrn[l])
    ob_s = _attn_step(bq_s, k_s, v_s, cache_k[l], cache_v[l], page_table, lam, slopes, g_subln[l])
    x1_s, h3_s, rt_s, cnt = _merge(oa_s, ob_s, gt_s, xs, m_s, 1, *merge_w, cnt, NS)

    bm = MOE_BLOCK
    counts = cnt.reshape(N_EXPERTS).astype(jnp.int32)
    padded = (counts + bm - 1) // bm * bm
    pad_end = jnp.cumsum(padded)
    pad_start = pad_end - padded
    n_blocks = -(-T_all * TOP_K // bm) + N_EXPERTS
    route = jnp.concatenate([rt_p[:, :RT_GATE], rt_s[:, :RT_GATE]], axis=0)
    e_idx = route[:, RT_E:RT_E + TOP_K].astype(jnp.int32)
    rank = route[:, RT_RANK:RT_RANK + TOP_K].astype(jnp.int32)
    e_iota = jnp.arange(N_EXPERTS, dtype=jnp.int32)
    dest = rank + jnp.sum(jnp.where(e_idx[..., None] == e_iota, pad_start, 0), axis=-1)
    dest_flat = dest.reshape(-1) * SUB
    n_used = pad_end[-1:] // bm

    x3 = _dispatch(h3_p, h3_s, dest_flat[:T * TOP_K], dest_flat[T * TOP_K:], pad_start + counts, pad_end, n_used,
                   n_blocks * bm)
    y3 = _experts(x3, pad_start // bm, padded // bm, n_used, w_gate_up[l], b_gate_up[l], w_down[l], b_down[l])
    y_p = _final(x1_p, m_p, tpm, g_post_ffn[l], rt_p, dest_flat, y3, 0, tm)
    y_s = _final(x1_s, m_s, 1, g_post_ffn[l], rt_s, dest_flat, y3, T, NS)

    return (y_p.reshape(B, L, D), y_s.reshape(NS, 1, D),
            k4_p.reshape(1, B, L, H, DH), v4_p.reshape(1, B, L, H, DH), s_p.reshape(1, B, H, DH, DH),
            k4_s.reshape(1, NS, 1, H, DH), v4_s.reshape(1, NS, 1, H, DH), s_s.reshape(1, NS, H, DH, DH))
```

```python
import functools
import math

import jax
import jax.numpy as jnp
from jax import lax
from jax.experimental import pallas as pl
from jax.experimental.pallas import tpu as pltpu

F32 = jnp.float32
BF16 = jnp.bfloat16

D = 1024
LANES = 128
SUB = D // LANES
H = 4
DH = 128
DK = 64
W_MIX = H * DH
HGRN_CHUNK = 64
N_EXPERTS = 32
TOP_K = 4
SWIGLU_LIMIT = 7.0
SWIGLU_ALPHA = 1.702
EPS = 1e-6
NEG_INF = -1e30
LOG2E = 1.4426950408889634
LAM_INIT = 0.8 - 0.6 * math.exp(-0.3 * 0)
C_HG = (0, 4 * W_MIX)
C_BQ = (4 * W_MIX, 5 * W_MIX)
C_BK = (5 * W_MIX, 6 * W_MIX)
C_BV = (6 * W_MIX, 7 * W_MIX)
C_GT = (7 * W_MIX, 7 * W_MIX + 2 * D)
D_IN = 7 * W_MIX + 2 * D
RT_E, RT_RANK, RT_GATE = 0, TOP_K, 2 * TOP_K

VMEM_LIMIT = 56 * 1024 * 1024
MOE_BLOCK = 256
DISPATCH_ROWS = 512
MERGE_ROWS = 512
INPROJ_ROWS = 512
ATTN_STEP_TOKENS = 2
ATT_G = 256
ATT_NG = 8
ATT_TK = 256
ATT_VR = DH + 16


def _cp(sem, vmem=VMEM_LIMIT):
    return pltpu.CompilerParams(dimension_semantics=sem, vmem_limit_bytes=vmem)


def _sigmoid(x):
    return 1.0 / (1.0 + jnp.exp(-x))


def _rms(x, g):
    return x * lax.rsqrt(jnp.mean(x * x, axis=-1, keepdims=True) + EPS) * g


def _dot(a, b):
    return jnp.dot(a, b, preferred_element_type=F32)


def _dot_nt(a, b):
    return lax.dot_general(a, b, (((1,), (1,)), ((), ())), preferred_element_type=F32)


def _dot_tn(a, b):
    return lax.dot_general(a, b, (((0,), (0,)), ((), ())), preferred_element_type=F32)


def _store_row_tiles(ref2, x2d, row0=0):
    n = x2d.shape[0]
    for j in range(SUB):
        ref2[pl.ds(row0 * SUB + j, n, stride=SUB), :] = x2d[:, j * LANES:(j + 1) * LANES]


def _load_row_tiles(ref2):
    n = ref2.shape[0] // SUB
    return jnp.concatenate([ref2[pl.ds(j, n, stride=SUB), :] for j in range(SUB)], axis=1)


def _mod_kernel(c_ref, w_ref, b_ref, o_ref):
    c = c_ref[...]
    o_ref[...] = _dot(c * _sigmoid(c), w_ref[...]) + b_ref[...]


def _modulation(c_all, w_mod, b_mod):
    n = c_all.shape[0]
    tn = 1024
    return pl.pallas_call(
        _mod_kernel,
        grid=(6 * D // tn,),
        in_specs=[pl.BlockSpec((n, D), lambda j: (0, 0)),
                  pl.BlockSpec((D, tn), lambda j: (0, j)),
                  pl.BlockSpec((1, tn), lambda j: (0, j))],
        out_specs=pl.BlockSpec((n, tn), lambda j: (0, j)),
        out_shape=jax.ShapeDtypeStruct((n, 6 * D), F32),
        compiler_params=_cp(("arbitrary",)),
        name="modulation",
    )(c_all, w_mod, b_mod.reshape(1, 6 * D))


def _inproj_kernel(x_ref, sc_ref, sh_ref, g_ref, w_ref,
                   aq_ref, af_ref, ai_ref, ag_ref, bq_ref, k_ref, v_ref, gt_ref, k4_ref, v4_ref):
    h = _rms(x_ref[...], g_ref[...]) * (1.0 + sc_ref[...]) + sh_ref[...]
    hb = h.astype(BF16)
    mm = lambda lo, hi: _dot(hb, w_ref[:, lo:hi])
    for j, o_ref in enumerate((aq_ref, af_ref, ai_ref, ag_ref)):
        o_ref[...] = mm(C_HG[0] + j * W_MIX, C_HG[0] + (j + 1) * W_MIX).astype(o_ref.dtype)
    bq_ref[...] = mm(*C_BQ).astype(BF16)
    gt_ref[...] = mm(*C_GT).astype(BF16)
    for c, o_ref, o4_ref in ((C_BK, k_ref, k4_ref), (C_BV, v_ref, v4_ref)):
        r = mm(*c)
        o_ref[...] = r.astype(BF16)
        o4 = o4_ref.reshape(r.shape[0] * H, DH)
        for hh in range(H):
            o4[pl.ds(hh, r.shape[0], stride=H), :] = r[:, hh * DH:(hh + 1) * DH]


def _inproj(x2d, m3d, tiles_per_mod, g_pre, w_in_bf16, tm):
    R = x2d.shape[0]
    r = m3d.shape[1]
    mod = lambda col: pl.BlockSpec((None, r, D), lambda i: (i // tiles_per_mod, 0, col))
    row = lambda w: pl.BlockSpec((tm, w), lambda i: (i, 0))
    row4 = pl.BlockSpec((tm, H, DH), lambda i: (i, 0, 0))
    outs = [(W_MIX, BF16), (W_MIX, F32), (W_MIX, BF16), (W_MIX, BF16), (W_MIX, BF16), (W_MIX, BF16), (W_MIX, BF16),
            (2 * D, BF16)]
    kv4 = jax.ShapeDtypeStruct((R, H, DH), F32)
    return pl.pallas_call(
        _inproj_kernel,
        grid=(R // tm,),
        in_specs=[row(D), mod(1), mod(0),
                  pl.BlockSpec((1, D), lambda i: (0, 0)),
                  pl.BlockSpec((D, D_IN), lambda i: (0, 0))],
        out_specs=[row(w) for w, _ in outs] + [row4, row4],
        out_shape=[jax.ShapeDtypeStruct((R, w), dt) for w, dt in outs] + [kv4, kv4],
        compiler_params=_cp(("arbitrary",)),
        name="inproj",
    )(x2d, m3d, m3d, g_pre.reshape(1, D), w_in_bf16)


def _hgrn_prompt_kernel(lb_ref, gn_ref, q_ref, f_ref, i_ref, g_ref, o_ref, sout_ref, st_ref):
    c = pl.program_id(0)
    nb = q_ref.shape[0]
    C = HGRN_CHUNK

    @pl.when(c == 0)
    def _():
        st_ref[...] = jnp.zeros_like(st_ref)

    row = lax.broadcasted_iota(jnp.int32, (C, DH), 0)
    causal = lax.broadcasted_iota(jnp.int32, (C, C), 0) >= lax.broadcasted_iota(jnp.int32, (C, C), 1)

    def per_batch_pair(bp, carry):
        ch = [(2 * bp + d, h, slice(h * DH, (h + 1) * DH)) for d in range(2) for h in range(H)]
        qd, kd, k2, v, eg = [], [], [], [], []
        for b, h, hs in ch:
            lb = lb_ref[:, hs]
            f = lb + (1.0 - lb) * _sigmoid(f_ref[b, :, hs].astype(F32))
            k = 1.0 - f
            gc = jnp.log(f)
            for s in (1, 2, 4, 8, 16, 32):
                gc = gc + jnp.where(row >= s, pltpu.roll(gc, s, axis=0), 0.0)
            g_last = gc[C - 1:C, :]
            qd.append(q_ref[b, :, hs].astype(F32) * jnp.exp(gc))
            kd.append(k * jnp.exp(-gc))
            k2.append(k * jnp.exp(g_last - gc))
            eg.append(jnp.exp(g_last))
            v.append(i_ref[b, :, hs].astype(F32))
        n = len(ch)
        st = [st_ref[b, h] for b, h, _ in ch]
        a = [jnp.where(causal, _dot_nt(qd[i], kd[i]), 0.0) for i in range(n)]
        o_inter = [_dot_nt(qd[i], st[i]) for i in range(n)]
        kv_t = [_dot_tn(v[i], k2[i]) for i in range(n)]
        o_intra = [_dot(a[i], v[i]) for i in range(n)]
        for i, (b, h, hs) in enumerate(ch):
            st_ref[b, h] = eg[i] * st[i] + kv_t[i]
            o = _rms(o_intra[i] + o_inter[i], gn_ref[:, hs])
            ag = g_ref[b, :, hs].astype(F32)
            o_ref[b, :, hs] = (o * (ag * _sigmoid(ag))).astype(o_ref.dtype)
        return carry

    lax.fori_loop(0, nb // 2, per_batch_pair, 0)

    @pl.when(c == pl.num_programs(0) - 1)
    def _():
        def fin(b, carry):
            for h in range(H):
                sout_ref[b, h] = st_ref[b, h].T
            return carry
        lax.fori_loop(0, nb, fin, 0)


def _hgrn_prompt(aq, af, ai, ag, lb, g_hgrn):
    B, L, _ = aq.shape
    C = HGRN_CHUNK
    blk = pl.BlockSpec((B, C, W_MIX), lambda c: (0, c, 0))
    vec = pl.BlockSpec((1, W_MIX), lambda c: (0, 0))
    return pl.pallas_call(
        _hgrn_prompt_kernel,
        grid=(L // C,),
        in_specs=[vec, vec, blk, blk, blk, blk],
        out_specs=[blk, pl.BlockSpec((B, H, DH, DH), lambda c: (0, 0, 0, 0))],
        out_shape=[jax.ShapeDtypeStruct((B, L, W_MIX), BF16),
                   jax.ShapeDtypeStruct((B, H, DH, DH), F32)],
        scratch_shapes=[pltpu.VMEM((B, H, DH, DH), F32)],
        compiler_params=_cp(("arbitrary",)),
        name="hgrn_prompt",
    )(lb.reshape(1, W_MIX), g_hgrn.reshape(1, W_MIX), aq, af, ai, ag)


def _hgrn_step_kernel(lb_ref, gn_ref, x_ref, s_ref, o_ref, sout_ref):
    nt = x_ref.shape[0]
    ones = jnp.ones((8, DH), F32)
    pad = jnp.zeros((6, DH), F32)

    def col(r):
        hi = r.astype(BF16).astype(F32)
        return _dot_tn(jnp.concatenate([hi, r - hi, pad], axis=0), ones)

    for t in range(nt):
        for h in range(H):
            seg = lambda j: x_ref[t:t + 1, j * W_MIX + h * DH: j * W_MIX + (h + 1) * DH]
            lb = lb_ref[:, h * DH:(h + 1) * DH]
            q, v, ag = seg(0), seg(2), seg(3)
            f = lb + (1.0 - lb) * _sigmoid(seg(1))
            f_col = col(f)
            s_new = f_col * s_ref[t, h] + (1.0 - f_col) * v
            sout_ref[t, h] = s_new
            o = jnp.sum(col(q) * s_new, axis=0, keepdims=True)
            o = _rms(o, gn_ref[:, h * DH:(h + 1) * DH])
            o_ref[t:t + 1, h * DH:(h + 1) * DH] = o * (ag * _sigmoid(ag))


def _hgrn_step(hg, state, lb, g_hgrn):
    N = hg.shape[0]
    nt = 8
    vec = pl.BlockSpec((1, W_MIX), lambda i: (0, 0))
    st = pl.BlockSpec((nt, H, DH, DH), lambda i: (i, 0, 0, 0))
    return pl.pallas_call(
        _hgrn_step_kernel,
        grid=(N // nt,),
        in_specs=[vec, vec, pl.BlockSpec((nt, 4 * W_MIX), lambda i: (i, 0)), st],
        out_specs=[pl.BlockSpec((nt, W_MIX), lambda i: (i, 0)), st],
        out_shape=[jax.ShapeDtypeStruct((N, W_MIX), F32),
                   jax.ShapeDtypeStruct((N, H, DH, DH), F32)],
        compiler_params=_cp(("arbitrary",)),
        name="hgrn_step",
    )(lb.reshape(1, W_MIX), g_hgrn.reshape(1, W_MIX), hg, state)


def _attn_prompt_kernel(lam_ref, sl_ref, q_ref, k_ref, v_ref, g_ref, o_ref, qt_ref, vt_ref, *chain_refs):
    G, NG, tk = ATT_G, ATT_NG, ATT_TK
    tq = G * NG
    m_refs = [chain_refs[mp * NG:(mp + 1) * NG] for mp in range(2)]
    acc_refs = [chain_refs[(2 + mp) * NG:(3 + mp) * NG] for mp in range(2)]
    slope = sl_ref[pl.program_id(1)] * LOG2E
    lam = lam_ref[0]
    L = q_ref.shape[0]
    q_t = q_ref[...].astype(F32).T * (DK ** -0.5 * LOG2E)
    sub = lax.broadcasted_iota(jnp.int32, (DH, L), 0)
    for mp in range(2):
        qm_t = jnp.where(sub < DK, q_t, 0.0) if mp == 0 else jnp.where(sub >= DK, q_t, 0.0)
        for t in range(L // G):
            qt_ref[mp, t] = qm_t[:, t * G:(t + 1) * G].astype(BF16)
    v_t = v_ref[...].astype(F32).T
    for kb in range(L // tk):
        vt_ref[kb, :DH, :] = v_t[:, kb * tk:(kb + 1) * tk].astype(BF16)
        vt_ref[kb, DH:, :] = jnp.ones((ATT_VR - DH, tk), BF16)
    idiff = lax.broadcasted_iota(jnp.int32, (tk, G), 1) - lax.broadcasted_iota(jnp.int32, (tk, G), 0)
    dmat = idiff.astype(F32) * slope

    def fold(qi, kb, chains):
        k0 = pl.multiple_of(kb * tk, tk)
        kt = k_ref[pl.ds(k0, tk), :]
        scores = [_dot(kt, qt_ref[mp, qi * NG + g]) for mp, g, _ in chains]
        probs = []
        for (mp, g, thresh), s in zip(chains, scores):
            s = s - dmat
            if thresh is not None:
                s = jnp.where(idiff >= thresh, s, NEG_INF)
            c = slope * ((qi * NG + g) * G - k0).astype(F32)
            m = m_refs[mp][g][...]
            m_new = jnp.maximum(m, jnp.max(s, axis=0, keepdims=True) - c)
            m_refs[mp][g][...] = m_new
            probs.append((jnp.exp2(s - (m_new + c)), jnp.exp2(m - m_new)))
        va = vt_ref[kb]
        pvs = [_dot(va, p.astype(BF16)) for p, _ in probs]
        for (mp, g, _), (_, alpha), pv in zip(chains, probs, pvs):
            acc_refs[mp][g][...] = alpha * acc_refs[mp][g][...] + pv

    def q_tile(qi, carry):
        for mp in range(2):
            for g in range(NG):
                m_refs[mp][g][...] = jnp.full((1, G), NEG_INF, F32)
                acc_refs[mp][g][...] = jnp.zeros((ATT_VR, G), F32)

        every = [(mp, g, None) for mp in range(2) for g in range(NG)]
        n_full = qi * (tq // tk)
        lax.fori_loop(0, n_full, lambda kb, c: (fold(qi, kb, every), c)[1], 0)
        for j in range(tq // tk):
            chains = []
            for mp in range(2):
                for g in range(NG):
                    thresh = j * tk - G * g
                    if thresh <= G - 1:
                        chains.append((mp, g, None if thresh <= -(tk - 1) else thresh))
            fold(qi, n_full + j, chains)


        for g in range(NG):
            a0, a1 = acc_refs[0][g][...], acc_refs[1][g][...]
            o_t = a0[:DH] / a0[DH:DH + 1] - lam * (a1[:DH] / a1[DH:DH + 1])
            o_t = o_t * lax.rsqrt(jnp.mean(o_t * o_t, axis=0, keepdims=True) + EPS)
            q0 = pl.multiple_of((qi * NG + g) * G, G)
            o_ref[pl.ds(q0, G), :] = (o_t.T * (g_ref[...] * (1.0 - LAM_INIT))).astype(o_ref.dtype)
        return carry

    lax.fori_loop(0, L // tq, q_tile, 0)


def _attn_prompt(bq, bk, bv, lam, slopes, g_subln, B, L):
    smem = pl.BlockSpec(memory_space=pltpu.SMEM)
    spec = pl.BlockSpec((L, DH), lambda b, h: (b, h))
    return pl.pallas_call(
        _attn_prompt_kernel,
        grid=(B, H),
        in_specs=[smem, smem, spec, spec, spec, pl.BlockSpec((1, DH), lambda b, h: (0, 0))],
        out_specs=spec,
        out_shape=jax.ShapeDtypeStruct((B * L, W_MIX), BF16),
        scratch_shapes=[pltpu.VMEM((2, L // ATT_G, DH, ATT_G), BF16),
                        pltpu.VMEM((L // ATT_TK, ATT_VR, ATT_TK), BF16)]
                       + [pltpu.VMEM((1, ATT_G), F32)] * (2 * ATT_NG)
                       + [pltpu.VMEM((ATT_VR, ATT_G), F32)] * (2 * ATT_NG),
        compiler_params=_cp(("arbitrary", "arbitrary")),
        name="attn_prompt",
    )(lam.reshape(1), slopes, bq, bk, bv, g_subln.reshape(1, DH))


def _attn_step_kernel(pt_ref, lam_ref, sl_ref, q_ref, kn_ref, vn_ref, g_ref, *refs, n_pages, page, n_tok):
    o_ref = refs[2 * n_tok * n_pages]
    PR = page * H
    past = n_pages * page
    lam = lam_ref[0]
    R = 2 * H
    grp = lax.broadcasted_iota(jnp.int32, (R, W_MIX), 1) // DK
    rid = lax.broadcasted_iota(jnp.int32, (R, W_MIX), 0)
    rhead = lax.broadcasted_iota(jnp.int32, (R, DH), 0) // 2
    slope = jnp.concatenate([jnp.full((2, 1), sl_ref[hh], F32) for hh in range(H)], axis=0)
    col = lax.broadcasted_iota(jnp.int32, (R, n_pages * PR), 1)
    own = (col % H) == (lax.broadcasted_iota(jnp.int32, (R, n_pages * PR), 0) // 2)
    bias = jnp.where(own, -slope * (past - col // H).astype(F32), NEG_INF)

    for t in range(n_tok):
        k_refs = refs[t * n_pages:(t + 1) * n_pages]
        v_refs = refs[(n_tok + t) * n_pages:(n_tok + t + 1) * n_pages]
        qbd = jnp.where(grp == rid, jnp.broadcast_to(q_ref[t], (R, W_MIX)), 0.0) * (DK ** -0.5)
        s_new = jnp.sum(qbd * kn_ref[t], axis=1, keepdims=True)
        qx = sum(qbd[:, hh * DH:(hh + 1) * DH] for hh in range(H))
        vnx = sum(jnp.where(rhead == hh, vn_ref[t, :, hh * DH:(hh + 1) * DH], 0.0) for hh in range(H))
        s = jnp.concatenate([_dot_nt(qx, k_refs[j][...]) for j in range(n_pages)], axis=1)
        s = s + bias
        m = jnp.maximum(jnp.max(s, axis=1, keepdims=True), s_new)
        p = jnp.exp(s - m)
        p_new = jnp.exp(s_new - m)
        inv_l = 1.0 / (jnp.sum(p, axis=1, keepdims=True) + p_new)
        p = p * inv_l
        o8 = (p_new * inv_l) * vnx
        for j in range(n_pages):
            o8 = o8 + _dot(p[:, j * PR:(j + 1) * PR], v_refs[j][...])
        for hh in range(H):
            o = o8[2 * hh:2 * hh + 1, :] - lam * o8[2 * hh + 1:2 * hh + 2, :]
            o_ref[t, :, hh * DH:(hh + 1) * DH] = _rms(o, g_ref[...]) * (1.0 - LAM_INIT)


def _attn_step(bq, k_new, v_new, cache_k, cache_v, page_table, lam, slopes, g_subln):
    N = bq.shape[0]
    n_pool, page = cache_k.shape[0], cache_k.shape[1]
    n_pages = page_table.shape[1]
    n_tok = ATTN_STEP_TOKENS
    cache_k = cache_k.reshape(n_pool * page * H, DH)
    cache_v = cache_v.reshape(n_pool * page * H, DH)
    smem = pl.BlockSpec(memory_space=pltpu.SMEM)
    rowspec = pl.BlockSpec((n_tok, 1, W_MIX), lambda b, pt: (b, 0, 0))
    pspec = lambda t, j: pl.BlockSpec((page * H, DH), lambda b, pt: (pt[(b * n_tok + t) * n_pages + j], 0))
    pages = [pspec(t, j) for t in range(n_tok) for j in range(n_pages)]
    grid_spec = pltpu.PrefetchScalarGridSpec(
        num_scalar_prefetch=1,
        grid=(N // n_tok,),
        in_specs=[smem, smem, rowspec, rowspec, rowspec, pl.BlockSpec((1, DH), lambda b, pt: (0, 0))] + pages * 2,
        out_specs=rowspec,
    )
    out = pl.pallas_call(
        functools.partial(_attn_step_kernel, n_pages=n_pages, page=page, n_tok=n_tok),
        grid_spec=grid_spec,
        out_shape=jax.ShapeDtypeStruct((N, 1, W_MIX), F32),
        compiler_params=_cp(("arbitrary",)),
        name="attn_step",
    )(page_table.reshape(-1), lam.reshape(1), slopes, bq.reshape(N, 1, W_MIX), k_new.reshape(N, 1, W_MIX),
      v_new.reshape(N, 1, W_MIX), g_subln.reshape(1, DH), *([cache_k] * (n_tok * n_pages)),
      *([cache_v] * (n_tok * n_pages)))
    return out.reshape(N, W_MIX)


def _merge_kernel(oa_ref, ob_ref, gt_ref, x_ref, g1_ref, sc2_ref, sh2_ref, gpost_ref, gpre_ref,
                  wa_ref, wb_ref, wo_ref, wr_ref, br_ref, cnt0_ref, x1_ref, h3_ref, rt_ref, cnt_ref, run_ref,
                  *, n_part):
    i = pl.program_id(0)
    tm = x_ref.shape[0]
    tp = tm // n_part
    parts = [slice(p * tp, (p + 1) * tp) for p in range(n_part)]
    rows = lambda ref, r: ref[...] if ref.shape[0] == 1 else ref[r, :]

    @pl.when(i == 0)
    def _():
        run_ref[...] = cnt0_ref[...]

    pa = [_dot(oa_ref[r, :], wa_ref[...]) for r in parts]
    pb = [_dot(ob_ref[r, :], wb_ref[...]) for r in parts]
    merged = [_sigmoid(gt_ref[r, :D].astype(F32)) * pa[p] + _sigmoid(gt_ref[r, D:].astype(F32)) * pb[p]
              for p, r in enumerate(parts)]
    z = [_dot(m.astype(BF16), wo_ref[...]) for m in merged]
    h2 = []
    for p, r in enumerate(parts):
        x1 = x_ref[r, :] + rows(g1_ref, r) * _rms(z[p], gpost_ref[...])
        x1_ref[r, :] = x1
        h2.append(_rms(x1, gpre_ref[...]) * (1.0 + rows(sc2_ref, r)) + rows(sh2_ref, r))
        _store_row_tiles(h3_ref, h2[p], p * tp)
    logits = [_dot(h, wr_ref[...]) + br_ref[...] for h in h2]

    lane = lax.broadcasted_iota(jnp.int32, (tp, N_EXPERTS), 1).astype(F32)
    sels, idxs, gates, chosen = [], [], [], []
    for p in range(n_part):
        work = logits[p]
        sel_p, idx_p, val_p = [], [], []
        for _ in range(TOP_K):
            mx = jnp.max(work, axis=1, keepdims=True)
            idx = jnp.min(jnp.where(work == mx, lane, float(N_EXPERTS)), axis=1, keepdims=True)
            sel = lane == idx
            work = jnp.where(sel, -jnp.inf, work)
            sel_p.append(sel)
            idx_p.append(idx)
            val_p.append(mx)
        ex = [jnp.exp(v - val_p[0]) for v in val_p]
        inv_den = 1.0 / (ex[0] + ex[1] + ex[2] + ex[3])
        sels.append(sel_p)
        idxs.append(idx_p)
        gates.append([e * inv_den for e in ex])
        chosen.append(sum(s.astype(F32) for s in sel_p))

    chosen_all = jnp.concatenate(chosen, axis=0) if n_part > 1 else chosen[0]
    before = (lax.broadcasted_iota(jnp.int32, (tm, tm), 0) > lax.broadcasted_iota(jnp.int32, (tm, tm), 1))
    prior = _dot(before.astype(BF16), chosen_all.astype(BF16)) + run_ref[...]
    run_ref[...] = run_ref[...] + jnp.sum(chosen_all, axis=0, keepdims=True)
    cnt_ref[...] = run_ref[...]

    rlane = lax.broadcasted_iota(jnp.int32, (tp, LANES), 1)
    for p, r in enumerate(parts):
        rt = jnp.zeros((tp, LANES), F32)
        for kk in range(TOP_K):
            rank = jnp.sum(jnp.where(sels[p][kk], prior[r, :], 0.0), axis=1, keepdims=True)
            rt = jnp.where(rlane == RT_E + kk, idxs[p][kk], rt)
            rt = jnp.where(rlane == RT_RANK + kk, rank, rt)
            rt = jnp.where(rlane == RT_GATE + kk, gates[p][kk], rt)
        rt_ref[r, :] = rt


def _merge(oa, ob, gt, x2d, m3d, tiles_per_mod, g_post, g_pre, wa, wb, wo, wr, br, cnt0, tm):
    R = x2d.shape[0]
    r = m3d.shape[1]
    mod = lambda col: pl.BlockSpec((None, r, D), lambda i: (i // tiles_per_mod, 0, col))
    row = lambda w: pl.BlockSpec((tm, w), lambda i: (i, 0))
    full = lambda a, b: pl.BlockSpec((a, b), lambda i: (0, 0))
    return pl.pallas_call(
        functools.partial(_merge_kernel, n_part=4 if tm >= 512 else 1),
        grid=(R // tm,),
        in_specs=[row(W_MIX), row(W_MIX), row(2 * D), row(D), mod(2), mod(4), mod(3),
                  full(1, D), full(1, D), full(W_MIX, D), full(W_MIX, D), full(D, D),
                  full(D, N_EXPERTS), full(1, N_EXPERTS), full(1, N_EXPERTS)],
        out_specs=[row(D), pl.BlockSpec((tm * SUB, LANES), lambda i: (i, 0)), row(LANES), full(1, N_EXPERTS)],
        out_shape=[jax.ShapeDtypeStruct((R, D), F32),
                   jax.ShapeDtypeStruct((R * SUB, LANES), F32),
                   jax.ShapeDtypeStruct((R, LANES), F32),
                   jax.ShapeDtypeStruct((1, N_EXPERTS), F32)],
        scratch_shapes=[pltpu.VMEM((1, N_EXPERTS), F32)],
        compiler_params=_cp(("arbitrary",)),
        name="merge",
    )(oa, ob, gt, x2d, m3d, m3d, m3d, g_post.reshape(1, D), g_pre.reshape(1, D), wa, wb, wo, wr,
      br.reshape(1, N_EXPERTS), cnt0)


def _row_copy(src, s_row, dst, d_row, sem):
    return pltpu.make_async_copy(src.at[pl.ds(pl.multiple_of(s_row, SUB), SUB)],
                                 dst.at[pl.ds(pl.multiple_of(d_row, SUB), SUB)], sem)


def _dispatch_kernel(lo_ref, hi_ref, nu_ref, da_ref, db_ref, ha_ref, hb_ref, x_hbm, zero_ref, sem):
    i = pl.program_id(0)
    last = pl.num_programs(0) - 1
    unroll = 8

    def scatter_rows(d_ref, h_ref):
        n = h_ref.shape[0] // SUB

        def issue(j, carry):
            for u in range(unroll):
                t = j * unroll + u
                for kk in range(TOP_K):
                    _row_copy(h_ref, t * SUB, x_hbm, d_ref[t * TOP_K + kk], sem).start(priority=kk % 2)
            return carry

        lax.fori_loop(0, n // unroll, issue, 0)
        for _ in range(TOP_K):
            pltpu.make_async_copy(h_ref, x_hbm.at[pl.ds(0, n * SUB)], sem).wait()

    @pl.when(i < last)
    def _():
        scatter_rows(da_ref, ha_ref)

    @pl.when(i == last)
    def _():
        scatter_rows(db_ref, hb_ref)
        zero_ref[...] = jnp.zeros_like(zero_ref)

        def per_expert(e, carry):
            zrow = lambda r: _row_copy(zero_ref, 0, x_hbm, r * SUB, sem)
            lax.fori_loop(lo_ref[e], hi_ref[e], lambda r, c: (zrow(r).start(), c)[1], 0)
            lax.fori_loop(lo_ref[e], hi_ref[e], lambda r, c: (zrow(r).wait(), c)[1], 0)
            return carry

        lax.fori_loop(0, N_EXPERTS, per_expert, 0)

        blk = zero_ref.shape[0]
        zblk = lambda b: pltpu.make_async_copy(zero_ref, x_hbm.at[pl.ds(pl.multiple_of(b * blk, blk), blk)], sem)
        lax.fori_loop(nu_ref[0], x_hbm.shape[0] // blk, lambda b, c: (zblk(b).start(), c)[1], 0)
        lax.fori_loop(nu_ref[0], x_hbm.shape[0] // blk, lambda b, c: (zblk(b).wait(), c)[1], 0)


def _dispatch(h3_a, h3_b, dest8_a, dest8_b, pad_lo, pad_hi, n_used, n_rows):
    tm = DISPATCH_ROWS
    n_a = h3_a.shape[0] // (tm * SUB)
    clamp = lambda i: jnp.minimum(i, n_a - 1)
    grid_spec = pltpu.PrefetchScalarGridSpec(
        num_scalar_prefetch=3,
        grid=(n_a + 1,),
        in_specs=[pl.BlockSpec((tm * TOP_K,), lambda i, *_: (clamp(i),), memory_space=pltpu.SMEM),
                  pl.BlockSpec(dest8_b.shape, lambda i, *_: (0,), memory_space=pltpu.SMEM),
                  pl.BlockSpec((tm * SUB, LANES), lambda i, *_: (clamp(i), 0)),
                  pl.BlockSpec(h3_b.shape, lambda i, *_: (0, 0))],
        out_specs=pl.BlockSpec(memory_space=pl.ANY),
        scratch_shapes=[pltpu.VMEM((MOE_BLOCK * SUB, LANES), F32), pltpu.SemaphoreType.DMA],
    )
    return pl.pallas_call(
        _dispatch_kernel,
        grid_spec=grid_spec,
        out_shape=jax.ShapeDtypeStruct((n_rows * SUB, LANES), F32),
        compiler_params=_cp(("arbitrary",)),
        name="dispatch",
    )(pad_lo, pad_hi, n_used, dest8_a, dest8_b, h3_a, h3_b)


def _expert_kernel(b0_ref, nb_ref, nu_ref, x_hbm, wgu_ref, bgu_ref, wd_ref, bd_ref, y_hbm,
                   xbuf, ybuf, wgu_b, wd_b, xsem, ysem):
    e = pl.program_id(0)
    rows = xbuf.shape[1]
    b0, nb = b0_ref[e], nb_ref[e]
    x_copy = lambda j, slot: pltpu.make_async_copy(
        x_hbm.at[pl.ds(pl.multiple_of((b0 + j) * rows, rows), rows)], xbuf.at[slot], xsem.at[slot])
    y_copy = lambda j, slot: pltpu.make_async_copy(
        ybuf.at[slot], y_hbm.at[pl.ds(pl.multiple_of((b0 + j) * rows, rows), rows)], ysem.at[slot])

    @pl.when(nb > 0)
    def _():
        x_copy(0, 0).start(priority=1)
        wgu_b[...] = wgu_ref[...].astype(BF16)
        wd_b[...] = wd_ref[...].astype(BF16)

        def block(j, carry):
            slot = j % 2
            x_copy(j, slot).wait()

            @pl.when(j + 1 < nb)
            def _():
                x_copy(j + 1, 1 - slot).start(priority=1)

            @pl.when(j >= 2)
            def _():
                y_copy(j - 2, slot).wait()

            gu = _dot(_load_row_tiles(xbuf.at[slot]).astype(BF16), wgu_b[...]) + bgu_ref[...]
            glu = jnp.minimum(gu[:, :D], SWIGLU_LIMIT)
            lin = jnp.clip(gu[:, D:], -SWIGLU_LIMIT, SWIGLU_LIMIT)
            act = (lin + 1.0) * glu * _sigmoid(SWIGLU_ALPHA * glu)
            _store_row_tiles(ybuf.at[slot], _dot(act.astype(BF16), wd_b[...]) + bd_ref[...])
            y_copy(j, slot).start(priority=1)
            return carry

        lax.fori_loop(0, nb, block, 0)

        @pl.when(nb >= 2)
        def _():
            y_copy(nb - 2, nb % 2).wait()
        y_copy(nb - 1, (nb - 1) % 2).wait()

    @pl.when(e == pl.num_programs(0) - 1)
    def _():
        ybuf[0] = jnp.zeros(ybuf.shape[1:], F32)
        zblk = lambda b: pltpu.make_async_copy(
            ybuf.at[0], y_hbm.at[pl.ds(pl.multiple_of(b * rows, rows), rows)], ysem.at[0])
        n_blocks = y_hbm.shape[0] // rows
        lax.fori_loop(nu_ref[0], n_blocks, lambda b, c: (zblk(b).start(), c)[1], 0)
        lax.fori_loop(nu_ref[0], n_blocks, lambda b, c: (zblk(b).wait(), c)[1], 0)


def _experts(x3, blk_start, blk_count, n_used, w_gate_up, b_gate_up, w_down, b_down):
    rows = MOE_BLOCK * SUB
    wspec = lambda a, b: pl.BlockSpec((None, a, b), lambda e, *_: (e, 0, 0))
    grid_spec = pltpu.PrefetchScalarGridSpec(
        num_scalar_prefetch=3,
        grid=(N_EXPERTS,),
        in_specs=[pl.BlockSpec(memory_space=pl.ANY), wspec(D, 2 * D), wspec(1, 2 * D), wspec(D, D), wspec(1, D)],
        out_specs=pl.BlockSpec(memory_space=pl.ANY),
        scratch_shapes=[pltpu.VMEM((2, rows, LANES), F32), pltpu.VMEM((2, rows, LANES), F32),
                        pltpu.VMEM((D, 2 * D), BF16), pltpu.VMEM((D, D), BF16),
                        pltpu.SemaphoreType.DMA((2,)), pltpu.SemaphoreType.DMA((2,))],
    )
    return pl.pallas_call(
        _expert_kernel,
        grid_spec=grid_spec,
        out_shape=jax.ShapeDtypeStruct(x3.shape, F32),
        compiler_params=_cp(("arbitrary",)),
        name="experts",
    )(blk_start, blk_count, n_used, x3, w_gate_up, b_gate_up.reshape(N_EXPERTS, 1, 2 * D), w_down,
      b_down.reshape(N_EXPERTS, 1, D))


def _final_kernel(dcur_ref, dnext_ref, x1_ref, g2_ref, gpost_ref, rt_ref, y_hbm, o_ref, ybuf, sem):
    i = pl.program_id(0)
    n = pl.num_programs(0)
    tm = x1_ref.shape[0]
    slot = i % 2
    unroll = 8

    def issue(d_ref, s):
        def body(j, carry):
            for u in range(unroll):
                t = j * unroll + u
                for kk in range(TOP_K):
                    _row_copy(y_hbm, d_ref[t * TOP_K + kk], ybuf.at[s, kk], t * SUB, sem.at[s]).start(priority=kk % 2)
            return carry
        lax.fori_loop(0, tm // unroll, body, 0)

    @pl.when(i == 0)
    def _():
        issue(dcur_ref, 0)

    @pl.when(i + 1 < n)
    def _():
        issue(dnext_ref, 1 - slot)

    for kk in range(TOP_K):
        pltpu.make_async_copy(y_hbm.at[pl.ds(0, tm * SUB)], ybuf.at[slot, kk], sem.at[slot]).wait()

    y = jnp.zeros((tm, D), F32)
    for kk in range(TOP_K):
        y = y + rt_ref[:, RT_GATE + kk:RT_GATE + kk + 1] * _load_row_tiles(ybuf.at[slot, kk])
    o_ref[...] = x1_ref[...] + g2_ref[...] * _rms(y, gpost_ref[...])


def _final(x1, m3d, tiles_per_mod, g_post, route, dest_flat, y3, row0, tm):
    R = x1.shape[0]
    r = m3d.shape[1]
    off = row0 // tm
    nsteps = R // tm
    row = pl.BlockSpec((tm, D), lambda i: (i, 0))
    dspec = lambda nxt: pl.BlockSpec((tm * TOP_K,), lambda i: (jnp.minimum(i + nxt, nsteps - 1) + off,),
                                     memory_space=pltpu.SMEM)
    return pl.pallas_call(
        _final_kernel,
        grid=(nsteps,),
        in_specs=[dspec(0), dspec(1), row,
                  pl.BlockSpec((None, r, D), lambda i: (i // tiles_per_mod, 0, 5)),
                  pl.BlockSpec((1, D), lambda i: (0, 0)),
                  pl.BlockSpec((tm, LANES), lambda i: (i, 0)),
                  pl.BlockSpec(memory_space=pl.ANY)],
        out_specs=row,
        out_shape=jax.ShapeDtypeStruct((R, D), F32),
        scratch_shapes=[pltpu.VMEM((2, TOP_K, tm * SUB, LANES), F32), pltpu.SemaphoreType.DMA((2,))],
        compiler_params=_cp(("arbitrary",)),
        name="final",
    )(dest_flat, dest_flat, x1, m3d, g_post.reshape(1, D), route, y3)


def kernel(x_prompt, x_sample, cache_k, cache_v, state_hgrn, page_table, c_prompt, c_sample, w_mod, b_mod, g_pre_mix, g_post_mix, g_pre_ffn, g_post_ffn, w_in, hgrn_lb, g_hgrn, lam_q1, lam_k1, lam_q2, lam_k2, g_subln, w_proj_a, w_proj_b, w_out, w_router, b_router, w_gate_up, b_gate_up, w_down, b_down):
    B, L, _ = x_prompt.shape
    NS = x_sample.shape[0]
    T = B * L
    T_all = T + NS
    l = 0

    lb = jnp.cumsum(jax.nn.softmax(hgrn_lb.astype(F32), axis=0), axis=0)[l]
    lam = (jnp.exp(jnp.sum(lam_q1[l] * lam_k1[l])) - jnp.exp(jnp.sum(lam_q2[l] * lam_k2[l])) + LAM_INIT).astype(F32)
    slopes = jnp.exp2(-8.0 * jnp.arange(1, H + 1, dtype=F32) / H)

    m_all = _modulation(jnp.concatenate([c_prompt, c_sample], axis=0), w_mod[l], b_mod[l])
    m_p = m_all[:B].reshape(B, 1, 6 * D)
    m_s = m_all[B:].reshape(1, NS, 6 * D)

    w_in_b = w_in[l].astype(BF16)
    tm = 256
    tpm = L // tm
    xp = x_prompt.reshape(T, D)
    xs = x_sample.reshape(NS, D)
    merge_w = (g_post_mix[l], g_pre_ffn[l], w_proj_a[l].astype(BF16), w_proj_b[l].astype(BF16),
               w_out[l].astype(BF16), w_router[l], b_router[l])

    aq_p, af_p, ai_p, ag_p, bq_p, k_p, v_p, gt_p, k4_p, v4_p = _inproj(xp, m_p, L // INPROJ_ROWS, g_pre_mix[l], w_in_b,
                                                                       INPROJ_ROWS)
    seq = lambda a: a.reshape(B, L, W_MIX)
    oa_p, s_p = _hgrn_prompt(seq(aq_p), seq(af_p), seq(ai_p), seq(ag_p), lb, g_hgrn[l])
    ob_p = _attn_prompt(bq_p, k_p, v_p, lam, slopes, g_subln[l], B, L)
    x1_p, h3_p, rt_p, cnt = _merge(oa_p.reshape(T, W_MIX), ob_p, gt_p, xp, m_p, L // MERGE_ROWS, *merge_w,
                                   jnp.zeros((1, N_EXPERTS), F32), MERGE_ROWS)

    aq_s, af_s, ai_s, ag_s, bq_s, _, _, gt_s, k4_s, v4_s = _inproj(xs, m_s, 1, g_pre_mix[l], w_in_b, NS)
    hg_s = jnp.concatenate([a.astype(F32) for a in (aq_s, af_s, ai_s, ag_s)], axis=1)
    oa_s, s_s = _hgrn_step(hg_s, state_hgrn[l], lb, g_hgrn[l])
    ob_s = _attn_step(bq_s.astype(F32), k4_s.reshape(NS, W_MIX), v4_s.reshape(NS, W_MIX), cache_k[l], cache_v[l],
                      page_table, lam, slopes, g_subln[l])
    x1_s, h3_s, rt_s, cnt = _merge(oa_s.astype(BF16), ob_s.astype(BF16), gt_s, xs, m_s, 1, *merge_w, cnt, NS)

    bm = MOE_BLOCK
    counts = cnt.reshape(N_EXPERTS).astype(jnp.int32)
    padded = (counts + bm - 1) // bm * bm
    pad_end = jnp.cumsum(padded)
    pad_start = pad_end - padded
    n_blocks = -(-T_all * TOP_K // bm) + N_EXPERTS
    route = jnp.concatenate([rt_p[:, :RT_GATE], rt_s[:, :RT_GATE]], axis=0)
    e_idx = route[:, RT_E:RT_E + TOP_K].astype(jnp.int32)
    rank = route[:, RT_RANK:RT_RANK + TOP_K].astype(jnp.int32)
    e_iota = jnp.arange(N_EXPERTS, dtype=jnp.int32)
    dest = rank + jnp.sum(jnp.where(e_idx[..., None] == e_iota, pad_start, 0), axis=-1)
    dest_flat = dest.reshape(-1) * SUB
    n_used = pad_end[-1:] // bm

    x3 = _dispatch(h3_p, h3_s, dest_flat[:T * TOP_K], dest_flat[T * TOP_K:], pad_start + counts, pad_end, n_used,
                   n_blocks * bm)
    y3 = _experts(x3, pad_start // bm, padded // bm, n_used, w_gate_up[l], b_gate_up[l], w_down[l], b_down[l])
    y_p = _final(x1_p, m_p, tpm, g_post_ffn[l], rt_p, dest_flat, y3, 0, tm)
    y_s = _final(x1_s, m_s, 1, g_post_ffn[l], rt_s, dest_flat, y3, T, NS)

    return (y_p.reshape(B, L, D), y_s.reshape(NS, 1, D),
            k4_p.reshape(1, B, L, H, DH), v4_p.reshape(1, B, L, H, DH), s_p.reshape(1, B, H, DH, DH),
            k4_s.reshape(1, NS, 1, H, DH), v4_s.reshape(1, NS, 1, H, DH), s_s.reshape(1, NS, H, DH, DH))
```

```python
import functools
import math

import jax
import jax.numpy as jnp
from jax import lax
from jax.experimental import pallas as pl
from jax.experimental.pallas import tpu as pltpu

F32 = jnp.float32
BF16 = jnp.bfloat16

D = 1024
LANES = 128
SUB = D // LANES
H = 4
DH = 128
DK = 64
W_MIX = H * DH
HGRN_CHUNK = 64
N_EXPERTS = 32
TOP_K = 4
SWIGLU_LIMIT = 7.0
SWIGLU_ALPHA = 1.702
EPS = 1e-6
NEG_INF = -1e30
LOG2E = 1.4426950408889634
LAM_INIT = 0.8 - 0.6 * math.exp(-0.3 * 0)
C_HG = (0, 4 * W_MIX)
C_BQ = (4 * W_MIX, 5 * W_MIX)
C_BK = (5 * W_MIX, 6 * W_MIX)
C_BV = (6 * W_MIX, 7 * W_MIX)
C_GT = (7 * W_MIX, 7 * W_MIX + 2 * D)
D_IN = 7 * W_MIX + 2 * D
RT_E, RT_RANK, RT_GATE = 0, TOP_K, 2 * TOP_K

VMEM_LIMIT = 56 * 1024 * 1024
MOE_BLOCK = 256
DISPATCH_ROWS = 512
MERGE_ROWS = 512
INPROJ_ROWS = 512
ATTN_STEP_TOKENS = 2
ATT_G = 256
ATT_NG = 8
ATT_TK = 256
ATT_VR = DH + 16


def _cp(sem, vmem=VMEM_LIMIT):
    return pltpu.CompilerParams(dimension_semantics=sem, vmem_limit_bytes=vmem)


def _sigmoid(x):
    return 1.0 / (1.0 + jnp.exp(-x))


def _rms(x, g):
    return x * lax.rsqrt(jnp.mean(x * x, axis=-1, keepdims=True) + EPS) * g


def _dot(a, b):
    return jnp.dot(a, b, preferred_element_type=F32)


def _dot_nt(a, b):
    return lax.dot_general(a, b, (((1,), (1,)), ((), ())), preferred_element_type=F32)


def _dot_tn(a, b):
    return lax.dot_general(a, b, (((0,), (0,)), ((), ())), preferred_element_type=F32)


def _store_row_tiles(ref2, x2d, row0=0):
    n = x2d.shape[0]
    for j in range(SUB):
        ref2[pl.ds(row0 * SUB + j, n, stride=SUB), :] = x2d[:, j * LANES:(j + 1) * LANES]


def _load_row_tiles(ref2):
    n = ref2.shape[0] // SUB
    return jnp.concatenate([ref2[pl.ds(j, n, stride=SUB), :] for j in range(SUB)], axis=1)


def _mod_kernel(c_ref, w_ref, b_ref, o_ref):
    c = c_ref[...]
    o_ref[...] = _dot(c * _sigmoid(c), w_ref[...]) + b_ref[...]


def _modulation(c_all, w_mod, b_mod):
    n = c_all.shape[0]
    tn = 1024
    return pl.pallas_call(
        _mod_kernel,
        grid=(6 * D // tn,),
        in_specs=[pl.BlockSpec((n, D), lambda j: (0, 0)),
                  pl.BlockSpec((D, tn), lambda j: (0, j)),
                  pl.BlockSpec((1, tn), lambda j: (0, j))],
        out_specs=pl.BlockSpec((n, tn), lambda j: (0, j)),
        out_shape=jax.ShapeDtypeStruct((n, 6 * D), F32),
        compiler_params=_cp(("arbitrary",)),
        name="modulation",
    )(c_all, w_mod, b_mod.reshape(1, 6 * D))


def _inproj_kernel(x_ref, sc_ref, sh_ref, g_ref, w_ref,
                   aq_ref, af_ref, ai_ref, ag_ref, bq_ref, k_ref, v_ref, gt_ref, k4_ref, v4_ref):
    h = _rms(x_ref[...], g_ref[...]) * (1.0 + sc_ref[...]) + sh_ref[...]
    hb = h.astype(BF16)
    mm = lambda lo, hi: _dot(hb, w_ref[:, lo:hi])
    for j, o_ref in enumerate((aq_ref, af_ref, ai_ref, ag_ref)):
        o_ref[...] = mm(C_HG[0] + j * W_MIX, C_HG[0] + (j + 1) * W_MIX).astype(o_ref.dtype)
    bq_ref[...] = mm(*C_BQ).astype(BF16)
    gt_ref[...] = mm(*C_GT).astype(BF16)
    for c, o_ref, o4_ref in ((C_BK, k_ref, k4_ref), (C_BV, v_ref, v4_ref)):
        r = mm(*c)
        o_ref[...] = r.astype(BF16)
        o4 = o4_ref.reshape(r.shape[0] * H, DH)
        for hh in range(H):
            o4[pl.ds(hh, r.shape[0], stride=H), :] = r[:, hh * DH:(hh + 1) * DH]


def _inproj(x2d, m3d, tiles_per_mod, g_pre, w_in_bf16, tm):
    R = x2d.shape[0]
    r = m3d.shape[1]
    mod = lambda col: pl.BlockSpec((None, r, D), lambda i: (i // tiles_per_mod, 0, col))
    row = lambda w: pl.BlockSpec((tm, w), lambda i: (i, 0))
    row4 = pl.BlockSpec((tm, H, DH), lambda i: (i, 0, 0))
    outs = [(W_MIX, BF16), (W_MIX, F32), (W_MIX, BF16), (W_MIX, BF16), (W_MIX, BF16), (W_MIX, BF16), (W_MIX, BF16),
            (2 * D, BF16)]
    kv4 = jax.ShapeDtypeStruct((R, H, DH), F32)
    return pl.pallas_call(
        _inproj_kernel,
        grid=(R // tm,),
        in_specs=[row(D), mod(1), mod(0),
                  pl.BlockSpec((1, D), lambda i: (0, 0)),
                  pl.BlockSpec((D, D_IN), lambda i: (0, 0))],
        out_specs=[row(w) for w, _ in outs] + [row4, row4],
        out_shape=[jax.ShapeDtypeStruct((R, w), dt) for w, dt in outs] + [kv4, kv4],
        compiler_params=_cp(("arbitrary",)),
        name="inproj",
    )(x2d, m3d, m3d, g_pre.reshape(1, D), w_in_bf16)


def _hgrn_prompt_kernel(lb_ref, gn_ref, q_ref, f_ref, i_ref, g_ref, o_ref, sout_ref, st_ref):
    c = pl.program_id(0)
    nb = q_ref.shape[0]
    C = HGRN_CHUNK

    @pl.when(c == 0)
    def _():
        st_ref[...] = jnp.zeros_like(st_ref)

    row = lax.broadcasted_iota(jnp.int32, (C, DH), 0)
    causal = lax.broadcasted_iota(jnp.int32, (C, C), 0) >= lax.broadcasted_iota(jnp.int32, (C, C), 1)

    def per_batch_pair(bp, carry):
        ch = [(2 * bp + d, h, slice(h * DH, (h + 1) * DH)) for d in range(2) for h in range(H)]
        qd, kd, k2, v, eg = [], [], [], [], []
        for b, h, hs in ch:
            lb = lb_ref[:, hs]
            f = lb + (1.0 - lb) * _sigmoid(f_ref[b, :, hs].astype(F32))
            k = 1.0 - f
            gc = jnp.log(f)
            for s in (1, 2, 4, 8, 16, 32):
                gc = gc + jnp.where(row >= s, pltpu.roll(gc, s, axis=0), 0.0)
            g_last = gc[C - 1:C, :]
            qd.append(q_ref[b, :, hs].astype(F32) * jnp.exp(gc))
            kd.append(k * jnp.exp(-gc))
            k2.append(k * jnp.exp(g_last - gc))
            eg.append(jnp.exp(g_last))
            v.append(i_ref[b, :, hs].astype(F32))
        n = len(ch)
        st = [st_ref[b, h] for b, h, _ in ch]
        a = [jnp.where(causal, _dot_nt(qd[i], kd[i]), 0.0) for i in range(n)]
        o_inter = [_dot_nt(qd[i], st[i]) for i in range(n)]
        kv_t = [_dot_tn(v[i], k2[i]) for i in range(n)]
        o_intra = [_dot(a[i], v[i]) for i in range(n)]
        for i, (b, h, hs) in enumerate(ch):
            st_ref[b, h] = eg[i] * st[i] + kv_t[i]
            o = _rms(o_intra[i] + o_inter[i], gn_ref[:, hs])
            ag = g_ref[b, :, hs].astype(F32)
            o_ref[b, :, hs] = (o * (ag * _sigmoid(ag))).astype(o_ref.dtype)
        return carry

    lax.fori_loop(0, nb // 2, per_batch_pair, 0)

    @pl.when(c == pl.num_programs(0) - 1)
    def _():
        def fin(b, carry):
            for h in range(H):
                sout_ref[b, h] = st_ref[b, h].T
            return carry
        lax.fori_loop(0, nb, fin, 0)


def _hgrn_prompt(aq, af, ai, ag, lb, g_hgrn):
    B, L, _ = aq.shape
    C = HGRN_CHUNK
    blk = pl.BlockSpec((B, C, W_MIX), lambda c: (0, c, 0))
    vec = pl.BlockSpec((1, W_MIX), lambda c: (0, 0))
    return pl.pallas_call(
        _hgrn_prompt_kernel,
        grid=(L // C,),
        in_specs=[vec, vec, blk, blk, blk, blk],
        out_specs=[blk, pl.BlockSpec((B, H, DH, DH), lambda c: (0, 0, 0, 0))],
        out_shape=[jax.ShapeDtypeStruct((B, L, W_MIX), BF16),
                   jax.ShapeDtypeStruct((B, H, DH, DH), F32)],
        scratch_shapes=[pltpu.VMEM((B, H, DH, DH), F32)],
        compiler_params=_cp(("arbitrary",)),
        name="hgrn_prompt",
    )(lb.reshape(1, W_MIX), g_hgrn.reshape(1, W_MIX), aq, af, ai, ag)


def _hgrn_step_kernel(lb_ref, gn_ref, x_ref, s_ref, o_ref, sout_ref):
    nt = x_ref.shape[0]
    ones = jnp.ones((8, DH), F32)
    pad = jnp.zeros((6, DH), F32)

    def col(r):
        hi = r.astype(BF16).astype(F32)
        return _dot_tn(jnp.concatenate([hi, r - hi, pad], axis=0), ones)

    for t in range(nt):
        for h in range(H):
            seg = lambda j: x_ref[t:t + 1, j * W_MIX + h * DH: j * W_MIX + (h + 1) * DH]
            lb = lb_ref[:, h * DH:(h + 1) * DH]
            q, v, ag = seg(0), seg(2), seg(3)
            f = lb + (1.0 - lb) * _sigmoid(seg(1))
            f_col = col(f)
            s_new = f_col * s_ref[t, h] + (1.0 - f_col) * v
            sout_ref[t, h] = s_new
            o = jnp.sum(col(q) * s_new, axis=0, keepdims=True)
            o = _rms(o, gn_ref[:, h * DH:(h + 1) * DH])
            o_ref[t:t + 1, h * DH:(h + 1) * DH] = o * (ag * _sigmoid(ag))


def _hgrn_step(hg, state, lb, g_hgrn):
    N = hg.shape[0]
    nt = 8
    vec = pl.BlockSpec((1, W_MIX), lambda i: (0, 0))
    st = pl.BlockSpec((nt, H, DH, DH), lambda i: (i, 0, 0, 0))
    return pl.pallas_call(
        _hgrn_step_kernel,
        grid=(N // nt,),
        in_specs=[vec, vec, pl.BlockSpec((nt, 4 * W_MIX), lambda i: (i, 0)), st],
        out_specs=[pl.BlockSpec((nt, W_MIX), lambda i: (i, 0)), st],
        out_shape=[jax.ShapeDtypeStruct((N, W_MIX), F32),
                   jax.ShapeDtypeStruct((N, H, DH, DH), F32)],
        compiler_params=_cp(("arbitrary",)),
        name="hgrn_step",
    )(lb.reshape(1, W_MIX), g_hgrn.reshape(1, W_MIX), hg, state)


def _attn_prompt_kernel(lam_ref, sl_ref, q_ref, k_ref, v_ref, g_ref, o_ref, qt_ref, vt_ref, *chain_refs):
    G, NG, tk = ATT_G, ATT_NG, ATT_TK
    tq = G * NG
    m_refs = [chain_refs[mp * NG:(mp + 1) * NG] for mp in range(2)]
    acc_refs = [chain_refs[(2 + mp) * NG:(3 + mp) * NG] for mp in range(2)]
    slope = sl_ref[pl.program_id(1)] * LOG2E
    lam = lam_ref[0]
    L = q_ref.shape[0]
    q_t = q_ref[...].astype(F32).T * (DK ** -0.5 * LOG2E)
    sub = lax.broadcasted_iota(jnp.int32, (DH, L), 0)
    for mp in range(2):
        qm_t = jnp.where(sub < DK, q_t, 0.0) if mp == 0 else jnp.where(sub >= DK, q_t, 0.0)
        for t in range(L // G):
            qt_ref[mp, t] = qm_t[:, t * G:(t + 1) * G].astype(BF16)
    v_t = v_ref[...].astype(F32).T
    for kb in range(L // tk):
        vt_ref[kb, :DH, :] = v_t[:, kb * tk:(kb + 1) * tk].astype(BF16)
        vt_ref[kb, DH:, :] = jnp.ones((ATT_VR - DH, tk), BF16)
    idiff = lax.broadcasted_iota(jnp.int32, (tk, G), 1) - lax.broadcasted_iota(jnp.int32, (tk, G), 0)
    dmat = idiff.astype(F32) * slope

    def fold(qi, kb, chains):
        k0 = pl.multiple_of(kb * tk, tk)
        kt = k_ref[pl.ds(k0, tk), :]
        scores = [_dot(kt, qt_ref[mp, qi * NG + g]) for mp, g, _ in chains]
        probs = []
        for (mp, g, thresh), s in zip(chains, scores):
            s = s - dmat
            if thresh is not None:
                s = jnp.where(idiff >= thresh, s, NEG_INF)
            c = slope * ((qi * NG + g) * G - k0).astype(F32)
            m = m_refs[mp][g][...]
            m_new = jnp.maximum(m, jnp.max(s, axis=0, keepdims=True) - c)
            m_refs[mp][g][...] = m_new
            probs.append((jnp.exp2(s - (m_new + c)), jnp.exp2(m - m_new)))
        va = vt_ref[kb]
        pvs = [_dot(va, p.astype(BF16)) for p, _ in probs]
        for (mp, g, _), (_, alpha), pv in zip(chains, probs, pvs):
            acc_refs[mp][g][...] = alpha * acc_refs[mp][g][...] + pv

    def q_tile(qi, carry):
        for mp in range(2):
            for g in range(NG):
                m_refs[mp][g][...] = jnp.full((1, G), NEG_INF, F32)
                acc_refs[mp][g][...] = jnp.zeros((ATT_VR, G), F32)

        every = [(mp, g, None) for mp in range(2) for g in range(NG)]
        n_full = qi * (tq // tk)
        lax.fori_loop(0, n_full, lambda kb, c: (fold(qi, kb, every), c)[1], 0)
        for j in range(tq // tk):
            chains = []
            for mp in range(2):
                for g in range(NG):
                    thresh = j * tk - G * g
                    if thresh <= G - 1:
                        chains.append((mp, g, None if thresh <= -(tk - 1) else thresh))
            fold(qi, n_full + j, chains)


        for g in range(NG):
            a0, a1 = acc_refs[0][g][...], acc_refs[1][g][...]
            o_t = a0[:DH] / a0[DH:DH + 1] - lam * (a1[:DH] / a1[DH:DH + 1])
            o_t = o_t * lax.rsqrt(jnp.mean(o_t * o_t, axis=0, keepdims=True) + EPS)
            q0 = pl.multiple_of((qi * NG + g) * G, G)
            o_ref[pl.ds(q0, G), :] = (o_t.T * (g_ref[...] * (1.0 - LAM_INIT))).astype(o_ref.dtype)
        return carry

    lax.fori_loop(0, L // tq, q_tile, 0)


def _attn_prompt(bq, bk, bv, lam, slopes, g_subln, B, L):
    smem = pl.BlockSpec(memory_space=pltpu.SMEM)
    spec = pl.BlockSpec((L, DH), lambda b, h: (b, h))
    return pl.pallas_call(
        _attn_prompt_kernel,
        grid=(B, H),
        in_specs=[smem, smem, spec, spec, spec, pl.BlockSpec((1, DH), lambda b, h: (0, 0))],
        out_specs=spec,
        out_shape=jax.ShapeDtypeStruct((B * L, W_MIX), BF16),
        scratch_shapes=[pltpu.VMEM((2, L // ATT_G, DH, ATT_G), BF16),
                        pltpu.VMEM((L // ATT_TK, ATT_VR, ATT_TK), BF16)]
                       + [pltpu.VMEM((1, ATT_G), F32)] * (2 * ATT_NG)
                       + [pltpu.VMEM((ATT_VR, ATT_G), F32)] * (2 * ATT_NG),
        compiler_params=_cp(("arbitrary", "arbitrary")),
        name="attn_prompt",
    )(lam.reshape(1), slopes, bq, bk, bv, g_subln.reshape(1, DH))


def _attn_step_kernel(pt_ref, lam_ref, sl_ref, q_ref, kn_ref, vn_ref, g_ref, *refs, n_pages, page, n_tok):
    o_ref = refs[2 * n_tok * n_pages]
    PR = page * H
    past = n_pages * page
    lam = lam_ref[0]
    R = 2 * H
    grp = lax.broadcasted_iota(jnp.int32, (R, W_MIX), 1) // DK
    rid = lax.broadcasted_iota(jnp.int32, (R, W_MIX), 0)
    rhead = lax.broadcasted_iota(jnp.int32, (R, DH), 0) // 2
    slope = jnp.concatenate([jnp.full((2, 1), sl_ref[hh], F32) for hh in range(H)], axis=0)
    col = lax.broadcasted_iota(jnp.int32, (R, n_pages * PR), 1)
    own = (col % H) == (lax.broadcasted_iota(jnp.int32, (R, n_pages * PR), 0) // 2)
    bias = jnp.where(own, -slope * (past - col // H).astype(F32), NEG_INF)

    for t in range(n_tok):
        k_refs = refs[t * n_pages:(t + 1) * n_pages]
        v_refs = refs[(n_tok + t) * n_pages:(n_tok + t + 1) * n_pages]
        qbd = jnp.where(grp == rid, jnp.broadcast_to(q_ref[t], (R, W_MIX)), 0.0) * (DK ** -0.5)
        s_new = jnp.sum(qbd * kn_ref[t], axis=1, keepdims=True)
        qx = sum(qbd[:, hh * DH:(hh + 1) * DH] for hh in range(H))
        vnx = sum(jnp.where(rhead == hh, vn_ref[t, :, hh * DH:(hh + 1) * DH], 0.0) for hh in range(H))
        s = jnp.concatenate([_dot_nt(qx, k_refs[j][...]) for j in range(n_pages)], axis=1)
        s = s + bias
        m = jnp.maximum(jnp.max(s, axis=1, keepdims=True), s_new)
        p = jnp.exp(s - m)
        p_new = jnp.exp(s_new - m)
        inv_l = 1.0 / (jnp.sum(p, axis=1, keepdims=True) + p_new)
        p = p * inv_l
        o8 = (p_new * inv_l) * vnx
        for j in range(n_pages):
            o8 = o8 + _dot(p[:, j * PR:(j + 1) * PR], v_refs[j][...])
        for hh in range(H):
            o = o8[2 * hh:2 * hh + 1, :] - lam * o8[2 * hh + 1:2 * hh + 2, :]
            o_ref[t, :, hh * DH:(hh + 1) * DH] = _rms(o, g_ref[...]) * (1.0 - LAM_INIT)


def _attn_step(bq, k_new, v_new, cache_k, cache_v, page_table, lam, slopes, g_subln):
    N = bq.shape[0]
    n_pool, page = cache_k.shape[0], cache_k.shape[1]
    n_pages = page_table.shape[1]
    n_tok = ATTN_STEP_TOKENS
    cache_k = cache_k.reshape(n_pool * page * H, DH)
    cache_v = cache_v.reshape(n_pool * page * H, DH)
    smem = pl.BlockSpec(memory_space=pltpu.SMEM)
    rowspec = pl.BlockSpec((n_tok, 1, W_MIX), lambda b, pt: (b, 0, 0))
    pspec = lambda t, j: pl.BlockSpec((page * H, DH), lambda b, pt: (pt[(b * n_tok + t) * n_pages + j], 0))
    pages = [pspec(t, j) for t in range(n_tok) for j in range(n_pages)]
    grid_spec = pltpu.PrefetchScalarGridSpec(
        num_scalar_prefetch=1,
        grid=(N // n_tok,),
        in_specs=[smem, smem, rowspec, rowspec, rowspec, pl.BlockSpec((1, DH), lambda b, pt: (0, 0))] + pages * 2,
        out_specs=rowspec,
    )
    out = pl.pallas_call(
        functools.partial(_attn_step_kernel, n_pages=n_pages, page=page, n_tok=n_tok),
        grid_spec=grid_spec,
        out_shape=jax.ShapeDtypeStruct((N, 1, W_MIX), F32),
        compiler_params=_cp(("arbitrary",)),
        name="attn_step",
    )(page_table.reshape(-1), lam.reshape(1), slopes, bq.reshape(N, 1, W_MIX), k_new.reshape(N, 1, W_MIX),
      v_new.reshape(N, 1, W_MIX), g_subln.reshape(1, DH), *([cache_k] * (n_tok * n_pages)),
      *([cache_v] * (n_tok * n_pages)))
    return out.reshape(N, W_MIX)


def _merge_kernel(oa_ref, ob_ref, gt_ref, x_ref, g1_ref, sc2_ref, sh2_ref, gpost_ref, gpre_ref,
                  wa_ref, wb_ref, wo_ref, wr_ref, br_ref, cnt0_ref, x1_ref, h3_ref, rt_ref, cnt_ref, run_ref,
                  *, n_part):
    i = pl.program_id(0)
    tm = x_ref.shape[0]
    tp = tm // n_part
    parts = [slice(p * tp, (p + 1) * tp) for p in range(n_part)]
    rows = lambda ref, r: ref[...] if ref.shape[0] == 1 else ref[r, :]

    @pl.when(i == 0)
    def _():
        run_ref[...] = cnt0_ref[...]

    pa = [_dot(oa_ref[r, :], wa_ref[...]) for r in parts]
    pb = [_dot(ob_ref[r, :], wb_ref[...]) for r in parts]
    merged = [_sigmoid(gt_ref[r, :D].astype(F32)) * pa[p] + _sigmoid(gt_ref[r, D:].astype(F32)) * pb[p]
              for p, r in enumerate(parts)]
    z = [_dot(m.astype(BF16), wo_ref[...]) for m in merged]
    h2 = []
    for p, r in enumerate(parts):
        x1 = x_ref[r, :] + rows(g1_ref, r) * _rms(z[p], gpost_ref[...])
        x1_ref[r, :] = x1
        h2.append(_rms(x1, gpre_ref[...]) * (1.0 + rows(sc2_ref, r)) + rows(sh2_ref, r))
        _store_row_tiles(h3_ref, h2[p], p * tp)
    logits = [_dot(h, wr_ref[...]) + br_ref[...] for h in h2]

    lane = lax.broadcasted_iota(jnp.int32, (tp, N_EXPERTS), 1).astype(F32)
    sels, idxs, gates, chosen = [], [], [], []
    for p in range(n_part):
        work = logits[p]
        sel_p, idx_p, val_p = [], [], []
        for _ in range(TOP_K):
            mx = jnp.max(work, axis=1, keepdims=True)
            idx = jnp.min(jnp.where(work == mx, lane, float(N_EXPERTS)), axis=1, keepdims=True)
            sel = lane == idx
            work = jnp.where(sel, -jnp.inf, work)
            sel_p.append(sel)
            idx_p.append(idx)
            val_p.append(mx)
        ex = [jnp.exp(v - val_p[0]) for v in val_p]
        inv_den = 1.0 / (ex[0] + ex[1] + ex[2] + ex[3])
        sels.append(sel_p)
        idxs.append(idx_p)
        gates.append([e * inv_den for e in ex])
        chosen.append(sum(s.astype(F32) for s in sel_p))

    chosen_all = jnp.concatenate(chosen, axis=0) if n_part > 1 else chosen[0]
    before = (lax.broadcasted_iota(jnp.int32, (tm, tm), 0) > lax.broadcasted_iota(jnp.int32, (tm, tm), 1))
    prior = _dot(before.astype(BF16), chosen_all.astype(BF16)) + run_ref[...]
    run_ref[...] = run_ref[...] + jnp.sum(chosen_all, axis=0, keepdims=True)
    cnt_ref[...] = run_ref[...]

    rlane = lax.broadcasted_iota(jnp.int32, (tp, LANES), 1)
    for p, r in enumerate(parts):
        rt = jnp.zeros((tp, LANES), F32)
        for kk in range(TOP_K):
            rank = jnp.sum(jnp.where(sels[p][kk], prior[r, :], 0.0), axis=1, keepdims=True)
            rt = jnp.where(rlane == RT_E + kk, idxs[p][kk], rt)
            rt = jnp.where(rlane == RT_RANK + kk, rank, rt)
            rt = jnp.where(rlane == RT_GATE + kk, gates[p][kk], rt)
        rt_ref[r, :] = rt


def _merge(oa, ob, gt, x2d, m3d, tiles_per_mod, g_post, g_pre, wa, wb, wo, wr, br, cnt0, tm):
    R = x2d.shape[0]
    r = m3d.shape[1]
    mod = lambda col: pl.BlockSpec((None, r, D), lambda i: (i // tiles_per_mod, 0, col))
    row = lambda w: pl.BlockSpec((tm, w), lambda i: (i, 0))
    full = lambda a, b: pl.BlockSpec((a, b), lambda i: (0, 0))
    return pl.pallas_call(
        functools.partial(_merge_kernel, n_part=4 if tm >= 512 else 1),
        grid=(R // tm,),
        in_specs=[row(W_MIX), row(W_MIX), row(2 * D), row(D), mod(2), mod(4), mod(3),
                  full(1, D), full(1, D), full(W_MIX, D), full(W_MIX, D), full(D, D),
                  full(D, N_EXPERTS), full(1, N_EXPERTS), full(1, N_EXPERTS)],
        out_specs=[row(D), pl.BlockSpec((tm * SUB, LANES), lambda i: (i, 0)), row(LANES), full(1, N_EXPERTS)],
        out_shape=[jax.ShapeDtypeStruct((R, D), F32),
                   jax.ShapeDtypeStruct((R * SUB, LANES), F32),
                   jax.ShapeDtypeStruct((R, LANES), F32),
                   jax.ShapeDtypeStruct((1, N_EXPERTS), F32)],
        scratch_shapes=[pltpu.VMEM((1, N_EXPERTS), F32)],
        compiler_params=_cp(("arbitrary",)),
        name="merge",
    )(oa, ob, gt, x2d, m3d, m3d, m3d, g_post.reshape(1, D), g_pre.reshape(1, D), wa, wb, wo, wr,
      br.reshape(1, N_EXPERTS), cnt0)


def _row_copy(src, s_row, dst, d_row, sem):
    return pltpu.make_async_copy(src.at[pl.ds(pl.multiple_of(s_row, SUB), SUB)],
                                 dst.at[pl.ds(pl.multiple_of(d_row, SUB), SUB)], sem)


def _dispatch_kernel(lo_ref, hi_ref, nu_ref, da_ref, db_ref, ha_ref, hb_ref, x_hbm, zero_ref, sem):
    i = pl.program_id(0)
    last = pl.num_programs(0) - 1
    unroll = 8

    def scatter_rows(d_ref, h_ref):
        n = h_ref.shape[0] // SUB

        def issue(j, carry):
            for u in range(unroll):
                t = j * unroll + u
                for kk in range(TOP_K):
                    _row_copy(h_ref, t * SUB, x_hbm, d_ref[t * TOP_K + kk], sem).start(priority=kk % 2)
            return carry

        lax.fori_loop(0, n // unroll, issue, 0)
        for _ in range(TOP_K):
            pltpu.make_async_copy(h_ref, x_hbm.at[pl.ds(0, n * SUB)], sem).wait()

    @pl.when(i < last)
    def _():
        scatter_rows(da_ref, ha_ref)

    @pl.when(i == last)
    def _():
        scatter_rows(db_ref, hb_ref)
        zero_ref[...] = jnp.zeros_like(zero_ref)

        def per_expert(e, carry):
            zrow = lambda r: _row_copy(zero_ref, 0, x_hbm, r * SUB, sem)
            lax.fori_loop(lo_ref[e], hi_ref[e], lambda r, c: (zrow(r).start(), c)[1], 0)
            lax.fori_loop(lo_ref[e], hi_ref[e], lambda r, c: (zrow(r).wait(), c)[1], 0)
            return carry

        lax.fori_loop(0, N_EXPERTS, per_expert, 0)

        blk = zero_ref.shape[0]
        zblk = lambda b: pltpu.make_async_copy(zero_ref, x_hbm.at[pl.ds(pl.multiple_of(b * blk, blk), blk)], sem)
        lax.fori_loop(nu_ref[0], x_hbm.shape[0] // blk, lambda b, c: (zblk(b).start(), c)[1], 0)
        lax.fori_loop(nu_ref[0], x_hbm.shape[0] // blk, lambda b, c: (zblk(b).wait(), c)[1], 0)


def _dispatch(h3_a, h3_b, dest8_a, dest8_b, pad_lo, pad_hi, n_used, n_rows):
    tm = DISPATCH_ROWS
    n_a = h3_a.shape[0] // (tm * SUB)
    clamp = lambda i: jnp.minimum(i, n_a - 1)
    grid_spec = pltpu.PrefetchScalarGridSpec(
        num_scalar_prefetch=3,
        grid=(n_a + 1,),
        in_specs=[pl.BlockSpec((tm * TOP_K,), lambda i, *_: (clamp(i),), memory_space=pltpu.SMEM),
                  pl.BlockSpec(dest8_b.shape, lambda i, *_: (0,), memory_space=pltpu.SMEM),
                  pl.BlockSpec((tm * SUB, LANES), lambda i, *_: (clamp(i), 0)),
                  pl.BlockSpec(h3_b.shape, lambda i, *_: (0, 0))],
        out_specs=pl.BlockSpec(memory_space=pl.ANY),
        scratch_shapes=[pltpu.VMEM((MOE_BLOCK * SUB, LANES), F32), pltpu.SemaphoreType.DMA],
    )
    return pl.pallas_call(
        _dispatch_kernel,
        grid_spec=grid_spec,
        out_shape=jax.ShapeDtypeStruct((n_rows * SUB, LANES), F32),
        compiler_params=_cp(("arbitrary",)),
        name="dispatch",
    )(pad_lo, pad_hi, n_used, dest8_a, dest8_b, h3_a, h3_b)


def _expert_kernel(be_ref, nu_ref, x_ref, wgu_ref, bgu_ref, wd_ref, bd_ref, y_ref, wgu_b, wd_b):
    i = pl.program_id(0)

    @pl.when(i < nu_ref[0])
    def _():
        @pl.when((i == 0) | (be_ref[i] != be_ref[jnp.maximum(i - 1, 0)]))
        def _():
            wgu_b[...] = wgu_ref[...].astype(BF16)
            wd_b[...] = wd_ref[...].astype(BF16)

        gu = _dot(_load_row_tiles(x_ref).astype(BF16), wgu_b[...]) + bgu_ref[...]
        glu = jnp.minimum(gu[:, :D], SWIGLU_LIMIT)
        lin = jnp.clip(gu[:, D:], -SWIGLU_LIMIT, SWIGLU_LIMIT)
        act = (lin + 1.0) * glu * _sigmoid(SWIGLU_ALPHA * glu)
        _store_row_tiles(y_ref, _dot(act.astype(BF16), wd_b[...]) + bd_ref[...])

    @pl.when(i >= nu_ref[0])
    def _():
        y_ref[...] = jnp.zeros_like(y_ref)


def _experts(x3, block_expert, n_used, w_gate_up, b_gate_up, w_down, b_down):
    bm = MOE_BLOCK
    n_blocks = x3.shape[0] // (bm * SUB)
    used = lambda i, nu: jnp.minimum(i, nu[0] - 1)
    grid_spec = pltpu.PrefetchScalarGridSpec(
        num_scalar_prefetch=2,
        grid=(n_blocks,),
        in_specs=[pl.BlockSpec((bm * SUB, LANES), lambda i, be, nu: (used(i, nu), 0)),
                  pl.BlockSpec((None, D, 2 * D), lambda i, be, nu: (be[used(i, nu)], 0, 0)),
                  pl.BlockSpec((None, 1, 2 * D), lambda i, be, nu: (be[used(i, nu)], 0, 0)),
                  pl.BlockSpec((None, D, D), lambda i, be, nu: (be[used(i, nu)], 0, 0)),
                  pl.BlockSpec((None, 1, D), lambda i, be, nu: (be[used(i, nu)], 0, 0))],
        out_specs=pl.BlockSpec((bm * SUB, LANES), lambda i, be, nu: (i, 0)),
        scratch_shapes=[pltpu.VMEM((D, 2 * D), BF16), pltpu.VMEM((D, D), BF16)],
    )
    return pl.pallas_call(
        _expert_kernel,
        grid_spec=grid_spec,
        out_shape=jax.ShapeDtypeStruct(x3.shape, F32),
        compiler_params=_cp(("arbitrary",)),
        name="experts",
    )(block_expert, n_used, x3, w_gate_up, b_gate_up.reshape(N_EXPERTS, 1, 2 * D), w_down,
      b_down.reshape(N_EXPERTS, 1, D))


def _final_kernel(dcur_ref, dnext_ref, x1_ref, g2_ref, gpost_ref, rt_ref, y_hbm, o_ref, ybuf, sem):
    i = pl.program_id(0)
    n = pl.num_programs(0)
    tm = x1_ref.shape[0]
    slot = i % 2
    unroll = 8

    def issue(d_ref, s):
        def body(j, carry):
            for u in range(unroll):
                t = j * unroll + u
                for kk in range(TOP_K):
                    _row_copy(y_hbm, d_ref[t * TOP_K + kk], ybuf.at[s, kk], t * SUB, sem.at[s]).start(priority=kk % 2)
            return carry
        lax.fori_loop(0, tm // unroll, body, 0)

    @pl.when(i == 0)
    def _():
        issue(dcur_ref, 0)

    @pl.when(i + 1 < n)
    def _():
        issue(dnext_ref, 1 - slot)

    for kk in range(TOP_K):
        pltpu.make_async_copy(y_hbm.at[pl.ds(0, tm * SUB)], ybuf.at[slot, kk], sem.at[slot]).wait()

    y = jnp.zeros((tm, D), F32)
    for kk in range(TOP_K):
        y = y + rt_ref[:, RT_GATE + kk:RT_GATE + kk + 1] * _load_row_tiles(ybuf.at[slot, kk])
    o_ref[...] = x1_ref[...] + g2_ref[...] * _rms(y, gpost_ref[...])


def _final(x1, m3d, tiles_per_mod, g_post, route, dest_flat, y3, row0, tm):
    R = x1.shape[0]
    r = m3d.shape[1]
    off = row0 // tm
    nsteps = R // tm
    row = pl.BlockSpec((tm, D), lambda i: (i, 0))
    dspec = lambda nxt: pl.BlockSpec((tm * TOP_K,), lambda i: (jnp.minimum(i + nxt, nsteps - 1) + off,),
                                     memory_space=pltpu.SMEM)
    return pl.pallas_call(
        _final_kernel,
        grid=(nsteps,),
        in_specs=[dspec(0), dspec(1), row,
                  pl.BlockSpec((None, r, D), lambda i: (i // tiles_per_mod, 0, 5)),
                  pl.BlockSpec((1, D), lambda i: (0, 0)),
                  pl.BlockSpec((tm, LANES), lambda i: (i, 0)),
                  pl.BlockSpec(memory_space=pl.ANY)],
        out_specs=row,
        out_shape=jax.ShapeDtypeStruct((R, D), F32),
        scratch_shapes=[pltpu.VMEM((2, TOP_K, tm * SUB, LANES), F32), pltpu.SemaphoreType.DMA((2,))],
        compiler_params=_cp(("arbitrary",)),
        name="final",
    )(dest_flat, dest_flat, x1, m3d, g_post.reshape(1, D), route, y3)


def kernel(x_prompt, x_sample, cache_k, cache_v, state_hgrn, page_table, c_prompt, c_sample, w_mod, b_mod, g_pre_mix, g_post_mix, g_pre_ffn, g_post_ffn, w_in, hgrn_lb, g_hgrn, lam_q1, lam_k1, lam_q2, lam_k2, g_subln, w_proj_a, w_proj_b, w_out, w_router, b_router, w_gate_up, b_gate_up, w_down, b_down):
    B, L, _ = x_prompt.shape
    NS = x_sample.shape[0]
    T = B * L
    T_all = T + NS
    l = 0

    lb = jnp.cumsum(jax.nn.softmax(hgrn_lb.astype(F32), axis=0), axis=0)[l]
    lam = (jnp.exp(jnp.sum(lam_q1[l] * lam_k1[l])) - jnp.exp(jnp.sum(lam_q2[l] * lam_k2[l])) + LAM_INIT).astype(F32)
    slopes = jnp.exp2(-8.0 * jnp.arange(1, H + 1, dtype=F32) / H)

    m_all = _modulation(jnp.concatenate([c_prompt, c_sample], axis=0), w_mod[l], b_mod[l])
    m_p = m_all[:B].reshape(B, 1, 6 * D)
    m_s = m_all[B:].reshape(1, NS, 6 * D)

    w_in_b = w_in[l].astype(BF16)
    tm = 256
    tpm = L // tm
    xp = x_prompt.reshape(T, D)
    xs = x_sample.reshape(NS, D)
    merge_w = (g_post_mix[l], g_pre_ffn[l], w_proj_a[l].astype(BF16), w_proj_b[l].astype(BF16),
               w_out[l].astype(BF16), w_router[l], b_router[l])

    aq_p, af_p, ai_p, ag_p, bq_p, k_p, v_p, gt_p, k4_p, v4_p = _inproj(xp, m_p, L // INPROJ_ROWS, g_pre_mix[l], w_in_b,
                                                                       INPROJ_ROWS)
    seq = lambda a: a.reshape(B, L, W_MIX)
    oa_p, s_p = _hgrn_prompt(seq(aq_p), seq(af_p), seq(ai_p), seq(ag_p), lb, g_hgrn[l])
    ob_p = _attn_prompt(bq_p, k_p, v_p, lam, slopes, g_subln[l], B, L)
    x1_p, h3_p, rt_p, cnt = _merge(oa_p.reshape(T, W_MIX), ob_p, gt_p, xp, m_p, L // MERGE_ROWS, *merge_w,
                                   jnp.zeros((1, N_EXPERTS), F32), MERGE_ROWS)

    aq_s, af_s, ai_s, ag_s, bq_s, _, _, gt_s, k4_s, v4_s = _inproj(xs, m_s, 1, g_pre_mix[l], w_in_b, NS)
    hg_s = jnp.concatenate([a.astype(F32) for a in (aq_s, af_s, ai_s, ag_s)], axis=1)
    oa_s, s_s = _hgrn_step(hg_s, state_hgrn[l], lb, g_hgrn[l])
    ob_s = _attn_step(bq_s.astype(F32), k4_s.reshape(NS, W_MIX), v4_s.reshape(NS, W_MIX), cache_k[l], cache_v[l],
                      page_table, lam, slopes, g_subln[l])
    x1_s, h3_s, rt_s, cnt = _merge(oa_s.astype(BF16), ob_s.astype(BF16), gt_s, xs, m_s, 1, *merge_w, cnt, NS)

    bm = MOE_BLOCK
    counts = cnt.reshape(N_EXPERTS).astype(jnp.int32)
    padded = (counts + bm - 1) // bm * bm
    pad_end = jnp.cumsum(padded)
    pad_start = pad_end - padded
    n_blocks = -(-T_all * TOP_K // bm) + N_EXPERTS
    route = jnp.concatenate([rt_p[:, :RT_GATE], rt_s[:, :RT_GATE]], axis=0)
    e_idx = route[:, RT_E:RT_E + TOP_K].astype(jnp.int32)
    rank = route[:, RT_RANK:RT_RANK + TOP_K].astype(jnp.int32)
    e_iota = jnp.arange(N_EXPERTS, dtype=jnp.int32)
    dest = rank + jnp.sum(jnp.where(e_idx[..., None] == e_iota, pad_start, 0), axis=-1)
    dest_flat = dest.reshape(-1) * SUB
    block_expert = jnp.minimum(
        jnp.sum((jnp.arange(n_blocks, dtype=jnp.int32)[:, None] * bm >= pad_end[None, :]).astype(jnp.int32), axis=1),
        N_EXPERTS - 1)
    n_used = pad_end[-1:] // bm

    x3 = _dispatch(h3_p, h3_s, dest_flat[:T * TOP_K], dest_flat[T * TOP_K:], pad_start + counts, pad_end, n_used,
                   n_blocks * bm)
    y3 = _experts(x3, block_expert, n_used, w_gate_up[l], b_gate_up[l], w_down[l], b_down[l])
    y_p = _final(x1_p, m_p, tpm, g_post_ffn[l], rt_p, dest_flat, y3, 0, tm)
    y_s = _final(x1_s, m_s, 1, g_post_ffn[l], rt_s, dest_flat, y3, T, NS)

    return (y_p.reshape(B, L, D), y_s.reshape(NS, 1, D),
            k4_p.reshape(1, B, L, H, DH), v4_p.reshape(1, B, L, H, DH), s_p.reshape(1, B, H, DH, DH),
            k4_s.reshape(1, NS, 1, H, DH), v4_s.reshape(1, NS, 1, H, DH), s_s.reshape(1, NS, H, DH, DH))
```

```python
import functools
import math

import jax
import jax.numpy as jnp
from jax import lax
from jax.experimental import pallas as pl
from jax.experimental.pallas import tpu as pltpu

F32 = jnp.float32
BF16 = jnp.bfloat16

D = 1024
LANES = 128
SUB = D // LANES
H = 4
DH = 128
DK = 64
W_MIX = H * DH
HGRN_CHUNK = 64
N_EXPERTS = 32
TOP_K = 4
SWIGLU_LIMIT = 7.0
SWIGLU_ALPHA = 1.702
EPS = 1e-6
NEG_INF = -1e30
LOG2E = 1.4426950408889634
LAM_INIT = 0.8 - 0.6 * math.exp(-0.3 * 0)
C_HG = (0, 4 * W_MIX)
C_BQ = (4 * W_MIX, 5 * W_MIX)
C_BK = (5 * W_MIX, 6 * W_MIX)
C_BV = (6 * W_MIX, 7 * W_MIX)
C_GT = (7 * W_MIX, 7 * W_MIX + 2 * D)
D_IN = 7 * W_MIX + 2 * D
RT_E, RT_RANK, RT_GATE = 0, TOP_K, 2 * TOP_K

VMEM_LIMIT = 56 * 1024 * 1024
MOE_BLOCK = 256
DISPATCH_ROWS = 512
MERGE_ROWS = 512
INPROJ_ROWS = 512
ATTN_STEP_TOKENS = 2
ATT_G = 256
ATT_NG = 8
ATT_TK = 256
ATT_VR = DH + 16


def _cp(sem, vmem=VMEM_LIMIT):
    return pltpu.CompilerParams(dimension_semantics=sem, vmem_limit_bytes=vmem)


def _sigmoid(x):
    return 1.0 / (1.0 + jnp.exp(-x))


def _rms(x, g):
    return x * lax.rsqrt(jnp.mean(x * x, axis=-1, keepdims=True) + EPS) * g


def _dot(a, b):
    return jnp.dot(a, b, preferred_element_type=F32)


def _dot_nt(a, b):
    return lax.dot_general(a, b, (((1,), (1,)), ((), ())), preferred_element_type=F32)


def _dot_tn(a, b):
    return lax.dot_general(a, b, (((0,), (0,)), ((), ())), preferred_element_type=F32)


def _store_row_tiles(ref2, x2d, row0=0):
    n = x2d.shape[0]
    for j in range(SUB):
        ref2[pl.ds(row0 * SUB + j, n, stride=SUB), :] = x2d[:, j * LANES:(j + 1) * LANES]


def _load_row_tiles(ref2):
    n = ref2.shape[0] // SUB
    return jnp.concatenate([ref2[pl.ds(j, n, stride=SUB), :] for j in range(SUB)], axis=1)


def _mod_kernel(c_ref, w_ref, b_ref, o_ref):
    c = c_ref[...]
    o_ref[...] = _dot(c * _sigmoid(c), w_ref[...]) + b_ref[...]


def _modulation(c_all, w_mod, b_mod):
    n = c_all.shape[0]
    tn = 1024
    return pl.pallas_call(
        _mod_kernel,
        grid=(6 * D // tn,),
        in_specs=[pl.BlockSpec((n, D), lambda j: (0, 0)),
                  pl.BlockSpec((D, tn), lambda j: (0, j)),
                  pl.BlockSpec((1, tn), lambda j: (0, j))],
        out_specs=pl.BlockSpec((n, tn), lambda j: (0, j)),
        out_shape=jax.ShapeDtypeStruct((n, 6 * D), F32),
        compiler_params=_cp(("arbitrary",)),
        name="modulation",
    )(c_all, w_mod, b_mod.reshape(1, 6 * D))


def _inproj_kernel(x_ref, sc_ref, sh_ref, g_ref, w_ref,
                   aq_ref, af_ref, ai_ref, ag_ref, bq_ref, k_ref, v_ref, gt_ref, k4_ref, v4_ref):
    h = _rms(x_ref[...], g_ref[...]) * (1.0 + sc_ref[...]) + sh_ref[...]
    hb = h.astype(BF16)
    mm = lambda lo, hi: _dot(hb, w_ref[:, lo:hi])
    for j, o_ref in enumerate((aq_ref, af_ref, ai_ref, ag_ref)):
        o_ref[...] = mm(C_HG[0] + j * W_MIX, C_HG[0] + (j + 1) * W_MIX).astype(o_ref.dtype)
    bq_ref[...] = mm(*C_BQ).astype(BF16)
    gt_ref[...] = mm(*C_GT).astype(BF16)
    for c, o_ref, o4_ref in ((C_BK, k_ref, k4_ref), (C_BV, v_ref, v4_ref)):
        r = mm(*c)
        o_ref[...] = r.astype(BF16)
        o4 = o4_ref.reshape(r.shape[0] * H, DH)
        for hh in range(H):
            o4[pl.ds(hh, r.shape[0], stride=H), :] = r[:, hh * DH:(hh + 1) * DH]


def _inproj(x2d, m3d, tiles_per_mod, g_pre, w_in_bf16, tm):
    R = x2d.shape[0]
    r = m3d.shape[1]
    mod = lambda col: pl.BlockSpec((None, r, D), lambda i: (i // tiles_per_mod, 0, col))
    row = lambda w: pl.BlockSpec((tm, w), lambda i: (i, 0))
    row4 = pl.BlockSpec((tm, H, DH), lambda i: (i, 0, 0))
    outs = [(W_MIX, BF16), (W_MIX, F32), (W_MIX, BF16), (W_MIX, BF16), (W_MIX, BF16), (W_MIX, BF16), (W_MIX, BF16),
            (2 * D, BF16)]
    kv4 = jax.ShapeDtypeStruct((R, H, DH), F32)
    return pl.pallas_call(
        _inproj_kernel,
        grid=(R // tm,),
        in_specs=[row(D), mod(1), mod(0),
                  pl.BlockSpec((1, D), lambda i: (0, 0)),
                  pl.BlockSpec((D, D_IN), lambda i: (0, 0))],
        out_specs=[row(w) for w, _ in outs] + [row4, row4],
        out_shape=[jax.ShapeDtypeStruct((R, w), dt) for w, dt in outs] + [kv4, kv4],
        compiler_params=_cp(("arbitrary",)),
        name="inproj",
    )(x2d, m3d, m3d, g_pre.reshape(1, D), w_in_bf16)


def _hgrn_prompt_kernel(lb_ref, gn_ref, q_ref, f_ref, i_ref, g_ref, o_ref, sout_ref, st_ref):
    c = pl.program_id(0)
    nb = q_ref.shape[0]
    C = HGRN_CHUNK

    @pl.when(c == 0)
    def _():
        st_ref[...] = jnp.zeros_like(st_ref)

    row = lax.broadcasted_iota(jnp.int32, (C, DH), 0)
    causal = lax.broadcasted_iota(jnp.int32, (C, C), 0) >= lax.broadcasted_iota(jnp.int32, (C, C), 1)

    def per_batch_pair(bp, carry):
        ch = [(2 * bp + d, h, slice(h * DH, (h + 1) * DH)) for d in range(2) for h in range(H)]
        qd, kd, k2, v, eg = [], [], [], [], []
        for b, h, hs in ch:
            lb = lb_ref[:, hs]
            f = lb + (1.0 - lb) * _sigmoid(f_ref[b, :, hs].astype(F32))
            k = 1.0 - f
            gc = jnp.log(f)
            for s in (1, 2, 4, 8, 16, 32):
                gc = gc + jnp.where(row >= s, pltpu.roll(gc, s, axis=0), 0.0)
            g_last = gc[C - 1:C, :]
            qd.append(q_ref[b, :, hs].astype(F32) * jnp.exp(gc))
            kd.append(k * jnp.exp(-gc))
            k2.append(k * jnp.exp(g_last - gc))
            eg.append(jnp.exp(g_last))
            v.append(i_ref[b, :, hs].astype(F32))
        n = len(ch)
        st = [st_ref[b, h] for b, h, _ in ch]
        a = [jnp.where(causal, _dot_nt(qd[i], kd[i]), 0.0) for i in range(n)]
        o_inter = [_dot_nt(qd[i], st[i]) for i in range(n)]
        kv_t = [_dot_tn(v[i], k2[i]) for i in range(n)]
        o_intra = [_dot(a[i], v[i]) for i in range(n)]
        for i, (b, h, hs) in enumerate(ch):
            st_ref[b, h] = eg[i] * st[i] + kv_t[i]
            o = _rms(o_intra[i] + o_inter[i], gn_ref[:, hs])
            ag = g_ref[b, :, hs].astype(F32)
            o_ref[b, :, hs] = (o * (ag * _sigmoid(ag))).astype(o_ref.dtype)
        return carry

    lax.fori_loop(0, nb // 2, per_batch_pair, 0)

    @pl.when(c == pl.num_programs(0) - 1)
    def _():
        def fin(b, carry):
            for h in range(H):
                sout_ref[b, h] = st_ref[b, h].T
            return carry
        lax.fori_loop(0, nb, fin, 0)


def _hgrn_prompt(aq, af, ai, ag, lb, g_hgrn):
    B, L, _ = aq.shape
    C = HGRN_CHUNK
    blk = pl.BlockSpec((B, C, W_MIX), lambda c: (0, c, 0))
    vec = pl.BlockSpec((1, W_MIX), lambda c: (0, 0))
    return pl.pallas_call(
        _hgrn_prompt_kernel,
        grid=(L // C,),
        in_specs=[vec, vec, blk, blk, blk, blk],
        out_specs=[blk, pl.BlockSpec((B, H, DH, DH), lambda c: (0, 0, 0, 0))],
        out_shape=[jax.ShapeDtypeStruct((B, L, W_MIX), BF16),
                   jax.ShapeDtypeStruct((B, H, DH, DH), F32)],
        scratch_shapes=[pltpu.VMEM((B, H, DH, DH), F32)],
        compiler_params=_cp(("arbitrary",)),
        name="hgrn_prompt",
    )(lb.reshape(1, W_MIX), g_hgrn.reshape(1, W_MIX), aq, af, ai, ag)


def _hgrn_step_kernel(lb_ref, gn_ref, x_ref, s_ref, o_ref, sout_ref):
    nt = x_ref.shape[0]
    ones = jnp.ones((8, DH), F32)
    pad = jnp.zeros((6, DH), F32)

    def col(r):
        hi = r.astype(BF16).astype(F32)
        return _dot_tn(jnp.concatenate([hi, r - hi, pad], axis=0), ones)

    for t in range(nt):
        for h in range(H):
            seg = lambda j: x_ref[t:t + 1, j * W_MIX + h * DH: j * W_MIX + (h + 1) * DH]
            lb = lb_ref[:, h * DH:(h + 1) * DH]
            q, v, ag = seg(0), seg(2), seg(3)
            f = lb + (1.0 - lb) * _sigmoid(seg(1))
            f_col = col(f)
            s_new = f_col * s_ref[t, h] + (1.0 - f_col) * v
            sout_ref[t, h] = s_new
            o = jnp.sum(col(q) * s_new, axis=0, keepdims=True)
            o = _rms(o, gn_ref[:, h * DH:(h + 1) * DH])
            o_ref[t:t + 1, h * DH:(h + 1) * DH] = o * (ag * _sigmoid(ag))


def _hgrn_step(hg, state, lb, g_hgrn):
    N = hg.shape[0]
    nt = 8
    vec = pl.BlockSpec((1, W_MIX), lambda i: (0, 0))
    st = pl.BlockSpec((nt, H, DH, DH), lambda i: (i, 0, 0, 0))
    return pl.pallas_call(
        _hgrn_step_kernel,
        grid=(N // nt,),
        in_specs=[vec, vec, pl.BlockSpec((nt, 4 * W_MIX), lambda i: (i, 0)), st],
        out_specs=[pl.BlockSpec((nt, W_MIX), lambda i: (i, 0)), st],
        out_shape=[jax.ShapeDtypeStruct((N, W_MIX), F32),
                   jax.ShapeDtypeStruct((N, H, DH, DH), F32)],
        compiler_params=_cp(("arbitrary",)),
        name="hgrn_step",
    )(lb.reshape(1, W_MIX), g_hgrn.reshape(1, W_MIX), hg, state)


def _attn_prompt_kernel(lam_ref, sl_ref, q_ref, k_ref, v_ref, g_ref, o_ref, qt_ref, vt_ref, *chain_refs):
    G, NG, tk = ATT_G, ATT_NG, ATT_TK
    tq = G * NG
    m_refs = [chain_refs[mp * NG:(mp + 1) * NG] for mp in range(2)]
    acc_refs = [chain_refs[(2 + mp) * NG:(3 + mp) * NG] for mp in range(2)]
    slope = sl_ref[pl.program_id(1)] * LOG2E
    lam = lam_ref[0]
    L = q_ref.shape[0]
    q_t = q_ref[...].astype(F32).T * (DK ** -0.5 * LOG2E)
    sub = lax.broadcasted_iota(jnp.int32, (DH, L), 0)
    for mp in range(2):
        qm_t = jnp.where(sub < DK, q_t, 0.0) if mp == 0 else jnp.where(sub >= DK, q_t, 0.0)
        for t in range(L // G):
            qt_ref[mp, t] = qm_t[:, t * G:(t + 1) * G].astype(BF16)
    v_t = v_ref[...].astype(F32).T
    for kb in range(L // tk):
        vt_ref[kb, :DH, :] = v_t[:, kb * tk:(kb + 1) * tk].astype(BF16)
        vt_ref[kb, DH:, :] = jnp.ones((ATT_VR - DH, tk), BF16)
    idiff = lax.broadcasted_iota(jnp.int32, (tk, G), 1) - lax.broadcasted_iota(jnp.int32, (tk, G), 0)
    dmat = idiff.astype(F32) * slope

    def fold(qi, kb, chains):
        k0 = pl.multiple_of(kb * tk, tk)
        kt = k_ref[pl.ds(k0, tk), :]
        scores = [_dot(kt, qt_ref[mp, qi * NG + g]) for mp, g, _ in chains]
        probs = []
        for (mp, g, thresh), s in zip(chains, scores):
            s = s - dmat
            if thresh is not None:
                s = jnp.where(idiff >= thresh, s, NEG_INF)
            c = slope * ((qi * NG + g) * G - k0).astype(F32)
            m = m_refs[mp][g][...]
            m_new = jnp.maximum(m, jnp.max(s, axis=0, keepdims=True) - c)
            m_refs[mp][g][...] = m_new
            probs.append((jnp.exp2(s - (m_new + c)), jnp.exp2(m - m_new)))
        va = vt_ref[kb]
        pvs = [_dot(va, p.astype(BF16)) for p, _ in probs]
        for (mp, g, _), (_, alpha), pv in zip(chains, probs, pvs):
            acc_refs[mp][g][...] = alpha * acc_refs[mp][g][...] + pv

    def q_tile(qi, carry):
        for mp in range(2):
            for g in range(NG):
                m_refs[mp][g][...] = jnp.full((1, G), NEG_INF, F32)
                acc_refs[mp][g][...] = jnp.zeros((ATT_VR, G), F32)

        every = [(mp, g, None) for mp in range(2) for g in range(NG)]
        n_full = qi * (tq // tk)
        lax.fori_loop(0, n_full, lambda kb, c: (fold(qi, kb, every), c)[1], 0)
        for j in range(tq // tk):
            chains = []
            for mp in range(2):
                for g in range(NG):
                    thresh = j * tk - G * g
                    if thresh <= G - 1:
                        chains.append((mp, g, None if thresh <= -(tk - 1) else thresh))
            fold(qi, n_full + j, chains)


        for g in range(NG):
            a0, a1 = acc_refs[0][g][...], acc_refs[1][g][...]
            o_t = a0[:DH] / a0[DH:DH + 1] - lam * (a1[:DH] / a1[DH:DH + 1])
            o_t = o_t * lax.rsqrt(jnp.mean(o_t * o_t, axis=0, keepdims=True) + EPS)
            q0 = pl.multiple_of((qi * NG + g) * G, G)
            o_ref[pl.ds(q0, G), :] = (o_t.T * (g_ref[...] * (1.0 - LAM_INIT))).astype(o_ref.dtype)
        return carry

    lax.fori_loop(0, L // tq, q_tile, 0)


def _attn_prompt(bq, bk, bv, lam, slopes, g_subln, B, L):
    smem = pl.BlockSpec(memory_space=pltpu.SMEM)
    spec = pl.BlockSpec((L, DH), lambda b, h: (b, h))
    return pl.pallas_call(
        _attn_prompt_kernel,
        grid=(B, H),
        in_specs=[smem, smem, spec, spec, spec, pl.BlockSpec((1, DH), lambda b, h: (0, 0))],
        out_specs=spec,
        out_shape=jax.ShapeDtypeStruct((B * L, W_MIX), BF16),
        scratch_shapes=[pltpu.VMEM((2, L // ATT_G, DH, ATT_G), BF16),
                        pltpu.VMEM((L // ATT_TK, ATT_VR, ATT_TK), BF16)]
                       + [pltpu.VMEM((1, ATT_G), F32)] * (2 * ATT_NG)
                       + [pltpu.VMEM((ATT_VR, ATT_G), F32)] * (2 * ATT_NG),
        compiler_params=_cp(("arbitrary", "arbitrary")),
        name="attn_prompt",
    )(lam.reshape(1), slopes, bq, bk, bv, g_subln.reshape(1, DH))


def _attn_step_kernel(pt_ref, lam_ref, sl_ref, q_ref, kn_ref, vn_ref, g_ref, *refs, n_pages, page, n_tok):
    o_ref = refs[2 * n_tok * n_pages]
    PR = page * H
    past = n_pages * page
    lam = lam_ref[0]
    R = 2 * H
    grp = lax.broadcasted_iota(jnp.int32, (R, W_MIX), 1) // DK
    rid = lax.broadcasted_iota(jnp.int32, (R, W_MIX), 0)
    rhead = lax.broadcasted_iota(jnp.int32, (R, DH), 0) // 2
    slope = jnp.concatenate([jnp.full((2, 1), sl_ref[hh], F32) for hh in range(H)], axis=0)
    col = lax.broadcasted_iota(jnp.int32, (R, n_pages * PR), 1)
    own = (col % H) == (lax.broadcasted_iota(jnp.int32, (R, n_pages * PR), 0) // 2)
    bias = jnp.where(own, -slope * (past - col // H).astype(F32), NEG_INF)

    for t in range(n_tok):
        k_refs = refs[t * n_pages:(t + 1) * n_pages]
        v_refs = refs[(n_tok + t) * n_pages:(n_tok + t + 1) * n_pages]
        qbd = jnp.where(grp == rid, jnp.broadcast_to(q_ref[t], (R, W_MIX)), 0.0) * (DK ** -0.5)
        s_new = jnp.sum(qbd * kn_ref[t], axis=1, keepdims=True)
        qx = sum(qbd[:, hh * DH:(hh + 1) * DH] for hh in range(H))
        vnx = sum(jnp.where(rhead == hh, vn_ref[t, :, hh * DH:(hh + 1) * DH], 0.0) for hh in range(H))
        s = jnp.concatenate([_dot_nt(qx, k_refs[j][...]) for j in range(n_pages)], axis=1)
        s = s + bias
        m = jnp.maximum(jnp.max(s, axis=1, keepdims=True), s_new)
        p = jnp.exp(s - m)
        p_new = jnp.exp(s_new - m)
        inv_l = 1.0 / (jnp.sum(p, axis=1, keepdims=True) + p_new)
        p = p * inv_l
        o8 = (p_new * inv_l) * vnx
        for j in range(n_pages):
            o8 = o8 + _dot(p[:, j * PR:(j + 1) * PR], v_refs[j][...])
        for hh in range(H):
            o = o8[2 * hh:2 * hh + 1, :] - lam * o8[2 * hh + 1:2 * hh + 2, :]
            o_ref[t, :, hh * DH:(hh + 1) * DH] = _rms(o, g_ref[...]) * (1.0 - LAM_INIT)


def _attn_step(bq, k_new, v_new, cache_k, cache_v, page_table, lam, slopes, g_subln):
    N = bq.shape[0]
    n_pool, page = cache_k.shape[0], cache_k.shape[1]
    n_pages = page_table.shape[1]
    n_tok = ATTN_STEP_TOKENS
    cache_k = cache_k.reshape(n_pool * page * H, DH)
    cache_v = cache_v.reshape(n_pool * page * H, DH)
    smem = pl.BlockSpec(memory_space=pltpu.SMEM)
    rowspec = pl.BlockSpec((n_tok, 1, W_MIX), lambda b, pt: (b, 0, 0))
    pspec = lambda t, j: pl.BlockSpec((page * H, DH), lambda b, pt: (pt[(b * n_tok + t) * n_pages + j], 0))
    pages = [pspec(t, j) for t in range(n_tok) for j in range(n_pages)]
    grid_spec = pltpu.PrefetchScalarGridSpec(
        num_scalar_prefetch=1,
        grid=(N // n_tok,),
        in_specs=[smem, smem, rowspec, rowspec, rowspec, pl.BlockSpec((1, DH), lambda b, pt: (0, 0))] + pages * 2,
        out_specs=rowspec,
    )
    out = pl.pallas_call(
        functools.partial(_attn_step_kernel, n_pages=n_pages, page=page, n_tok=n_tok),
        grid_spec=grid_spec,
        out_shape=jax.ShapeDtypeStruct((N, 1, W_MIX), F32),
        compiler_params=_cp(("arbitrary",)),
        name="attn_step",
    )(page_table.reshape(-1), lam.reshape(1), slopes, bq.reshape(N, 1, W_MIX), k_new.reshape(N, 1, W_MIX),
      v_new.reshape(N, 1, W_MIX), g_subln.reshape(1, DH), *([cache_k] * (n_tok * n_pages)),
      *([cache_v] * (n_tok * n_pages)))
    return out.reshape(N, W_MIX)


def _merge_kernel(oa_ref, ob_ref, gt_ref, x_ref, g1_ref, sc2_ref, sh2_ref, gpost_ref, gpre_ref,
                  wa_ref, wb_ref, wo_ref, wr_ref, br_ref, cnt0_ref, x1_ref, h3_ref, rt_ref, rtt_ref, cnt_ref, run_ref,
                  *, n_part):
    i = pl.program_id(0)
    tm = x_ref.shape[0]
    tp = tm // n_part
    parts = [slice(p * tp, (p + 1) * tp) for p in range(n_part)]
    rows = lambda ref, r: ref[...] if ref.shape[0] == 1 else ref[r, :]

    @pl.when(i == 0)
    def _():
        run_ref[...] = cnt0_ref[...]

    pa = [_dot(oa_ref[r, :], wa_ref[...]) for r in parts]
    pb = [_dot(ob_ref[r, :], wb_ref[...]) for r in parts]
    merged = [_sigmoid(gt_ref[r, :D].astype(F32)) * pa[p] + _sigmoid(gt_ref[r, D:].astype(F32)) * pb[p]
              for p, r in enumerate(parts)]
    z = [_dot(m.astype(BF16), wo_ref[...]) for m in merged]
    h2 = []
    for p, r in enumerate(parts):
        x1 = x_ref[r, :] + rows(g1_ref, r) * _rms(z[p], gpost_ref[...])
        x1_ref[r, :] = x1
        h2.append(_rms(x1, gpre_ref[...]) * (1.0 + rows(sc2_ref, r)) + rows(sh2_ref, r))
        _store_row_tiles(h3_ref, h2[p], p * tp)
    logits = [_dot(h, wr_ref[...]) + br_ref[...] for h in h2]

    lane = lax.broadcasted_iota(jnp.int32, (tp, N_EXPERTS), 1).astype(F32)
    sels, idxs, gates, chosen = [], [], [], []
    for p in range(n_part):
        work = logits[p]
        sel_p, idx_p, val_p = [], [], []
        for _ in range(TOP_K):
            mx = jnp.max(work, axis=1, keepdims=True)
            idx = jnp.min(jnp.where(work == mx, lane, float(N_EXPERTS)), axis=1, keepdims=True)
            sel = lane == idx
            work = jnp.where(sel, -jnp.inf, work)
            sel_p.append(sel)
            idx_p.append(idx)
            val_p.append(mx)
        ex = [jnp.exp(v - val_p[0]) for v in val_p]
        inv_den = 1.0 / (ex[0] + ex[1] + ex[2] + ex[3])
        sels.append(sel_p)
        idxs.append(idx_p)
        gates.append([e * inv_den for e in ex])
        chosen.append(sum(s.astype(F32) for s in sel_p))

    chosen_all = jnp.concatenate(chosen, axis=0) if n_part > 1 else chosen[0]
    before = (lax.broadcasted_iota(jnp.int32, (tm, tm), 0) > lax.broadcasted_iota(jnp.int32, (tm, tm), 1))
    prior = _dot(before.astype(BF16), chosen_all.astype(BF16)) + run_ref[...]
    run_ref[...] = run_ref[...] + jnp.sum(chosen_all, axis=0, keepdims=True)
    cnt_ref[...] = run_ref[...]

    rlane = lax.broadcasted_iota(jnp.int32, (tp, LANES), 1)
    for p, r in enumerate(parts):
        rt = jnp.zeros((tp, LANES), F32)
        for kk in range(TOP_K):
            rank = jnp.sum(jnp.where(sels[p][kk], prior[r, :], 0.0), axis=1, keepdims=True)
            rt = jnp.where(rlane == RT_E + kk, idxs[p][kk], rt)
            rt = jnp.where(rlane == RT_RANK + kk, rank, rt)
            rt = jnp.where(rlane == RT_GATE + kk, gates[p][kk], rt)
        rt_ref[r, :] = rt
        rtt_ref[:, r] = rt.T[:RT_GATE, :]


def _merge(oa, ob, gt, x2d, m3d, tiles_per_mod, g_post, g_pre, wa, wb, wo, wr, br, cnt0, tm):
    R = x2d.shape[0]
    r = m3d.shape[1]
    mod = lambda col: pl.BlockSpec((None, r, D), lambda i: (i // tiles_per_mod, 0, col))
    row = lambda w: pl.BlockSpec((tm, w), lambda i: (i, 0))
    full = lambda a, b: pl.BlockSpec((a, b), lambda i: (0, 0))
    return pl.pallas_call(
        functools.partial(_merge_kernel, n_part=4 if tm >= 512 else 1),
        grid=(R // tm,),
        in_specs=[row(W_MIX), row(W_MIX), row(2 * D), row(D), mod(2), mod(4), mod(3),
                  full(1, D), full(1, D), full(W_MIX, D), full(W_MIX, D), full(D, D),
                  full(D, N_EXPERTS), full(1, N_EXPERTS), full(1, N_EXPERTS)],
        out_specs=[row(D), pl.BlockSpec((tm * SUB, LANES), lambda i: (i, 0)), row(LANES),
                   pl.BlockSpec((RT_GATE, tm), lambda i: (0, i)), full(1, N_EXPERTS)],
        out_shape=[jax.ShapeDtypeStruct((R, D), F32),
                   jax.ShapeDtypeStruct((R * SUB, LANES), F32),
                   jax.ShapeDtypeStruct((R, LANES), F32),
                   jax.ShapeDtypeStruct((RT_GATE, R), F32),
                   jax.ShapeDtypeStruct((1, N_EXPERTS), F32)],
        scratch_shapes=[pltpu.VMEM((1, N_EXPERTS), F32)],
        compiler_params=_cp(("arbitrary",)),
        name="merge",
    )(oa, ob, gt, x2d, m3d, m3d, m3d, g_post.reshape(1, D), g_pre.reshape(1, D), wa, wb, wo, wr,
      br.reshape(1, N_EXPERTS), cnt0)


def _row_copy(src, s_row, dst, d_row, sem):
    return pltpu.make_async_copy(src.at[pl.ds(pl.multiple_of(s_row, SUB), SUB)],
                                 dst.at[pl.ds(pl.multiple_of(d_row, SUB), SUB)], sem)


def _dispatch_kernel(lo_ref, hi_ref, nu_ref, da_ref, db_ref, ha_ref, hb_ref, x_hbm, zero_ref, sem):
    i = pl.program_id(0)
    last = pl.num_programs(0) - 1
    unroll = 8

    def scatter_rows(d_ref, h_ref):
        n = h_ref.shape[0] // SUB

        def issue(j, carry):
            for u in range(unroll):
                t = j * unroll + u
                for kk in range(TOP_K):
                    _row_copy(h_ref, t * SUB, x_hbm, d_ref[kk, t], sem).start(priority=kk % 2)
            return carry

        lax.fori_loop(0, n // unroll, issue, 0)
        for _ in range(TOP_K):
            pltpu.make_async_copy(h_ref, x_hbm.at[pl.ds(0, n * SUB)], sem).wait()

    @pl.when(i < last)
    def _():
        scatter_rows(da_ref, ha_ref)

    @pl.when(i == last)
    def _():
        scatter_rows(db_ref, hb_ref)
        zero_ref[...] = jnp.zeros_like(zero_ref)

        def per_expert(e, carry):
            zrow = lambda r: _row_copy(zero_ref, 0, x_hbm, r * SUB, sem)
            lax.fori_loop(lo_ref[e], hi_ref[e], lambda r, c: (zrow(r).start(), c)[1], 0)
            lax.fori_loop(lo_ref[e], hi_ref[e], lambda r, c: (zrow(r).wait(), c)[1], 0)
            return carry

        lax.fori_loop(0, N_EXPERTS, per_expert, 0)

        blk = zero_ref.shape[0]
        zblk = lambda b: pltpu.make_async_copy(zero_ref, x_hbm.at[pl.ds(pl.multiple_of(b * blk, blk), blk)], sem)
        lax.fori_loop(nu_ref[0], x_hbm.shape[0] // blk, lambda b, c: (zblk(b).start(), c)[1], 0)
        lax.fori_loop(nu_ref[0], x_hbm.shape[0] // blk, lambda b, c: (zblk(b).wait(), c)[1], 0)


def _dispatch(h3_a, h3_b, dest8_a, dest8_b, pad_lo, pad_hi, n_used, n_rows):
    tm = DISPATCH_ROWS
    n_a = h3_a.shape[0] // (tm * SUB)
    clamp = lambda i: jnp.minimum(i, n_a - 1)
    grid_spec = pltpu.PrefetchScalarGridSpec(
        num_scalar_prefetch=3,
        grid=(n_a + 1,),
        in_specs=[pl.BlockSpec((TOP_K, tm), lambda i, *_: (0, clamp(i)), memory_space=pltpu.SMEM),
                  pl.BlockSpec(dest8_b.shape, lambda i, *_: (0, 0), memory_space=pltpu.SMEM),
                  pl.BlockSpec((tm * SUB, LANES), lambda i, *_: (clamp(i), 0)),
                  pl.BlockSpec(h3_b.shape, lambda i, *_: (0, 0))],
        out_specs=pl.BlockSpec(memory_space=pl.ANY),
        scratch_shapes=[pltpu.VMEM((MOE_BLOCK * SUB, LANES), F32), pltpu.SemaphoreType.DMA],
    )
    return pl.pallas_call(
        _dispatch_kernel,
        grid_spec=grid_spec,
        out_shape=jax.ShapeDtypeStruct((n_rows * SUB, LANES), F32),
        compiler_params=_cp(("arbitrary",)),
        name="dispatch",
    )(pad_lo, pad_hi, n_used, dest8_a, dest8_b, h3_a, h3_b)


def _expert_kernel(be_ref, nu_ref, x_ref, wgu_ref, bgu_ref, wd_ref, bd_ref, y_ref, wgu_b, wd_b):
    i = pl.program_id(0)

    @pl.when(i < nu_ref[0])
    def _():
        @pl.when((i == 0) | (be_ref[i] != be_ref[jnp.maximum(i - 1, 0)]))
        def _():
            wgu_b[...] = wgu_ref[...].astype(BF16)
            wd_b[...] = wd_ref[...].astype(BF16)

        gu = _dot(_load_row_tiles(x_ref).astype(BF16), wgu_b[...]) + bgu_ref[...]
        glu = jnp.minimum(gu[:, :D], SWIGLU_LIMIT)
        lin = jnp.clip(gu[:, D:], -SWIGLU_LIMIT, SWIGLU_LIMIT)
        act = (lin + 1.0) * glu * _sigmoid(SWIGLU_ALPHA * glu)
        _store_row_tiles(y_ref, _dot(act.astype(BF16), wd_b[...]) + bd_ref[...])

    @pl.when(i >= nu_ref[0])
    def _():
        y_ref[...] = jnp.zeros_like(y_ref)


def _experts(x3, block_expert, n_used, w_gate_up, b_gate_up, w_down, b_down):
    bm = MOE_BLOCK
    n_blocks = x3.shape[0] // (bm * SUB)
    used = lambda i, nu: jnp.minimum(i, nu[0] - 1)
    grid_spec = pltpu.PrefetchScalarGridSpec(
        num_scalar_prefetch=2,
        grid=(n_blocks,),
        in_specs=[pl.BlockSpec((bm * SUB, LANES), lambda i, be, nu: (used(i, nu), 0)),
                  pl.BlockSpec((None, D, 2 * D), lambda i, be, nu: (be[used(i, nu)], 0, 0)),
                  pl.BlockSpec((None, 1, 2 * D), lambda i, be, nu: (be[used(i, nu)], 0, 0)),
                  pl.BlockSpec((None, D, D), lambda i, be, nu: (be[used(i, nu)], 0, 0)),
                  pl.BlockSpec((None, 1, D), lambda i, be, nu: (be[used(i, nu)], 0, 0))],
        out_specs=pl.BlockSpec((bm * SUB, LANES), lambda i, be, nu: (i, 0)),
        scratch_shapes=[pltpu.VMEM((D, 2 * D), BF16), pltpu.VMEM((D, D), BF16)],
    )
    return pl.pallas_call(
        _expert_kernel,
        grid_spec=grid_spec,
        out_shape=jax.ShapeDtypeStruct(x3.shape, F32),
        compiler_params=_cp(("arbitrary",)),
        name="experts",
    )(block_expert, n_used, x3, w_gate_up, b_gate_up.reshape(N_EXPERTS, 1, 2 * D), w_down,
      b_down.reshape(N_EXPERTS, 1, D))


def _final_kernel(dcur_ref, dnext_ref, x1_ref, g2_ref, gpost_ref, rt_ref, y_hbm, o_ref, ybuf, sem):
    i = pl.program_id(0)
    n = pl.num_programs(0)
    tm = x1_ref.shape[0]
    slot = i % 2
    unroll = 8

    def issue(d_ref, s):
        def body(j, carry):
            for u in range(unroll):
                t = j * unroll + u
                for kk in range(TOP_K):
                    _row_copy(y_hbm, d_ref[kk, t], ybuf.at[s, kk], t * SUB, sem.at[s]).start(priority=kk % 2)
            return carry
        lax.fori_loop(0, tm // unroll, body, 0)

    @pl.when(i == 0)
    def _():
        issue(dcur_ref, 0)

    @pl.when(i + 1 < n)
    def _():
        issue(dnext_ref, 1 - slot)

    for kk in range(TOP_K):
        pltpu.make_async_copy(y_hbm.at[pl.ds(0, tm * SUB)], ybuf.at[slot, kk], sem.at[slot]).wait()

    y = jnp.zeros((tm, D), F32)
    for kk in range(TOP_K):
        y = y + rt_ref[:, RT_GATE + kk:RT_GATE + kk + 1] * _load_row_tiles(ybuf.at[slot, kk])
    o_ref[...] = x1_ref[...] + g2_ref[...] * _rms(y, gpost_ref[...])


def _final(x1, m3d, tiles_per_mod, g_post, route, dest8, y3, row0, tm):
    R = x1.shape[0]
    r = m3d.shape[1]
    off = row0 // tm
    nsteps = R // tm
    row = pl.BlockSpec((tm, D), lambda i: (i, 0))
    dspec = lambda nxt: pl.BlockSpec((TOP_K, tm), lambda i: (0, jnp.minimum(i + nxt, nsteps - 1) + off),
                                     memory_space=pltpu.SMEM)
    return pl.pallas_call(
        _final_kernel,
        grid=(nsteps,),
        in_specs=[dspec(0), dspec(1), row,
                  pl.BlockSpec((None, r, D), lambda i: (i // tiles_per_mod, 0, 5)),
                  pl.BlockSpec((1, D), lambda i: (0, 0)),
                  pl.BlockSpec((tm, LANES), lambda i: (i, 0)),
                  pl.BlockSpec(memory_space=pl.ANY)],
        out_specs=row,
        out_shape=jax.ShapeDtypeStruct((R, D), F32),
        scratch_shapes=[pltpu.VMEM((2, TOP_K, tm * SUB, LANES), F32), pltpu.SemaphoreType.DMA((2,))],
        compiler_params=_cp(("arbitrary",)),
        name="final",
    )(dest8, dest8, x1, m3d, g_post.reshape(1, D), route, y3)


def kernel(x_prompt, x_sample, cache_k, cache_v, state_hgrn, page_table, c_prompt, c_sample, w_mod, b_mod, g_pre_mix, g_post_mix, g_pre_ffn, g_post_ffn, w_in, hgrn_lb, g_hgrn, lam_q1, lam_k1, lam_q2, lam_k2, g_subln, w_proj_a, w_proj_b, w_out, w_router, b_router, w_gate_up, b_gate_up, w_down, b_down):
    B, L, _ = x_prompt.shape
    NS = x_sample.shape[0]
    T = B * L
    T_all = T + NS
    l = 0

    lb = jnp.cumsum(jax.nn.softmax(hgrn_lb.astype(F32), axis=0), axis=0)[l]
    lam = (jnp.exp(jnp.sum(lam_q1[l] * lam_k1[l])) - jnp.exp(jnp.sum(lam_q2[l] * lam_k2[l])) + LAM_INIT).astype(F32)
    slopes = jnp.exp2(-8.0 * jnp.arange(1, H + 1, dtype=F32) / H)

    m_all = _modulation(jnp.concatenate([c_prompt, c_sample], axis=0), w_mod[l], b_mod[l])
    m_p = m_all[:B].reshape(B, 1, 6 * D)
    m_s = m_all[B:].reshape(1, NS, 6 * D)

    w_in_b = w_in[l].astype(BF16)
    tm = 256
    tpm = L // tm
    xp = x_prompt.reshape(T, D)
    xs = x_sample.reshape(NS, D)
    merge_w = (g_post_mix[l], g_pre_ffn[l], w_proj_a[l].astype(BF16), w_proj_b[l].astype(BF16),
               w_out[l].astype(BF16), w_router[l], b_router[l])

    aq_p, af_p, ai_p, ag_p, bq_p, k_p, v_p, gt_p, k4_p, v4_p = _inproj(xp, m_p, L // INPROJ_ROWS, g_pre_mix[l], w_in_b,
                                                                       INPROJ_ROWS)
    seq = lambda a: a.reshape(B, L, W_MIX)
    oa_p, s_p = _hgrn_prompt(seq(aq_p), seq(af_p), seq(ai_p), seq(ag_p), lb, g_hgrn[l])
    ob_p = _attn_prompt(bq_p, k_p, v_p, lam, slopes, g_subln[l], B, L)
    x1_p, h3_p, rt_p, rtt_p, cnt = _merge(oa_p.reshape(T, W_MIX), ob_p, gt_p, xp, m_p, L // MERGE_ROWS, *merge_w,
                                   jnp.zeros((1, N_EXPERTS), F32), MERGE_ROWS)

    aq_s, af_s, ai_s, ag_s, bq_s, _, _, gt_s, k4_s, v4_s = _inproj(xs, m_s, 1, g_pre_mix[l], w_in_b, NS)
    hg_s = jnp.concatenate([a.astype(F32) for a in (aq_s, af_s, ai_s, ag_s)], axis=1)
    oa_s, s_s = _hgrn_step(hg_s, state_hgrn[l], lb, g_hgrn[l])
    ob_s = _attn_step(bq_s.astype(F32), k4_s.reshape(NS, W_MIX), v4_s.reshape(NS, W_MIX), cache_k[l], cache_v[l],
                      page_table, lam, slopes, g_subln[l])
    x1_s, h3_s, rt_s, rtt_s, cnt = _merge(oa_s.astype(BF16), ob_s.astype(BF16), gt_s, xs, m_s, 1, *merge_w, cnt, NS)

    bm = MOE_BLOCK
    counts = cnt.reshape(N_EXPERTS).astype(jnp.int32)
    padded = (counts + bm - 1) // bm * bm
    pad_end = jnp.cumsum(padded)
    pad_start = pad_end - padded
    n_blocks = -(-T_all * TOP_K // bm) + N_EXPERTS
    route_t = jnp.concatenate([rtt_p, rtt_s], axis=1).astype(jnp.int32)
    e_idx, rank = route_t[RT_E:RT_E + TOP_K], route_t[RT_RANK:RT_RANK + TOP_K]
    e_iota = jnp.arange(N_EXPERTS, dtype=jnp.int32)[:, None, None]
    dest8 = (rank + jnp.sum(jnp.where(e_idx[None] == e_iota, pad_start[:, None, None], 0), axis=0)) * SUB
    block_expert = jnp.minimum(
        jnp.sum((jnp.arange(n_blocks, dtype=jnp.int32)[:, None] * bm >= pad_end[None, :]).astype(jnp.int32), axis=1),
        N_EXPERTS - 1)
    n_used = pad_end[-1:] // bm

    x3 = _dispatch(h3_p, h3_s, dest8, dest8[:, T:], pad_start + counts, pad_end, n_used,
                   n_blocks * bm)
    y3 = _experts(x3, block_expert, n_used, w_gate_up[l], b_gate_up[l], w_down[l], b_down[l])
    y_p = _final(x1_p, m_p, tpm, g_post_ffn[l], rt_p, dest8, y3, 0, tm)
    y_s = _final(x1_s, m_s, 1, g_post_ffn[l], rt_s, dest8, y3, T, NS)

    return (y_p.reshape(B, L, D), y_s.reshape(NS, 1, D),
            k4_p.reshape(1, B, L, H, DH), v4_p.reshape(1, B, L, H, DH), s_p.reshape(1, B, H, DH, DH),
            k4_s.reshape(1, NS, 1, H, DH), v4_s.reshape(1, NS, 1, H, DH), s_s.reshape(1, NS, H, DH, DH))
```

```python
import functools
import math

import jax
import jax.numpy as jnp
from jax import lax
from jax.experimental import pallas as pl
from jax.experimental.pallas import tpu as pltpu

F32 = jnp.float32
BF16 = jnp.bfloat16

D = 1024
LANES = 128
SUB = D // LANES
H = 4
DH = 128
DK = 64
W_MIX = H * DH
HGRN_CHUNK = 64
N_EXPERTS = 32
TOP_K = 4
SWIGLU_LIMIT = 7.0
SWIGLU_ALPHA = 1.702
EPS = 1e-6
NEG_INF = -1e30
LOG2E = 1.4426950408889634
LAM_INIT = 0.8 - 0.6 * math.exp(-0.3 * 0)
C_HG = (0, 4 * W_MIX)
C_BQ = (4 * W_MIX, 5 * W_MIX)
C_BK = (5 * W_MIX, 6 * W_MIX)
C_BV = (6 * W_MIX, 7 * W_MIX)
C_GT = (7 * W_MIX, 7 * W_MIX + 2 * D)
D_IN = 7 * W_MIX + 2 * D
RT_E, RT_RANK, RT_GATE = 0, TOP_K, 2 * TOP_K

VMEM_LIMIT = 56 * 1024 * 1024
MOE_BLOCK = 256
DISPATCH_ROWS = 512
MERGE_ROWS = 512
INPROJ_ROWS = 512
ATTN_STEP_TOKENS = 2
ATT_G = 256
ATT_NG = 8
ATT_TK = 256
ATT_VR = DH + 16


def _cp(sem, vmem=VMEM_LIMIT):
    return pltpu.CompilerParams(dimension_semantics=sem, vmem_limit_bytes=vmem)


def _sigmoid(x):
    return 1.0 / (1.0 + jnp.exp(-x))


def _rms(x, g):
    return x * lax.rsqrt(jnp.mean(x * x, axis=-1, keepdims=True) + EPS) * g


def _dot(a, b):
    return jnp.dot(a, b, preferred_element_type=F32)


def _dot_nt(a, b):
    return lax.dot_general(a, b, (((1,), (1,)), ((), ())), preferred_element_type=F32)


def _dot_tn(a, b):
    return lax.dot_general(a, b, (((0,), (0,)), ((), ())), preferred_element_type=F32)


def _store_row_tiles(ref2, x2d, row0=0):
    n = x2d.shape[0]
    for j in range(SUB):
        ref2[pl.ds(row0 * SUB + j, n, stride=SUB), :] = x2d[:, j * LANES:(j + 1) * LANES]


def _load_row_tiles(ref2):
    n = ref2.shape[0] // SUB
    return jnp.concatenate([ref2[pl.ds(j, n, stride=SUB), :] for j in range(SUB)], axis=1)


def _mod_kernel(c_ref, w_ref, b_ref, o_ref):
    c = c_ref[...]
    o_ref[...] = _dot(c * _sigmoid(c), w_ref[...]) + b_ref[...]


def _modulation(c_all, w_mod, b_mod):
    n = c_all.shape[0]
    tn = 1024
    return pl.pallas_call(
        _mod_kernel,
        grid=(6 * D // tn,),
        in_specs=[pl.BlockSpec((n, D), lambda j: (0, 0)),
                  pl.BlockSpec((D, tn), lambda j: (0, j)),
                  pl.BlockSpec((1, tn), lambda j: (0, j))],
        out_specs=pl.BlockSpec((n, tn), lambda j: (0, j)),
        out_shape=jax.ShapeDtypeStruct((n, 6 * D), F32),
        compiler_params=_cp(("arbitrary",)),
        name="modulation",
    )(c_all, w_mod, b_mod.reshape(1, 6 * D))


def _inproj_kernel(x_ref, sc_ref, sh_ref, g_ref, w_ref,
                   aq_ref, af_ref, ai_ref, ag_ref, bq_ref, k_ref, v_ref, gt_ref, k4_ref, v4_ref):
    h = _rms(x_ref[...], g_ref[...]) * (1.0 + sc_ref[...]) + sh_ref[...]
    hb = h.astype(BF16)
    mm = lambda lo, hi: _dot(hb, w_ref[:, lo:hi])
    for j, o_ref in enumerate((aq_ref, af_ref, ai_ref, ag_ref)):
        o_ref[...] = mm(C_HG[0] + j * W_MIX, C_HG[0] + (j + 1) * W_MIX).astype(o_ref.dtype)
    bq_ref[...] = mm(*C_BQ).astype(BF16)
    gt_ref[...] = mm(*C_GT).astype(BF16)
    for c, o_ref, o4_ref in ((C_BK, k_ref, k4_ref), (C_BV, v_ref, v4_ref)):
        r = mm(*c)
        o_ref[...] = r.astype(BF16)
        o4 = o4_ref.reshape(r.shape[0] * H, DH)
        for hh in range(H):
            o4[pl.ds(hh, r.shape[0], stride=H), :] = r[:, hh * DH:(hh + 1) * DH]


def _inproj(x2d, m3d, tiles_per_mod, g_pre, w_in_bf16, tm):
    R = x2d.shape[0]
    r = m3d.shape[1]
    mod = lambda col: pl.BlockSpec((None, r, D), lambda i: (i // tiles_per_mod, 0, col))
    row = lambda w: pl.BlockSpec((tm, w), lambda i: (i, 0))
    row4 = pl.BlockSpec((tm, H, DH), lambda i: (i, 0, 0))
    outs = [(W_MIX, BF16), (W_MIX, F32), (W_MIX, BF16), (W_MIX, BF16), (W_MIX, BF16), (W_MIX, BF16), (W_MIX, BF16),
            (2 * D, BF16)]
    kv4 = jax.ShapeDtypeStruct((R, H, DH), F32)
    return pl.pallas_call(
        _inproj_kernel,
        grid=(R // tm,),
        in_specs=[row(D), mod(1), mod(0),
                  pl.BlockSpec((1, D), lambda i: (0, 0)),
                  pl.BlockSpec((D, D_IN), lambda i: (0, 0))],
        out_specs=[row(w) for w, _ in outs] + [row4, row4],
        out_shape=[jax.ShapeDtypeStruct((R, w), dt) for w, dt in outs] + [kv4, kv4],
        compiler_params=_cp(("arbitrary",)),
        name="inproj",
    )(x2d, m3d, m3d, g_pre.reshape(1, D), w_in_bf16)


def _hgrn_prompt_kernel(lb_ref, gn_ref, q_ref, f_ref, i_ref, g_ref, o_ref, sout_ref, st_ref):
    c = pl.program_id(0)
    nb = q_ref.shape[0]
    C = HGRN_CHUNK

    @pl.when(c == 0)
    def _():
        st_ref[...] = jnp.zeros_like(st_ref)

    row = lax.broadcasted_iota(jnp.int32, (C, DH), 0)
    causal = lax.broadcasted_iota(jnp.int32, (C, C), 0) >= lax.broadcasted_iota(jnp.int32, (C, C), 1)

    def per_batch_pair(bp, carry):
        ch = [(2 * bp + d, h, slice(h * DH, (h + 1) * DH)) for d in range(2) for h in range(H)]
        qd, kd, k2, v, eg = [], [], [], [], []
        for b, h, hs in ch:
            lb = lb_ref[:, hs]
            f = lb + (1.0 - lb) * _sigmoid(f_ref[b, :, hs].astype(F32))
            k = 1.0 - f
            gc = jnp.log(f)
            for s in (1, 2, 4, 8, 16, 32):
                gc = gc + jnp.where(row >= s, pltpu.roll(gc, s, axis=0), 0.0)
            g_last = gc[C - 1:C, :]
            qd.append(q_ref[b, :, hs].astype(F32) * jnp.exp(gc))
            kd.append(k * jnp.exp(-gc))
            k2.append(k * jnp.exp(g_last - gc))
            eg.append(jnp.exp(g_last))
            v.append(i_ref[b, :, hs].astype(F32))
        n = len(ch)
        st = [st_ref[b, h] for b, h, _ in ch]
        a = [jnp.where(causal, _dot_nt(qd[i], kd[i]), 0.0) for i in range(n)]
        o_inter = [_dot_nt(qd[i], st[i]) for i in range(n)]
        kv_t = [_dot_tn(v[i], k2[i]) for i in range(n)]
        o_intra = [_dot(a[i], v[i]) for i in range(n)]
        for i, (b, h, hs) in enumerate(ch):
            st_ref[b, h] = eg[i] * st[i] + kv_t[i]
            o = _rms(o_intra[i] + o_inter[i], gn_ref[:, hs])
            ag = g_ref[b, :, hs].astype(F32)
            o_ref[b, :, hs] = (o * (ag * _sigmoid(ag))).astype(o_ref.dtype)
        return carry

    lax.fori_loop(0, nb // 2, per_batch_pair, 0)

    @pl.when(c == pl.num_programs(0) - 1)
    def _():
        def fin(b, carry):
            for h in range(H):
                sout_ref[b, h] = st_ref[b, h].T
            return carry
        lax.fori_loop(0, nb, fin, 0)


def _hgrn_prompt(aq, af, ai, ag, lb, g_hgrn):
    B, L, _ = aq.shape
    C = HGRN_CHUNK
    blk = pl.BlockSpec((B, C, W_MIX), lambda c: (0, c, 0))
    vec = pl.BlockSpec((1, W_MIX), lambda c: (0, 0))
    return pl.pallas_call(
        _hgrn_prompt_kernel,
        grid=(L // C,),
        in_specs=[vec, vec, blk, blk, blk, blk],
        out_specs=[blk, pl.BlockSpec((B, H, DH, DH), lambda c: (0, 0, 0, 0))],
        out_shape=[jax.ShapeDtypeStruct((B, L, W_MIX), BF16),
                   jax.ShapeDtypeStruct((B, H, DH, DH), F32)],
        scratch_shapes=[pltpu.VMEM((B, H, DH, DH), F32)],
        compiler_params=_cp(("arbitrary",)),
        name="hgrn_prompt",
    )(lb.reshape(1, W_MIX), g_hgrn.reshape(1, W_MIX), aq, af, ai, ag)


def _hgrn_step_kernel(lb_ref, gn_ref, x_ref, s_ref, o_ref, sout_ref):
    nt = x_ref.shape[0]
    ones = jnp.ones((8, DH), F32)
    pad = jnp.zeros((6, DH), F32)

    def col(r):
        hi = r.astype(BF16).astype(F32)
        return _dot_tn(jnp.concatenate([hi, r - hi, pad], axis=0), ones)

    for t in range(nt):
        for h in range(H):
            seg = lambda j: x_ref[t:t + 1, j * W_MIX + h * DH: j * W_MIX + (h + 1) * DH]
            lb = lb_ref[:, h * DH:(h + 1) * DH]
            q, v, ag = seg(0), seg(2), seg(3)
            f = lb + (1.0 - lb) * _sigmoid(seg(1))
            f_col = col(f)
            s_new = f_col * s_ref[t, h] + (1.0 - f_col) * v
            sout_ref[t, h] = s_new
            o = jnp.sum(col(q) * s_new, axis=0, keepdims=True)
            o = _rms(o, gn_ref[:, h * DH:(h + 1) * DH])
            o_ref[t:t + 1, h * DH:(h + 1) * DH] = o * (ag * _sigmoid(ag))


def _hgrn_step(hg, state, lb, g_hgrn):
    N = hg.shape[0]
    nt = 8
    vec = pl.BlockSpec((1, W_MIX), lambda i: (0, 0))
    st = pl.BlockSpec((nt, H, DH, DH), lambda i: (i, 0, 0, 0))
    return pl.pallas_call(
        _hgrn_step_kernel,
        grid=(N // nt,),
        in_specs=[vec, vec, pl.BlockSpec((nt, 4 * W_MIX), lambda i: (i, 0)), st],
        out_specs=[pl.BlockSpec((nt, W_MIX), lambda i: (i, 0)), st],
        out_shape=[jax.ShapeDtypeStruct((N, W_MIX), F32),
                   jax.ShapeDtypeStruct((N, H, DH, DH), F32)],
        compiler_params=_cp(("arbitrary",)),
        name="hgrn_step",
    )(lb.reshape(1, W_MIX), g_hgrn.reshape(1, W_MIX), hg, state)


def _attn_prompt_kernel(lam_ref, sl_ref, q_ref, k_ref, v_ref, g_ref, o_ref, qt_ref, vt_ref, *chain_refs):
    G, NG, tk = ATT_G, ATT_NG, ATT_TK
    tq = G * NG
    m_refs = [chain_refs[mp * NG:(mp + 1) * NG] for mp in range(2)]
    acc_refs = [chain_refs[(2 + mp) * NG:(3 + mp) * NG] for mp in range(2)]
    slope = sl_ref[pl.program_id(1)] * LOG2E
    lam = lam_ref[0]
    L = q_ref.shape[0]
    q_t = q_ref[...].astype(F32).T * (DK ** -0.5 * LOG2E)
    sub = lax.broadcasted_iota(jnp.int32, (DH, L), 0)
    for mp in range(2):
        qm_t = jnp.where(sub < DK, q_t, 0.0) if mp == 0 else jnp.where(sub >= DK, q_t, 0.0)
        for t in range(L // G):
            qt_ref[mp, t] = qm_t[:, t * G:(t + 1) * G].astype(BF16)
    v_t = v_ref[...].astype(F32).T
    for kb in range(L // tk):
        vt_ref[kb, :DH, :] = v_t[:, kb * tk:(kb + 1) * tk].astype(BF16)
        vt_ref[kb, DH:, :] = jnp.ones((ATT_VR - DH, tk), BF16)
    idiff = lax.broadcasted_iota(jnp.int32, (tk, G), 1) - lax.broadcasted_iota(jnp.int32, (tk, G), 0)
    dmat = idiff.astype(F32) * slope

    def fold(qi, kb, chains):
        k0 = pl.multiple_of(kb * tk, tk)
        kt = k_ref[pl.ds(k0, tk), :]
        scores = [_dot(kt, qt_ref[mp, qi * NG + g]) for mp, g, _ in chains]
        probs = []
        for (mp, g, thresh), s in zip(chains, scores):
            s = s - dmat
            if thresh is not None:
                s = jnp.where(idiff >= thresh, s, NEG_INF)
            c = slope * ((qi * NG + g) * G - k0).astype(F32)
            m = m_refs[mp][g][...]
            m_new = jnp.maximum(m, jnp.max(s, axis=0, keepdims=True) - c)
            m_refs[mp][g][...] = m_new
            probs.append((jnp.exp2(s - (m_new + c)), jnp.exp2(m - m_new)))
        va = vt_ref[kb]
        pvs = [_dot(va, p.astype(BF16)) for p, _ in probs]
        for (mp, g, _), (_, alpha), pv in zip(chains, probs, pvs):
            acc_refs[mp][g][...] = alpha * acc_refs[mp][g][...] + pv

    def q_tile(qi, carry):
        for mp in range(2):
            for g in range(NG):
                m_refs[mp][g][...] = jnp.full((1, G), NEG_INF, F32)
                acc_refs[mp][g][...] = jnp.zeros((ATT_VR, G), F32)

        every = [(mp, g, None) for mp in range(2) for g in range(NG)]
        n_full = qi * (tq // tk)
        lax.fori_loop(0, n_full, lambda kb, c: (fold(qi, kb, every), c)[1], 0)
        for j in range(tq // tk):
            chains = []
            for mp in range(2):
                for g in range(NG):
                    thresh = j * tk - G * g
                    if thresh <= G - 1:
                        chains.append((mp, g, None if thresh <= -(tk - 1) else thresh))
            fold(qi, n_full + j, chains)


        for g in range(NG):
            a0, a1 = acc_refs[0][g][...], acc_refs[1][g][...]
            o_t = a0[:DH] / a0[DH:DH + 1] - lam * (a1[:DH] / a1[DH:DH + 1])
            o_t = o_t * lax.rsqrt(jnp.mean(o_t * o_t, axis=0, keepdims=True) + EPS)
            q0 = pl.multiple_of((qi * NG + g) * G, G)
            o_ref[pl.ds(q0, G), :] = (o_t.T * (g_ref[...] * (1.0 - LAM_INIT))).astype(o_ref.dtype)
        return carry

    lax.fori_loop(0, L // tq, q_tile, 0)


def _attn_prompt(bq, bk, bv, lam, slopes, g_subln, B, L):
    smem = pl.BlockSpec(memory_space=pltpu.SMEM)
    spec = pl.BlockSpec((L, DH), lambda b, h: (b, h))
    return pl.pallas_call(
        _attn_prompt_kernel,
        grid=(B, H),
        in_specs=[smem, smem, spec, spec, spec, pl.BlockSpec((1, DH), lambda b, h: (0, 0))],
        out_specs=spec,
        out_shape=jax.ShapeDtypeStruct((B * L, W_MIX), BF16),
        scratch_shapes=[pltpu.VMEM((2, L // ATT_G, DH, ATT_G), BF16),
                        pltpu.VMEM((L // ATT_TK, ATT_VR, ATT_TK), BF16)]
                       + [pltpu.VMEM((1, ATT_G), F32)] * (2 * ATT_NG)
                       + [pltpu.VMEM((ATT_VR, ATT_G), F32)] * (2 * ATT_NG),
        compiler_params=_cp(("arbitrary", "arbitrary")),
        name="attn_prompt",
    )(lam.reshape(1), slopes, bq, bk, bv, g_subln.reshape(1, DH))


def _attn_step_kernel(pt_ref, lam_ref, sl_ref, q_ref, kn_ref, vn_ref, g_ref, *refs, n_pages, page, n_tok):
    o_ref = refs[2 * n_tok * n_pages]
    PR = page * H
    past = n_pages * page
    lam = lam_ref[0]
    R = 2 * H
    grp = lax.broadcasted_iota(jnp.int32, (R, W_MIX), 1) // DK
    rid = lax.broadcasted_iota(jnp.int32, (R, W_MIX), 0)
    rhead = lax.broadcasted_iota(jnp.int32, (R, DH), 0) // 2
    slope = jnp.concatenate([jnp.full((2, 1), sl_ref[hh], F32) for hh in range(H)], axis=0)
    col = lax.broadcasted_iota(jnp.int32, (R, n_pages * PR), 1)
    own = (col % H) == (lax.broadcasted_iota(jnp.int32, (R, n_pages * PR), 0) // 2)
    bias = jnp.where(own, -slope * (past - col // H).astype(F32), NEG_INF)

    for t in range(n_tok):
        k_refs = refs[t * n_pages:(t + 1) * n_pages]
        v_refs = refs[(n_tok + t) * n_pages:(n_tok + t + 1) * n_pages]
        qbd = jnp.where(grp == rid, jnp.broadcast_to(q_ref[t], (R, W_MIX)), 0.0) * (DK ** -0.5)
        s_new = jnp.sum(qbd * kn_ref[t], axis=1, keepdims=True)
        qx = sum(qbd[:, hh * DH:(hh + 1) * DH] for hh in range(H))
        vnx = sum(jnp.where(rhead == hh, vn_ref[t, :, hh * DH:(hh + 1) * DH], 0.0) for hh in range(H))
        s = jnp.concatenate([_dot_nt(qx, k_refs[j][...]) for j in range(n_pages)], axis=1)
        s = s + bias
        m = jnp.maximum(jnp.max(s, axis=1, keepdims=True), s_new)
        p = jnp.exp(s - m)
        p_new = jnp.exp(s_new - m)
        inv_l = 1.0 / (jnp.sum(p, axis=1, keepdims=True) + p_new)
        p = p * inv_l
        o8 = (p_new * inv_l) * vnx
        for j in range(n_pages):
            o8 = o8 + _dot(p[:, j * PR:(j + 1) * PR], v_refs[j][...])
        for hh in range(H):
            o = o8[2 * hh:2 * hh + 1, :] - lam * o8[2 * hh + 1:2 * hh + 2, :]
            o_ref[t, :, hh * DH:(hh + 1) * DH] = _rms(o, g_ref[...]) * (1.0 - LAM_INIT)


def _attn_step(bq, k_new, v_new, cache_k, cache_v, page_table, lam, slopes, g_subln):
    N = bq.shape[0]
    n_pool, page = cache_k.shape[0], cache_k.shape[1]
    n_pages = page_table.shape[1]
    n_tok = ATTN_STEP_TOKENS
    cache_k = cache_k.reshape(n_pool * page * H, DH)
    cache_v = cache_v.reshape(n_pool * page * H, DH)
    smem = pl.BlockSpec(memory_space=pltpu.SMEM)
    rowspec = pl.BlockSpec((n_tok, 1, W_MIX), lambda b, pt: (b, 0, 0))
    pspec = lambda t, j: pl.BlockSpec((page * H, DH), lambda b, pt: (pt[(b * n_tok + t) * n_pages + j], 0))
    pages = [pspec(t, j) for t in range(n_tok) for j in range(n_pages)]
    grid_spec = pltpu.PrefetchScalarGridSpec(
        num_scalar_prefetch=1,
        grid=(N // n_tok,),
        in_specs=[smem, smem, rowspec, rowspec, rowspec, pl.BlockSpec((1, DH), lambda b, pt: (0, 0))] + pages * 2,
        out_specs=rowspec,
    )
    out = pl.pallas_call(
        functools.partial(_attn_step_kernel, n_pages=n_pages, page=page, n_tok=n_tok),
        grid_spec=grid_spec,
        out_shape=jax.ShapeDtypeStruct((N, 1, W_MIX), F32),
        compiler_params=_cp(("arbitrary",)),
        name="attn_step",
    )(page_table.reshape(-1), lam.reshape(1), slopes, bq.reshape(N, 1, W_MIX), k_new.reshape(N, 1, W_MIX),
      v_new.reshape(N, 1, W_MIX), g_subln.reshape(1, DH), *([cache_k] * (n_tok * n_pages)),
      *([cache_v] * (n_tok * n_pages)))
    return out.reshape(N, W_MIX)


def _merge_kernel(oa_ref, ob_ref, gt_ref, x_ref, g1_ref, sc2_ref, sh2_ref, gpost_ref, gpre_ref,
                  wa_ref, wb_ref, wo_ref, wr_ref, br_ref, cnt0_ref, x1_ref, h3_ref, rt_ref, cnt_ref, run_ref,
                  *, n_part):
    i = pl.program_id(0)
    tm = x_ref.shape[0]
    tp = tm // n_part
    parts = [slice(p * tp, (p + 1) * tp) for p in range(n_part)]
    rows = lambda ref, r: ref[...] if ref.shape[0] == 1 else ref[r, :]

    @pl.when(i == 0)
    def _():
        run_ref[...] = cnt0_ref[...]

    pa = [_dot(oa_ref[r, :], wa_ref[...]) for r in parts]
    pb = [_dot(ob_ref[r, :], wb_ref[...]) for r in parts]
    merged = [_sigmoid(gt_ref[r, :D].astype(F32)) * pa[p] + _sigmoid(gt_ref[r, D:].astype(F32)) * pb[p]
              for p, r in enumerate(parts)]
    z = [_dot(m.astype(BF16), wo_ref[...]) for m in merged]
    h2 = []
    for p, r in enumerate(parts):
        x1 = x_ref[r, :] + rows(g1_ref, r) * _rms(z[p], gpost_ref[...])
        x1_ref[r, :] = x1
        h2.append(_rms(x1, gpre_ref[...]) * (1.0 + rows(sc2_ref, r)) + rows(sh2_ref, r))
        _store_row_tiles(h3_ref, h2[p], p * tp)
    logits = [_dot(h, wr_ref[...]) + br_ref[...] for h in h2]

    lane = lax.broadcasted_iota(jnp.int32, (tp, N_EXPERTS), 1).astype(F32)
    sels, idxs, gates, chosen = [], [], [], []
    for p in range(n_part):
        work = logits[p]
        sel_p, idx_p, val_p = [], [], []
        for _ in range(TOP_K):
            mx = jnp.max(work, axis=1, keepdims=True)
            idx = jnp.min(jnp.where(work == mx, lane, float(N_EXPERTS)), axis=1, keepdims=True)
            sel = lane == idx
            work = jnp.where(sel, -jnp.inf, work)
            sel_p.append(sel)
            idx_p.append(idx)
            val_p.append(mx)
        ex = [jnp.exp(v - val_p[0]) for v in val_p]
        inv_den = 1.0 / (ex[0] + ex[1] + ex[2] + ex[3])
        sels.append(sel_p)
        idxs.append(idx_p)
        gates.append([e * inv_den for e in ex])
        chosen.append(sum(s.astype(F32) for s in sel_p))

    chosen_all = jnp.concatenate(chosen, axis=0) if n_part > 1 else chosen[0]
    before = (lax.broadcasted_iota(jnp.int32, (tm, tm), 0) > lax.broadcasted_iota(jnp.int32, (tm, tm), 1))
    prior = _dot(before.astype(BF16), chosen_all.astype(BF16)) + run_ref[...]
    run_ref[...] = run_ref[...] + jnp.sum(chosen_all, axis=0, keepdims=True)
    cnt_ref[...] = run_ref[...]

    rlane = lax.broadcasted_iota(jnp.int32, (tp, LANES), 1)
    for p, r in enumerate(parts):
        rt = jnp.zeros((tp, LANES), F32)
        for kk in range(TOP_K):
            rank = jnp.sum(jnp.where(sels[p][kk], prior[r, :], 0.0), axis=1, keepdims=True)
            rt = jnp.where(rlane == RT_E + kk, idxs[p][kk], rt)
            rt = jnp.where(rlane == RT_RANK + kk, rank, rt)
            rt = jnp.where(rlane == RT_GATE + kk, gates[p][kk], rt)
        rt_ref[r, :] = rt


def _merge(oa, ob, gt, x2d, m3d, tiles_per_mod, g_post, g_pre, wa, wb, wo, wr, br, cnt0, tm):
    R = x2d.shape[0]
    r = m3d.shape[1]
    mod = lambda col: pl.BlockSpec((None, r, D), lambda i: (i // tiles_per_mod, 0, col))
    row = lambda w: pl.BlockSpec((tm, w), lambda i: (i, 0))
    full = lambda a, b: pl.BlockSpec((a, b), lambda i: (0, 0))
    return pl.pallas_call(
        functools.partial(_merge_kernel, n_part=4 if tm >= 512 else 1),
        grid=(R // tm,),
        in_specs=[row(W_MIX), row(W_MIX), row(2 * D), row(D), mod(2), mod(4), mod(3),
                  full(1, D), full(1, D), full(W_MIX, D), full(W_MIX, D), full(D, D),
                  full(D, N_EXPERTS), full(1, N_EXPERTS), full(1, N_EXPERTS)],
        out_specs=[row(D), pl.BlockSpec((tm * SUB, LANES), lambda i: (i, 0)), row(LANES), full(1, N_EXPERTS)],
        out_shape=[jax.ShapeDtypeStruct((R, D), F32),
                   jax.ShapeDtypeStruct((R * SUB, LANES), F32),
                   jax.ShapeDtypeStruct((R, LANES), F32),
                   jax.ShapeDtypeStruct((1, N_EXPERTS), F32)],
        scratch_shapes=[pltpu.VMEM((1, N_EXPERTS), F32)],
        compiler_params=_cp(("arbitrary",)),
        name="merge",
    )(oa, ob, gt, x2d, m3d, m3d, m3d, g_post.reshape(1, D), g_pre.reshape(1, D), wa, wb, wo, wr,
      br.reshape(1, N_EXPERTS), cnt0)


def _row_copy(src, s_row, dst, d_row, sem):
    return pltpu.make_async_copy(src.at[pl.ds(pl.multiple_of(s_row, SUB), SUB)],
                                 dst.at[pl.ds(pl.multiple_of(d_row, SUB), SUB)], sem)


def _dispatch_kernel(lo_ref, hi_ref, nu_ref, da_ref, db_ref, ha_ref, hb_ref, x_hbm, zero_ref, sem):
    i = pl.program_id(0)
    last = pl.num_programs(0) - 1
    unroll = 8

    def scatter_rows(d_ref, h_ref):
        n = h_ref.shape[0] // SUB

        def issue(j, carry):
            for u in range(unroll):
                t = j * unroll + u
                for kk in range(TOP_K):
                    _row_copy(h_ref, t * SUB, x_hbm, d_ref[t * TOP_K + kk], sem).start(priority=kk % 2)
            return carry

        lax.fori_loop(0, n // unroll, issue, 0)
        for _ in range(TOP_K):
            pltpu.make_async_copy(h_ref, x_hbm.at[pl.ds(0, n * SUB)], sem).wait()

    @pl.when(i < last)
    def _():
        scatter_rows(da_ref, ha_ref)

    @pl.when(i == last)
    def _():
        scatter_rows(db_ref, hb_ref)
        zero_ref[...] = jnp.zeros_like(zero_ref)

        def per_expert(e, carry):
            zrow = lambda r: _row_copy(zero_ref, 0, x_hbm, r * SUB, sem)
            lax.fori_loop(lo_ref[e], hi_ref[e], lambda r, c: (zrow(r).start(), c)[1], 0)
            lax.fori_loop(lo_ref[e], hi_ref[e], lambda r, c: (zrow(r).wait(), c)[1], 0)
            return carry

        lax.fori_loop(0, N_EXPERTS, per_expert, 0)

        blk = zero_ref.shape[0]
        zblk = lambda b: pltpu.make_async_copy(zero_ref, x_hbm.at[pl.ds(pl.multiple_of(b * blk, blk), blk)], sem)
        lax.fori_loop(nu_ref[0], x_hbm.shape[0] // blk, lambda b, c: (zblk(b).start(), c)[1], 0)
        lax.fori_loop(nu_ref[0], x_hbm.shape[0] // blk, lambda b, c: (zblk(b).wait(), c)[1], 0)


def _dispatch(h3_a, h3_b, dest8_a, dest8_b, pad_lo, pad_hi, n_used, n_rows):
    tm = DISPATCH_ROWS
    n_a = h3_a.shape[0] // (tm * SUB)
    clamp = lambda i: jnp.minimum(i, n_a - 1)
    grid_spec = pltpu.PrefetchScalarGridSpec(
        num_scalar_prefetch=3,
        grid=(n_a + 1,),
        in_specs=[pl.BlockSpec((tm * TOP_K,), lambda i, *_: (clamp(i),), memory_space=pltpu.SMEM),
                  pl.BlockSpec(dest8_b.shape, lambda i, *_: (0,), memory_space=pltpu.SMEM),
                  pl.BlockSpec((tm * SUB, LANES), lambda i, *_: (clamp(i), 0)),
                  pl.BlockSpec(h3_b.shape, lambda i, *_: (0, 0))],
        out_specs=pl.BlockSpec(memory_space=pl.ANY),
        scratch_shapes=[pltpu.VMEM((MOE_BLOCK * SUB, LANES), F32), pltpu.SemaphoreType.DMA],
    )
    return pl.pallas_call(
        _dispatch_kernel,
        grid_spec=grid_spec,
        out_shape=jax.ShapeDtypeStruct((n_rows * SUB, LANES), F32),
        compiler_params=_cp(("arbitrary",)),
        name="dispatch",
    )(pad_lo, pad_hi, n_used, dest8_a, dest8_b, h3_a, h3_b)


def _expert_kernel(be_ref, nu_ref, nxt_ref, x_ref, wgu_hbm, bgu_ref, wd_hbm, bd_ref, y_ref,
                   wgu_f, wd_f, wgu_b, wd_b, slot_ref, sem_gu, sem_d):
    i = pl.program_id(0)

    def fetch(e, slot):
        return (pltpu.make_async_copy(wgu_hbm.at[e], wgu_f.at[slot], sem_gu.at[slot]),
                pltpu.make_async_copy(wd_hbm.at[e], wd_f.at[slot], sem_d.at[slot]))

    @pl.when(i == 0)
    def _():
        slot_ref[0] = 0
        for cp in fetch(be_ref[0], 0):
            cp.start()

    @pl.when(i < nu_ref[0])
    def _():
        @pl.when((i == 0) | (be_ref[i] != be_ref[jnp.maximum(i - 1, 0)]))
        def _():
            slot = slot_ref[0]
            for cp in fetch(be_ref[i], slot):
                cp.wait()
            wgu_b[...] = wgu_f[slot].astype(BF16)
            wd_b[...] = wd_f[slot].astype(BF16)

            @pl.when(nxt_ref[i] >= 0)
            def _():
                for cp in fetch(nxt_ref[i], 1 - slot):
                    cp.start()
            slot_ref[0] = 1 - slot

        gu = _dot(_load_row_tiles(x_ref).astype(BF16), wgu_b[...]) + bgu_ref[...]
        glu = jnp.minimum(gu[:, :D], SWIGLU_LIMIT)
        lin = jnp.clip(gu[:, D:], -SWIGLU_LIMIT, SWIGLU_LIMIT)
        act = (lin + 1.0) * glu * _sigmoid(SWIGLU_ALPHA * glu)
        _store_row_tiles(y_ref, _dot(act.astype(BF16), wd_b[...]) + bd_ref[...])

    @pl.when(i >= nu_ref[0])
    def _():
        y_ref[...] = jnp.zeros_like(y_ref)


def _experts(x3, block_expert, n_used, next_expert, w_gate_up, b_gate_up, w_down, b_down):
    bm = MOE_BLOCK
    n_blocks = x3.shape[0] // (bm * SUB)
    used = lambda i, nu: jnp.minimum(i, nu[0] - 1)
    anyspec = pl.BlockSpec(memory_space=pl.ANY)
    grid_spec = pltpu.PrefetchScalarGridSpec(
        num_scalar_prefetch=3,
        grid=(n_blocks,),
        in_specs=[pl.BlockSpec((bm * SUB, LANES), lambda i, be, nu, nx: (used(i, nu), 0)),
                  anyspec,
                  pl.BlockSpec((None, 1, 2 * D), lambda i, be, nu, nx: (be[used(i, nu)], 0, 0)),
                  anyspec,
                  pl.BlockSpec((None, 1, D), lambda i, be, nu, nx: (be[used(i, nu)], 0, 0))],
        out_specs=pl.BlockSpec((bm * SUB, LANES), lambda i, be, nu, nx: (i, 0)),
        scratch_shapes=[pltpu.VMEM((2, D, 2 * D), F32), pltpu.VMEM((2, D, D), F32),
                        pltpu.VMEM((D, 2 * D), BF16), pltpu.VMEM((D, D), BF16),
                        pltpu.SMEM((1,), jnp.int32),
                        pltpu.SemaphoreType.DMA((2,)), pltpu.SemaphoreType.DMA((2,))],
    )
    return pl.pallas_call(
        _expert_kernel,
        grid_spec=grid_spec,
        out_shape=jax.ShapeDtypeStruct(x3.shape, F32),
        compiler_params=_cp(("arbitrary",)),
        name="experts",
    )(block_expert, n_used, next_expert, x3, w_gate_up, b_gate_up.reshape(N_EXPERTS, 1, 2 * D), w_down,
      b_down.reshape(N_EXPERTS, 1, D))


def _final_kernel(dcur_ref, dnext_ref, x1_ref, g2_ref, gpost_ref, rt_ref, y_hbm, o_ref, ybuf, sem):
    i = pl.program_id(0)
    n = pl.num_programs(0)
    tm = x1_ref.shape[0]
    slot = i % 2
    unroll = 8

    def issue(d_ref, s):
        def body(j, carry):
            for u in range(unroll):
                t = j * unroll + u
                for kk in range(TOP_K):
                    _row_copy(y_hbm, d_ref[t * TOP_K + kk], ybuf.at[s, kk], t * SUB, sem.at[s]).start(priority=kk % 2)
            return carry
        lax.fori_loop(0, tm // unroll, body, 0)

    @pl.when(i == 0)
    def _():
        issue(dcur_ref, 0)

    @pl.when(i + 1 < n)
    def _():
        issue(dnext_ref, 1 - slot)

    for kk in range(TOP_K):
        pltpu.make_async_copy(y_hbm.at[pl.ds(0, tm * SUB)], ybuf.at[slot, kk], sem.at[slot]).wait()

    y = jnp.zeros((tm, D), F32)
    for kk in range(TOP_K):
        y = y + rt_ref[:, RT_GATE + kk:RT_GATE + kk + 1] * _load_row_tiles(ybuf.at[slot, kk])
    o_ref[...] = x1_ref[...] + g2_ref[...] * _rms(y, gpost_ref[...])


def _final(x1, m3d, tiles_per_mod, g_post, route, dest_flat, y3, row0, tm):
    R = x1.shape[0]
    r = m3d.shape[1]
    off = row0 // tm
    nsteps = R // tm
    row = pl.BlockSpec((tm, D), lambda i: (i, 0))
    dspec = lambda nxt: pl.BlockSpec((tm * TOP_K,), lambda i: (jnp.minimum(i + nxt, nsteps - 1) + off,),
                                     memory_space=pltpu.SMEM)
    return pl.pallas_call(
        _final_kernel,
        grid=(nsteps,),
        in_specs=[dspec(0), dspec(1), row,
                  pl.BlockSpec((None, r, D), lambda i: (i // tiles_per_mod, 0, 5)),
                  pl.BlockSpec((1, D), lambda i: (0, 0)),
                  pl.BlockSpec((tm, LANES), lambda i: (i, 0)),
                  pl.BlockSpec(memory_space=pl.ANY)],
        out_specs=row,
        out_shape=jax.ShapeDtypeStruct((R, D), F32),
        scratch_shapes=[pltpu.VMEM((2, TOP_K, tm * SUB, LANES), F32), pltpu.SemaphoreType.DMA((2,))],
        compiler_params=_cp(("arbitrary",)),
        name="final",
    )(dest_flat, dest_flat, x1, m3d, g_post.reshape(1, D), route, y3)


def kernel(x_prompt, x_sample, cache_k, cache_v, state_hgrn, page_table, c_prompt, c_sample, w_mod, b_mod, g_pre_mix, g_post_mix, g_pre_ffn, g_post_ffn, w_in, hgrn_lb, g_hgrn, lam_q1, lam_k1, lam_q2, lam_k2, g_subln, w_proj_a, w_proj_b, w_out, w_router, b_router, w_gate_up, b_gate_up, w_down, b_down):
    B, L, _ = x_prompt.shape
    NS = x_sample.shape[0]
    T = B * L
    T_all = T + NS
    l = 0

    lb = jnp.cumsum(jax.nn.softmax(hgrn_lb.astype(F32), axis=0), axis=0)[l]
    lam = (jnp.exp(jnp.sum(lam_q1[l] * lam_k1[l])) - jnp.exp(jnp.sum(lam_q2[l] * lam_k2[l])) + LAM_INIT).astype(F32)
    slopes = jnp.exp2(-8.0 * jnp.arange(1, H + 1, dtype=F32) / H)

    m_all = _modulation(jnp.concatenate([c_prompt, c_sample], axis=0), w_mod[l], b_mod[l])
    m_p = m_all[:B].reshape(B, 1, 6 * D)
    m_s = m_all[B:].reshape(1, NS, 6 * D)

    w_in_b = w_in[l].astype(BF16)
    tm = 256
    tpm = L // tm
    xp = x_prompt.reshape(T, D)
    xs = x_sample.reshape(NS, D)
    merge_w = (g_post_mix[l], g_pre_ffn[l], w_proj_a[l].astype(BF16), w_proj_b[l].astype(BF16),
               w_out[l].astype(BF16), w_router[l], b_router[l])

    aq_p, af_p, ai_p, ag_p, bq_p, k_p, v_p, gt_p, k4_p, v4_p = _inproj(xp, m_p, L // INPROJ_ROWS, g_pre_mix[l], w_in_b,
                                                                       INPROJ_ROWS)
    seq = lambda a: a.reshape(B, L, W_MIX)
    oa_p, s_p = _hgrn_prompt(seq(aq_p), seq(af_p), seq(ai_p), seq(ag_p), lb, g_hgrn[l])
    ob_p = _attn_prompt(bq_p, k_p, v_p, lam, slopes, g_subln[l], B, L)
    x1_p, h3_p, rt_p, cnt = _merge(oa_p.reshape(T, W_MIX), ob_p, gt_p, xp, m_p, L // MERGE_ROWS, *merge_w,
                                   jnp.zeros((1, N_EXPERTS), F32), MERGE_ROWS)

    aq_s, af_s, ai_s, ag_s, bq_s, _, _, gt_s, k4_s, v4_s = _inproj(xs, m_s, 1, g_pre_mix[l], w_in_b, NS)
    hg_s = jnp.concatenate([a.astype(F32) for a in (aq_s, af_s, ai_s, ag_s)], axis=1)
    oa_s, s_s = _hgrn_step(hg_s, state_hgrn[l], lb, g_hgrn[l])
    ob_s = _attn_step(bq_s.astype(F32), k4_s.reshape(NS, W_MIX), v4_s.reshape(NS, W_MIX), cache_k[l], cache_v[l],
                      page_table, lam, slopes, g_subln[l])
    x1_s, h3_s, rt_s, cnt = _merge(oa_s.astype(BF16), ob_s.astype(BF16), gt_s, xs, m_s, 1, *merge_w, cnt, NS)

    bm = MOE_BLOCK
    counts = cnt.reshape(N_EXPERTS).astype(jnp.int32)
    padded = (counts + bm - 1) // bm * bm
    pad_end = jnp.cumsum(padded)
    pad_start = pad_end - padded
    n_blocks = -(-T_all * TOP_K // bm) + N_EXPERTS
    route = jnp.concatenate([rt_p[:, :RT_GATE], rt_s[:, :RT_GATE]], axis=0)
    e_idx = route[:, RT_E:RT_E + TOP_K].astype(jnp.int32)
    rank = route[:, RT_RANK:RT_RANK + TOP_K].astype(jnp.int32)
    e_iota = jnp.arange(N_EXPERTS, dtype=jnp.int32)
    dest = rank + jnp.sum(jnp.where(e_idx[..., None] == e_iota, pad_start, 0), axis=-1)
    dest_flat = dest.reshape(-1) * SUB
    block_expert = jnp.minimum(
        jnp.sum((jnp.arange(n_blocks, dtype=jnp.int32)[:, None] * bm >= pad_end[None, :]).astype(jnp.int32), axis=1),
        N_EXPERTS - 1)
    n_used = pad_end[-1:] // bm
    later = lax.cummin(jnp.where(counts > 0, jnp.arange(N_EXPERTS, dtype=jnp.int32), N_EXPERTS), reverse=True)
    later = jnp.concatenate([later[1:], jnp.full((1,), N_EXPERTS, jnp.int32)])
    next_expert = jnp.where(later < N_EXPERTS, later, -1)[block_expert]

    x3 = _dispatch(h3_p, h3_s, dest_flat[:T * TOP_K], dest_flat[T * TOP_K:], pad_start + counts, pad_end, n_used,
                   n_blocks * bm)
    y3 = _experts(x3, block_expert, n_used, next_expert, w_gate_up[l], b_gate_up[l], w_down[l], b_down[l])
    y_p = _final(x1_p, m_p, tpm, g_post_ffn[l], rt_p, dest_flat, y3, 0, tm)
    y_s = _final(x1_s, m_s, 1, g_post_ffn[l], rt_s, dest_flat, y3, T, NS)

    return (y_p.reshape(B, L, D), y_s.reshape(NS, 1, D),
            k4_p.reshape(1, B, L, H, DH), v4_p.reshape(1, B, L, H, DH), s_p.reshape(1, B, H, DH, DH),
            k4_s.reshape(1, NS, 1, H, DH), v4_s.reshape(1, NS, 1, H, DH), s_s.reshape(1, NS, H, DH, DH))
```

```python
import functools
import math

import jax
import jax.numpy as jnp
from jax import lax
from jax.experimental import pallas as pl
from jax.experimental.pallas import tpu as pltpu

F32 = jnp.float32
BF16 = jnp.bfloat16

D = 1024
LANES = 128
SUB = D // LANES
H = 4
DH = 128
DK = 64
W_MIX = H * DH
HGRN_CHUNK = 64
N_EXPERTS = 32
TOP_K = 4
SWIGLU_LIMIT = 7.0
SWIGLU_ALPHA = 1.702
EPS = 1e-6
NEG_INF = -1e30
LOG2E = 1.4426950408889634
LAM_INIT = 0.8 - 0.6 * math.exp(-0.3 * 0)
C_HG = (0, 4 * W_MIX)
C_BQ = (4 * W_MIX, 5 * W_MIX)
C_BK = (5 * W_MIX, 6 * W_MIX)
C_BV = (6 * W_MIX, 7 * W_MIX)
C_GT = (7 * W_MIX, 7 * W_MIX + 2 * D)
D_IN = 7 * W_MIX + 2 * D
RT_E, RT_RANK, RT_GATE = 0, TOP_K, 2 * TOP_K

VMEM_LIMIT = 56 * 1024 * 1024
MOE_BLOCK = 256
DISPATCH_ROWS = 512
MERGE_ROWS = 512
INPROJ_ROWS = 512
ATTN_STEP_TOKENS = 2
ATT_G = 256
ATT_NG = 8
ATT_TK = 256
ATT_VR = DH + 16


def _cp(sem, vmem=VMEM_LIMIT):
    return pltpu.CompilerParams(dimension_semantics=sem, vmem_limit_bytes=vmem)


def _sigmoid(x):
    return 1.0 / (1.0 + jnp.exp(-x))


def _rms(x, g):
    return x * lax.rsqrt(jnp.mean(x * x, axis=-1, keepdims=True) + EPS) * g


def _dot(a, b):
    return jnp.dot(a, b, preferred_element_type=F32)


def _dot_nt(a, b):
    return lax.dot_general(a, b, (((1,), (1,)), ((), ())), preferred_element_type=F32)


def _dot_tn(a, b):
    return lax.dot_general(a, b, (((0,), (0,)), ((), ())), preferred_element_type=F32)


def _store_row_tiles(ref2, x2d, row0=0):
    n = x2d.shape[0]
    for j in range(SUB):
        ref2[pl.ds(row0 * SUB + j, n, stride=SUB), :] = x2d[:, j * LANES:(j + 1) * LANES]


def _load_row_tiles(ref2):
    n = ref2.shape[0] // SUB
    return jnp.concatenate([ref2[pl.ds(j, n, stride=SUB), :] for j in range(SUB)], axis=1)


def _mod_kernel(c_ref, w_ref, b_ref, o_ref):
    c = c_ref[...]
    o_ref[...] = _dot(c * _sigmoid(c), w_ref[...]) + b_ref[...]


def _modulation(c_all, w_mod, b_mod):
    n = c_all.shape[0]
    tn = 1024
    return pl.pallas_call(
        _mod_kernel,
        grid=(6 * D // tn,),
        in_specs=[pl.BlockSpec((n, D), lambda j: (0, 0)),
                  pl.BlockSpec((D, tn), lambda j: (0, j)),
                  pl.BlockSpec((1, tn), lambda j: (0, j))],
        out_specs=pl.BlockSpec((n, tn), lambda j: (0, j)),
        out_shape=jax.ShapeDtypeStruct((n, 6 * D), F32),
        compiler_params=_cp(("arbitrary",)),
        name="modulation",
    )(c_all, w_mod, b_mod.reshape(1, 6 * D))


def _inproj_kernel(x_ref, sc_ref, sh_ref, g_ref, w_ref,
                   aq_ref, af_ref, ai_ref, ag_ref, bq_ref, k_ref, v_ref, gt_ref, k4_ref, v4_ref):
    h = _rms(x_ref[...], g_ref[...]) * (1.0 + sc_ref[...]) + sh_ref[...]
    hb = h.astype(BF16)
    mm = lambda lo, hi: _dot(hb, w_ref[:, lo:hi])
    for j, o_ref in enumerate((aq_ref, af_ref, ai_ref, ag_ref)):
        o_ref[...] = mm(C_HG[0] + j * W_MIX, C_HG[0] + (j + 1) * W_MIX).astype(o_ref.dtype)
    bq_ref[...] = mm(*C_BQ).astype(BF16)
    gt_ref[...] = mm(*C_GT).astype(BF16)
    for c, o_ref, o4_ref in ((C_BK, k_ref, k4_ref), (C_BV, v_ref, v4_ref)):
        r = mm(*c)
        o_ref[...] = r.astype(BF16)
        o4 = o4_ref.reshape(r.shape[0] * H, DH)
        for hh in range(H):
            o4[pl.ds(hh, r.shape[0], stride=H), :] = r[:, hh * DH:(hh + 1) * DH]


def _inproj(x2d, m3d, tiles_per_mod, g_pre, w_in_bf16, tm):
    R = x2d.shape[0]
    r = m3d.shape[1]
    mod = lambda col: pl.BlockSpec((None, r, D), lambda i: (i // tiles_per_mod, 0, col))
    row = lambda w: pl.BlockSpec((tm, w), lambda i: (i, 0))
    row4 = pl.BlockSpec((tm, H, DH), lambda i: (i, 0, 0))
    outs = [(W_MIX, BF16), (W_MIX, F32), (W_MIX, BF16), (W_MIX, BF16), (W_MIX, BF16), (W_MIX, BF16), (W_MIX, BF16),
            (2 * D, BF16)]
    kv4 = jax.ShapeDtypeStruct((R, H, DH), F32)
    return pl.pallas_call(
        _inproj_kernel,
        grid=(R // tm,),
        in_specs=[row(D), mod(1), mod(0),
                  pl.BlockSpec((1, D), lambda i: (0, 0)),
                  pl.BlockSpec((D, D_IN), lambda i: (0, 0))],
        out_specs=[row(w) for w, _ in outs] + [row4, row4],
        out_shape=[jax.ShapeDtypeStruct((R, w), dt) for w, dt in outs] + [kv4, kv4],
        compiler_params=_cp(("arbitrary",)),
        name="inproj",
    )(x2d, m3d, m3d, g_pre.reshape(1, D), w_in_bf16)


def _hgrn_prompt_kernel(lb_ref, gn_ref, q_ref, f_ref, i_ref, g_ref, o_ref, sout_ref, st_ref):
    c = pl.program_id(0)
    nb = q_ref.shape[0]
    C = HGRN_CHUNK

    @pl.when(c == 0)
    def _():
        st_ref[...] = jnp.zeros_like(st_ref)

    row = lax.broadcasted_iota(jnp.int32, (C, DH), 0)
    causal = lax.broadcasted_iota(jnp.int32, (C, C), 0) >= lax.broadcasted_iota(jnp.int32, (C, C), 1)

    def per_batch_group(bp, carry):
        ch = [(4 * bp + d, h, slice(h * DH, (h + 1) * DH)) for d in range(4) for h in range(H)]
        qd, kd, k2, v, eg = [], [], [], [], []
        for b, h, hs in ch:
            lb = lb_ref[:, hs]
            f = lb + (1.0 - lb) * _sigmoid(f_ref[b, :, hs].astype(F32))
            k = 1.0 - f
            gc = jnp.log(f)
            for s in (1, 2, 4, 8, 16, 32):
                gc = gc + jnp.where(row >= s, pltpu.roll(gc, s, axis=0), 0.0)
            g_last = gc[C - 1:C, :]
            qd.append(q_ref[b, :, hs].astype(F32) * jnp.exp(gc))
            kd.append(k * jnp.exp(-gc))
            k2.append(k * jnp.exp(g_last - gc))
            eg.append(jnp.exp(g_last))
            v.append(i_ref[b, :, hs].astype(F32))
        n = len(ch)
        st = [st_ref[b, h] for b, h, _ in ch]
        a = [jnp.where(causal, _dot_nt(qd[i], kd[i]), 0.0) for i in range(n)]
        o_inter = [_dot_nt(qd[i], st[i]) for i in range(n)]
        kv_t = [_dot_tn(v[i], k2[i]) for i in range(n)]
        o_intra = [_dot(a[i], v[i]) for i in range(n)]
        for i, (b, h, hs) in enumerate(ch):
            st_ref[b, h] = eg[i] * st[i] + kv_t[i]
            o = _rms(o_intra[i] + o_inter[i], gn_ref[:, hs])
            ag = g_ref[b, :, hs].astype(F32)
            o_ref[b, :, hs] = (o * (ag * _sigmoid(ag))).astype(o_ref.dtype)
        return carry

    lax.fori_loop(0, nb // 4, per_batch_group, 0)

    @pl.when(c == pl.num_programs(0) - 1)
    def _():
        def fin(b, carry):
            for h in range(H):
                sout_ref[b, h] = st_ref[b, h].T
            return carry
        lax.fori_loop(0, nb, fin, 0)


def _hgrn_prompt(aq, af, ai, ag, lb, g_hgrn):
    B, L, _ = aq.shape
    C = HGRN_CHUNK
    blk = pl.BlockSpec((B, C, W_MIX), lambda c: (0, c, 0))
    vec = pl.BlockSpec((1, W_MIX), lambda c: (0, 0))
    return pl.pallas_call(
        _hgrn_prompt_kernel,
        grid=(L // C,),
        in_specs=[vec, vec, blk, blk, blk, blk],
        out_specs=[blk, pl.BlockSpec((B, H, DH, DH), lambda c: (0, 0, 0, 0))],
        out_shape=[jax.ShapeDtypeStruct((B, L, W_MIX), BF16),
                   jax.ShapeDtypeStruct((B, H, DH, DH), F32)],
        scratch_shapes=[pltpu.VMEM((B, H, DH, DH), F32)],
        compiler_params=_cp(("arbitrary",)),
        name="hgrn_prompt",
    )(lb.reshape(1, W_MIX), g_hgrn.reshape(1, W_MIX), aq, af, ai, ag)


def _hgrn_step_kernel(lb_ref, gn_ref, x_ref, s_ref, o_ref, sout_ref):
    nt = x_ref.shape[0]
    ones = jnp.ones((8, DH), F32)
    pad = jnp.zeros((6, DH), F32)

    def col(r):
        hi = r.astype(BF16).astype(F32)
        return _dot_tn(jnp.concatenate([hi, r - hi, pad], axis=0), ones)

    for t in range(nt):
        for h in range(H):
            seg = lambda j: x_ref[t:t + 1, j * W_MIX + h * DH: j * W_MIX + (h + 1) * DH]
            lb = lb_ref[:, h * DH:(h + 1) * DH]
            q, v, ag = seg(0), seg(2), seg(3)
            f = lb + (1.0 - lb) * _sigmoid(seg(1))
            f_col = col(f)
            s_new = f_col * s_ref[t, h] + (1.0 - f_col) * v
            sout_ref[t, h] = s_new
            o = jnp.sum(col(q) * s_new, axis=0, keepdims=True)
            o = _rms(o, gn_ref[:, h * DH:(h + 1) * DH])
            o_ref[t:t + 1, h * DH:(h + 1) * DH] = o * (ag * _sigmoid(ag))


def _hgrn_step(hg, state, lb, g_hgrn):
    N = hg.shape[0]
    nt = 8
    vec = pl.BlockSpec((1, W_MIX), lambda i: (0, 0))
    st = pl.BlockSpec((nt, H, DH, DH), lambda i: (i, 0, 0, 0))
    return pl.pallas_call(
        _hgrn_step_kernel,
        grid=(N // nt,),
        in_specs=[vec, vec, pl.BlockSpec((nt, 4 * W_MIX), lambda i: (i, 0)), st],
        out_specs=[pl.BlockSpec((nt, W_MIX), lambda i: (i, 0)), st],
        out_shape=[jax.ShapeDtypeStruct((N, W_MIX), F32),
                   jax.ShapeDtypeStruct((N, H, DH, DH), F32)],
        compiler_params=_cp(("arbitrary",)),
        name="hgrn_step",
    )(lb.reshape(1, W_MIX), g_hgrn.reshape(1, W_MIX), hg, state)


def _attn_prompt_kernel(lam_ref, sl_ref, q_ref, k_ref, v_ref, g_ref, o_ref, qt_ref, vt_ref, *chain_refs):
    G, NG, tk = ATT_G, ATT_NG, ATT_TK
    tq = G * NG
    m_refs = [chain_refs[mp * NG:(mp + 1) * NG] for mp in range(2)]
    acc_refs = [chain_refs[(2 + mp) * NG:(3 + mp) * NG] for mp in range(2)]
    slope = sl_ref[pl.program_id(1)] * LOG2E
    lam = lam_ref[0]
    L = q_ref.shape[0]
    q_t = q_ref[...].astype(F32).T * (DK ** -0.5 * LOG2E)
    sub = lax.broadcasted_iota(jnp.int32, (DH, L), 0)
    for mp in range(2):
        qm_t = jnp.where(sub < DK, q_t, 0.0) if mp == 0 else jnp.where(sub >= DK, q_t, 0.0)
        for t in range(L // G):
            qt_ref[mp, t] = qm_t[:, t * G:(t + 1) * G].astype(BF16)
    v_t = v_ref[...].astype(F32).T
    for kb in range(L // tk):
        vt_ref[kb, :DH, :] = v_t[:, kb * tk:(kb + 1) * tk].astype(BF16)
        vt_ref[kb, DH:, :] = jnp.ones((ATT_VR - DH, tk), BF16)
    idiff = lax.broadcasted_iota(jnp.int32, (tk, G), 1) - lax.broadcasted_iota(jnp.int32, (tk, G), 0)
    dmat = idiff.astype(F32) * slope

    def fold(qi, kb, chains):
        k0 = pl.multiple_of(kb * tk, tk)
        kt = k_ref[pl.ds(k0, tk), :]
        scores = [_dot(kt, qt_ref[mp, qi * NG + g]) for mp, g, _ in chains]
        probs = []
        for (mp, g, thresh), s in zip(chains, scores):
            s = s - dmat
            if thresh is not None:
                s = jnp.where(idiff >= thresh, s, NEG_INF)
            c = slope * ((qi * NG + g) * G - k0).astype(F32)
            m = m_refs[mp][g][...]
            m_new = jnp.maximum(m, jnp.max(s, axis=0, keepdims=True) - c)
            m_refs[mp][g][...] = m_new
            probs.append((jnp.exp2(s - (m_new + c)), jnp.exp2(m - m_new)))
        va = vt_ref[kb]
        pvs = [_dot(va, p.astype(BF16)) for p, _ in probs]
        for (mp, g, _), (_, alpha), pv in zip(chains, probs, pvs):
            acc_refs[mp][g][...] = alpha * acc_refs[mp][g][...] + pv

    def q_tile(qi, carry):
        for mp in range(2):
            for g in range(NG):
                m_refs[mp][g][...] = jnp.full((1, G), NEG_INF, F32)
                acc_refs[mp][g][...] = jnp.zeros((ATT_VR, G), F32)

        every = [(mp, g, None) for mp in range(2) for g in range(NG)]
        n_full = qi * (tq // tk)
        lax.fori_loop(0, n_full, lambda kb, c: (fold(qi, kb, every), c)[1], 0)
        for j in range(tq // tk):
            chains = []
            for mp in range(2):
                for g in range(NG):
                    thresh = j * tk - G * g
                    if thresh <= G - 1:
                        chains.append((mp, g, None if thresh <= -(tk - 1) else thresh))
            fold(qi, n_full + j, chains)


        for g in range(NG):
            a0, a1 = acc_refs[0][g][...], acc_refs[1][g][...]
            o_t = a0[:DH] / a0[DH:DH + 1] - lam * (a1[:DH] / a1[DH:DH + 1])
            o_t = o_t * lax.rsqrt(jnp.mean(o_t * o_t, axis=0, keepdims=True) + EPS)
            q0 = pl.multiple_of((qi * NG + g) * G, G)
            o_ref[pl.ds(q0, G), :] = (o_t.T * (g_ref[...] * (1.0 - LAM_INIT))).astype(o_ref.dtype)
        return carry

    lax.fori_loop(0, L // tq, q_tile, 0)


def _attn_prompt(bq, bk, bv, lam, slopes, g_subln, B, L):
    smem = pl.BlockSpec(memory_space=pltpu.SMEM)
    spec = pl.BlockSpec((L, DH), lambda b, h: (b, h))
    return pl.pallas_call(
        _attn_prompt_kernel,
        grid=(B, H),
        in_specs=[smem, smem, spec, spec, spec, pl.BlockSpec((1, DH), lambda b, h: (0, 0))],
        out_specs=spec,
        out_shape=jax.ShapeDtypeStruct((B * L, W_MIX), BF16),
        scratch_shapes=[pltpu.VMEM((2, L // ATT_G, DH, ATT_G), BF16),
                        pltpu.VMEM((L // ATT_TK, ATT_VR, ATT_TK), BF16)]
                       + [pltpu.VMEM((1, ATT_G), F32)] * (2 * ATT_NG)
                       + [pltpu.VMEM((ATT_VR, ATT_G), F32)] * (2 * ATT_NG),
        compiler_params=_cp(("arbitrary", "arbitrary")),
        name="attn_prompt",
    )(lam.reshape(1), slopes, bq, bk, bv, g_subln.reshape(1, DH))


def _attn_step_kernel(pt_ref, lam_ref, sl_ref, q_ref, kn_ref, vn_ref, g_ref, *refs, n_pages, page, n_tok):
    o_ref = refs[2 * n_tok * n_pages]
    PR = page * H
    past = n_pages * page
    lam = lam_ref[0]
    R = 2 * H
    grp = lax.broadcasted_iota(jnp.int32, (R, W_MIX), 1) // DK
    rid = lax.broadcasted_iota(jnp.int32, (R, W_MIX), 0)
    rhead = lax.broadcasted_iota(jnp.int32, (R, DH), 0) // 2
    slope = jnp.concatenate([jnp.full((2, 1), sl_ref[hh], F32) for hh in range(H)], axis=0)
    col = lax.broadcasted_iota(jnp.int32, (R, n_pages * PR), 1)
    own = (col % H) == (lax.broadcasted_iota(jnp.int32, (R, n_pages * PR), 0) // 2)
    bias = jnp.where(own, -slope * (past - col // H).astype(F32), NEG_INF)

    for t in range(n_tok):
        k_refs = refs[t * n_pages:(t + 1) * n_pages]
        v_refs = refs[(n_tok + t) * n_pages:(n_tok + t + 1) * n_pages]
        qbd = jnp.where(grp == rid, jnp.broadcast_to(q_ref[t], (R, W_MIX)), 0.0) * (DK ** -0.5)
        s_new = jnp.sum(qbd * kn_ref[t], axis=1, keepdims=True)
        qx = sum(qbd[:, hh * DH:(hh + 1) * DH] for hh in range(H))
        vnx = sum(jnp.where(rhead == hh, vn_ref[t, :, hh * DH:(hh + 1) * DH], 0.0) for hh in range(H))
        s = jnp.concatenate([_dot_nt(qx, k_refs[j][...]) for j in range(n_pages)], axis=1)
        s = s + bias
        m = jnp.maximum(jnp.max(s, axis=1, keepdims=True), s_new)
        p = jnp.exp(s - m)
        p_new = jnp.exp(s_new - m)
        inv_l = 1.0 / (jnp.sum(p, axis=1, keepdims=True) + p_new)
        p = p * inv_l
        o8 = (p_new * inv_l) * vnx
        for j in range(n_pages):
            o8 = o8 + _dot(p[:, j * PR:(j + 1) * PR], v_refs[j][...])
        for hh in range(H):
            o = o8[2 * hh:2 * hh + 1, :] - lam * o8[2 * hh + 1:2 * hh + 2, :]
            o_ref[t, :, hh * DH:(hh + 1) * DH] = _rms(o, g_ref[...]) * (1.0 - LAM_INIT)


def _attn_step(bq, k_new, v_new, cache_k, cache_v, page_table, lam, slopes, g_subln):
    N = bq.shape[0]
    n_pool, page = cache_k.shape[0], cache_k.shape[1]
    n_pages = page_table.shape[1]
    n_tok = ATTN_STEP_TOKENS
    cache_k = cache_k.reshape(n_pool * page * H, DH)
    cache_v = cache_v.reshape(n_pool * page * H, DH)
    smem = pl.BlockSpec(memory_space=pltpu.SMEM)
    rowspec = pl.BlockSpec((n_tok, 1, W_MIX), lambda b, pt: (b, 0, 0))
    pspec = lambda t, j: pl.BlockSpec((page * H, DH), lambda b, pt: (pt[(b * n_tok + t) * n_pages + j], 0))
    pages = [pspec(t, j) for t in range(n_tok) for j in range(n_pages)]
    grid_spec = pltpu.PrefetchScalarGridSpec(
        num_scalar_prefetch=1,
        grid=(N // n_tok,),
        in_specs=[smem, smem, rowspec, rowspec, rowspec, pl.BlockSpec((1, DH), lambda b, pt: (0, 0))] + pages * 2,
        out_specs=rowspec,
    )
    out = pl.pallas_call(
        functools.partial(_attn_step_kernel, n_pages=n_pages, page=page, n_tok=n_tok),
        grid_spec=grid_spec,
        out_shape=jax.ShapeDtypeStruct((N, 1, W_MIX), F32),
        compiler_params=_cp(("arbitrary",)),
        name="attn_step",
    )(page_table.reshape(-1), lam.reshape(1), slopes, bq.reshape(N, 1, W_MIX), k_new.reshape(N, 1, W_MIX),
      v_new.reshape(N, 1, W_MIX), g_subln.reshape(1, DH), *([cache_k] * (n_tok * n_pages)),
      *([cache_v] * (n_tok * n_pages)))
    return out.reshape(N, W_MIX)


def _merge_kernel(oa_ref, ob_ref, gt_ref, x_ref, g1_ref, sc2_ref, sh2_ref, gpost_ref, gpre_ref,
                  wa_ref, wb_ref, wo_ref, wr_ref, br_ref, cnt0_ref, x1_ref, h3_ref, rt_ref, cnt_ref, run_ref,
                  *, n_part):
    i = pl.program_id(0)
    tm = x_ref.shape[0]
    tp = tm // n_part
    parts = [slice(p * tp, (p + 1) * tp) for p in range(n_part)]
    rows = lambda ref, r: ref[...] if ref.shape[0] == 1 else ref[r, :]

    @pl.when(i == 0)
    def _():
        run_ref[...] = cnt0_ref[...]

    pa = [_dot(oa_ref[r, :], wa_ref[...]) for r in parts]
    pb = [_dot(ob_ref[r, :], wb_ref[...]) for r in parts]
    merged = [_sigmoid(gt_ref[r, :D].astype(F32)) * pa[p] + _sigmoid(gt_ref[r, D:].astype(F32)) * pb[p]
              for p, r in enumerate(parts)]
    z = [_dot(m.astype(BF16), wo_ref[...]) for m in merged]
    h2 = []
    for p, r in enumerate(parts):
        x1 = x_ref[r, :] + rows(g1_ref, r) * _rms(z[p], gpost_ref[...])
        x1_ref[r, :] = x1
        h2.append(_rms(x1, gpre_ref[...]) * (1.0 + rows(sc2_ref, r)) + rows(sh2_ref, r))
        _store_row_tiles(h3_ref, h2[p], p * tp)
    logits = [_dot(h, wr_ref[...]) + br_ref[...] for h in h2]

    lane = lax.broadcasted_iota(jnp.int32, (tp, N_EXPERTS), 1).astype(F32)
    sels, idxs, gates, chosen = [], [], [], []
    for p in range(n_part):
        work = logits[p]
        sel_p, idx_p, val_p = [], [], []
        for _ in range(TOP_K):
            mx = jnp.max(work, axis=1, keepdims=True)
            idx = jnp.min(jnp.where(work == mx, lane, float(N_EXPERTS)), axis=1, keepdims=True)
            sel = lane == idx
            work = jnp.where(sel, -jnp.inf, work)
            sel_p.append(sel)
            idx_p.append(idx)
            val_p.append(mx)
        ex = [jnp.exp(v - val_p[0]) for v in val_p]
        inv_den = 1.0 / (ex[0] + ex[1] + ex[2] + ex[3])
        sels.append(sel_p)
        idxs.append(idx_p)
        gates.append([e * inv_den for e in ex])
        chosen.append(sum(s.astype(F32) for s in sel_p))

    chosen_all = jnp.concatenate(chosen, axis=0) if n_part > 1 else chosen[0]
    before = (lax.broadcasted_iota(jnp.int32, (tm, tm), 0) > lax.broadcasted_iota(jnp.int32, (tm, tm), 1))
    prior = _dot(before.astype(BF16), chosen_all.astype(BF16)) + run_ref[...]
    run_ref[...] = run_ref[...] + jnp.sum(chosen_all, axis=0, keepdims=True)
    cnt_ref[...] = run_ref[...]

    rlane = lax.broadcasted_iota(jnp.int32, (tp, LANES), 1)
    for p, r in enumerate(parts):
        rt = jnp.zeros((tp, LANES), F32)
        for kk in range(TOP_K):
            rank = jnp.sum(jnp.where(sels[p][kk], prior[r, :], 0.0), axis=1, keepdims=True)
            rt = jnp.where(rlane == RT_E + kk, idxs[p][kk], rt)
            rt = jnp.where(rlane == RT_RANK + kk, rank, rt)
            rt = jnp.where(rlane == RT_GATE + kk, gates[p][kk], rt)
        rt_ref[r, :] = rt


def _merge(oa, ob, gt, x2d, m3d, tiles_per_mod, g_post, g_pre, wa, wb, wo, wr, br, cnt0, tm):
    R = x2d.shape[0]
    r = m3d.shape[1]
    mod = lambda col: pl.BlockSpec((None, r, D), lambda i: (i // tiles_per_mod, 0, col))
    row = lambda w: pl.BlockSpec((tm, w), lambda i: (i, 0))
    full = lambda a, b: pl.BlockSpec((a, b), lambda i: (0, 0))
    return pl.pallas_call(
        functools.partial(_merge_kernel, n_part=4 if tm >= 512 else 1),
        grid=(R // tm,),
        in_specs=[row(W_MIX), row(W_MIX), row(2 * D), row(D), mod(2), mod(4), mod(3),
                  full(1, D), full(1, D), full(W_MIX, D), full(W_MIX, D), full(D, D),
                  full(D, N_EXPERTS), full(1, N_EXPERTS), full(1, N_EXPERTS)],
        out_specs=[row(D), pl.BlockSpec((tm * SUB, LANES), lambda i: (i, 0)), row(LANES), full(1, N_EXPERTS)],
        out_shape=[jax.ShapeDtypeStruct((R, D), F32),
                   jax.ShapeDtypeStruct((R * SUB, LANES), F32),
                   jax.ShapeDtypeStruct((R, LANES), F32),
                   jax.ShapeDtypeStruct((1, N_EXPERTS), F32)],
        scratch_shapes=[pltpu.VMEM((1, N_EXPERTS), F32)],
        compiler_params=_cp(("arbitrary",)),
        name="merge",
    )(oa, ob, gt, x2d, m3d, m3d, m3d, g_post.reshape(1, D), g_pre.reshape(1, D), wa, wb, wo, wr,
      br.reshape(1, N_EXPERTS), cnt0)


def _row_copy(src, s_row, dst, d_row, sem):
    return pltpu.make_async_copy(src.at[pl.ds(pl.multiple_of(s_row, SUB), SUB)],
                                 dst.at[pl.ds(pl.multiple_of(d_row, SUB), SUB)], sem)


def _dispatch_kernel(lo_ref, hi_ref, nu_ref, da_ref, db_ref, ha_ref, hb_ref, x_hbm, zero_ref, sem):
    i = pl.program_id(0)
    last = pl.num_programs(0) - 1
    unroll = 8

    def scatter_rows(d_ref, h_ref):
        n = h_ref.shape[0] // SUB

        def issue(j, carry):
            for u in range(unroll):
                t = j * unroll + u
                for kk in range(TOP_K):
                    _row_copy(h_ref, t * SUB, x_hbm, d_ref[t * TOP_K + kk], sem).start(priority=kk % 2)
            return carry

        lax.fori_loop(0, n // unroll, issue, 0)
        for _ in range(TOP_K):
            pltpu.make_async_copy(h_ref, x_hbm.at[pl.ds(0, n * SUB)], sem).wait()

    @pl.when(i < last)
    def _():
        scatter_rows(da_ref, ha_ref)

    @pl.when(i == last)
    def _():
        scatter_rows(db_ref, hb_ref)
        zero_ref[...] = jnp.zeros_like(zero_ref)

        def per_expert(e, carry):
            zrow = lambda r: _row_copy(zero_ref, 0, x_hbm, r * SUB, sem)
            lax.fori_loop(lo_ref[e], hi_ref[e], lambda r, c: (zrow(r).start(), c)[1], 0)
            lax.fori_loop(lo_ref[e], hi_ref[e], lambda r, c: (zrow(r).wait(), c)[1], 0)
            return carry

        lax.fori_loop(0, N_EXPERTS, per_expert, 0)

        blk = zero_ref.shape[0]
        zblk = lambda b: pltpu.make_async_copy(zero_ref, x_hbm.at[pl.ds(pl.multiple_of(b * blk, blk), blk)], sem)
        lax.fori_loop(nu_ref[0], x_hbm.shape[0] // blk, lambda b, c: (zblk(b).start(), c)[1], 0)
        lax.fori_loop(nu_ref[0], x_hbm.shape[0] // blk, lambda b, c: (zblk(b).wait(), c)[1], 0)


def _dispatch(h3_a, h3_b, dest8_a, dest8_b, pad_lo, pad_hi, n_used, n_rows):
    tm = DISPATCH_ROWS
    n_a = h3_a.shape[0] // (tm * SUB)
    clamp = lambda i: jnp.minimum(i, n_a - 1)
    grid_spec = pltpu.PrefetchScalarGridSpec(
        num_scalar_prefetch=3,
        grid=(n_a + 1,),
        in_specs=[pl.BlockSpec((tm * TOP_K,), lambda i, *_: (clamp(i),), memory_space=pltpu.SMEM),
                  pl.BlockSpec(dest8_b.shape, lambda i, *_: (0,), memory_space=pltpu.SMEM),
                  pl.BlockSpec((tm * SUB, LANES), lambda i, *_: (clamp(i), 0)),
                  pl.BlockSpec(h3_b.shape, lambda i, *_: (0, 0))],
        out_specs=pl.BlockSpec(memory_space=pl.ANY),
        scratch_shapes=[pltpu.VMEM((MOE_BLOCK * SUB, LANES), F32), pltpu.SemaphoreType.DMA],
    )
    return pl.pallas_call(
        _dispatch_kernel,
        grid_spec=grid_spec,
        out_shape=jax.ShapeDtypeStruct((n_rows * SUB, LANES), F32),
        compiler_params=_cp(("arbitrary",)),
        name="dispatch",
    )(pad_lo, pad_hi, n_used, dest8_a, dest8_b, h3_a, h3_b)


def _expert_kernel(be_ref, nu_ref, nxt_ref, x_ref, wgu_hbm, bgu_ref, wd_hbm, bd_ref, y_ref,
                   wgu_f, wd_f, wgu_b, wd_b, slot_ref, sem_gu, sem_d):
    i = pl.program_id(0)

    def fetch(e, slot):
        return (pltpu.make_async_copy(wgu_hbm.at[e], wgu_f.at[slot], sem_gu.at[slot]),
                pltpu.make_async_copy(wd_hbm.at[e], wd_f.at[slot], sem_d.at[slot]))

    @pl.when(i == 0)
    def _():
        slot_ref[0] = 0
        for cp in fetch(be_ref[0], 0):
            cp.start()

    @pl.when(i < nu_ref[0])
    def _():
        @pl.when((i == 0) | (be_ref[i] != be_ref[jnp.maximum(i - 1, 0)]))
        def _():
            slot = slot_ref[0]
            for cp in fetch(be_ref[i], slot):
                cp.wait()
            wgu_b[...] = wgu_f[slot].astype(BF16)
            wd_b[...] = wd_f[slot].astype(BF16)

            @pl.when(nxt_ref[i] >= 0)
            def _():
                for cp in fetch(nxt_ref[i], 1 - slot):
                    cp.start()
            slot_ref[0] = 1 - slot

        gu = _dot(_load_row_tiles(x_ref).astype(BF16), wgu_b[...]) + bgu_ref[...]
        glu = jnp.minimum(gu[:, :D], SWIGLU_LIMIT)
        lin = jnp.clip(gu[:, D:], -SWIGLU_LIMIT, SWIGLU_LIMIT)
        act = (lin + 1.0) * glu * _sigmoid(SWIGLU_ALPHA * glu)
        _store_row_tiles(y_ref, _dot(act.astype(BF16), wd_b[...]) + bd_ref[...])

    @pl.when(i >= nu_ref[0])
    def _():
        y_ref[...] = jnp.zeros_like(y_ref)


def _experts(x3, block_expert, n_used, next_expert, w_gate_up, b_gate_up, w_down, b_down):
    bm = MOE_BLOCK
    n_blocks = x3.shape[0] // (bm * SUB)
    used = lambda i, nu: jnp.minimum(i, nu[0] - 1)
    anyspec = pl.BlockSpec(memory_space=pl.ANY)
    grid_spec = pltpu.PrefetchScalarGridSpec(
        num_scalar_prefetch=3,
        grid=(n_blocks,),
        in_specs=[pl.BlockSpec((bm * SUB, LANES), lambda i, be, nu, nx: (used(i, nu), 0)),
                  anyspec,
                  pl.BlockSpec((None, 1, 2 * D), lambda i, be, nu, nx: (be[used(i, nu)], 0, 0)),
                  anyspec,
                  pl.BlockSpec((None, 1, D), lambda i, be, nu, nx: (be[used(i, nu)], 0, 0))],
        out_specs=pl.BlockSpec((bm * SUB, LANES), lambda i, be, nu, nx: (i, 0)),
        scratch_shapes=[pltpu.VMEM((2, D, 2 * D), F32), pltpu.VMEM((2, D, D), F32),
                        pltpu.VMEM((D, 2 * D), BF16), pltpu.VMEM((D, D), BF16),
                        pltpu.SMEM((1,), jnp.int32),
                        pltpu.SemaphoreType.DMA((2,)), pltpu.SemaphoreType.DMA((2,))],
    )
    return pl.pallas_call(
        _expert_kernel,
        grid_spec=grid_spec,
        out_shape=jax.ShapeDtypeStruct(x3.shape, F32),
        compiler_params=_cp(("arbitrary",)),
        name="experts",
    )(block_expert, n_used, next_expert, x3, w_gate_up, b_gate_up.reshape(N_EXPERTS, 1, 2 * D), w_down,
      b_down.reshape(N_EXPERTS, 1, D))


def _final_kernel(dcur_ref, dnext_ref, x1_ref, g2_ref, gpost_ref, rt_ref, y_hbm, o_ref, ybuf, sem):
    i = pl.program_id(0)
    n = pl.num_programs(0)
    tm = x1_ref.shape[0]
    slot = i % 2
    unroll = 8

    def issue(d_ref, s):
        def body(j, carry):
            for u in range(unroll):
                t = j * unroll + u
                for kk in range(TOP_K):
                    _row_copy(y_hbm, d_ref[t * TOP_K + kk], ybuf.at[s, kk], t * SUB, sem.at[s]).start(priority=kk % 2)
            return carry
        lax.fori_loop(0, tm // unroll, body, 0)

    @pl.when(i == 0)
    def _():
        issue(dcur_ref, 0)

    @pl.when(i + 1 < n)
    def _():
        issue(dnext_ref, 1 - slot)

    for kk in range(TOP_K):
        pltpu.make_async_copy(y_hbm.at[pl.ds(0, tm * SUB)], ybuf.at[slot, kk], sem.at[slot]).wait()

    y = jnp.zeros((tm, D), F32)
    for kk in range(TOP_K):
        y = y + rt_ref[:, RT_GATE + kk:RT_GATE + kk + 1] * _load_row_tiles(ybuf.at[slot, kk])
    o_ref[...] = x1_ref[...] + g2_ref[...] * _rms(y, gpost_ref[...])


def _final(x1, m3d, tiles_per_mod, g_post, route, dest_flat, y3, row0, tm):
    R = x1.shape[0]
    r = m3d.shape[1]
    off = row0 // tm
    nsteps = R // tm
    row = pl.BlockSpec((tm, D), lambda i: (i, 0))
    dspec = lambda nxt: pl.BlockSpec((tm * TOP_K,), lambda i: (jnp.minimum(i + nxt, nsteps - 1) + off,),
                                     memory_space=pltpu.SMEM)
    return pl.pallas_call(
        _final_kernel,
        grid=(nsteps,),
        in_specs=[dspec(0), dspec(1), row,
                  pl.BlockSpec((None, r, D), lambda i: (i // tiles_per_mod, 0, 5)),
                  pl.BlockSpec((1, D), lambda i: (0, 0)),
                  pl.BlockSpec((tm, LANES), lambda i: (i, 0)),
                  pl.BlockSpec(memory_space=pl.ANY)],
        out_specs=row,
        out_shape=jax.ShapeDtypeStruct((R, D), F32),
        scratch_shapes=[pltpu.VMEM((2, TOP_K, tm * SUB, LANES), F32), pltpu.SemaphoreType.DMA((2,))],
        compiler_params=_cp(("arbitrary",)),
        name="final",
    )(dest_flat, dest_flat, x1, m3d, g_post.reshape(1, D), route, y3)


def kernel(x_prompt, x_sample, cache_k, cache_v, state_hgrn, page_table, c_prompt, c_sample, w_mod, b_mod, g_pre_mix, g_post_mix, g_pre_ffn, g_post_ffn, w_in, hgrn_lb, g_hgrn, lam_q1, lam_k1, lam_q2, lam_k2, g_subln, w_proj_a, w_proj_b, w_out, w_router, b_router, w_gate_up, b_gate_up, w_down, b_down):
    B, L, _ = x_prompt.shape
    NS = x_sample.shape[0]
    T = B * L
    T_all = T + NS
    l = 0

    lb = jnp.cumsum(jax.nn.softmax(hgrn_lb.astype(F32), axis=0), axis=0)[l]
    lam = (jnp.exp(jnp.sum(lam_q1[l] * lam_k1[l])) - jnp.exp(jnp.sum(lam_q2[l] * lam_k2[l])) + LAM_INIT).astype(F32)
    slopes = jnp.exp2(-8.0 * jnp.arange(1, H + 1, dtype=F32) / H)

    m_all = _modulation(jnp.concatenate([c_prompt, c_sample], axis=0), w_mod[l], b_mod[l])
    m_p = m_all[:B].reshape(B, 1, 6 * D)
    m_s = m_all[B:].reshape(1, NS, 6 * D)

    w_in_b = w_in[l].astype(BF16)
    tm = 512
    tpm = L // tm
    xp = x_prompt.reshape(T, D)
    xs = x_sample.reshape(NS, D)
    merge_w = (g_post_mix[l], g_pre_ffn[l], w_proj_a[l].astype(BF16), w_proj_b[l].astype(BF16),
               w_out[l].astype(BF16), w_router[l], b_router[l])

    aq_p, af_p, ai_p, ag_p, bq_p, k_p, v_p, gt_p, k4_p, v4_p = _inproj(xp, m_p, L // INPROJ_ROWS, g_pre_mix[l], w_in_b,
                                                                       INPROJ_ROWS)
    seq = lambda a: a.reshape(B, L, W_MIX)
    oa_p, s_p = _hgrn_prompt(seq(aq_p), seq(af_p), seq(ai_p), seq(ag_p), lb, g_hgrn[l])
    ob_p = _attn_prompt(bq_p, k_p, v_p, lam, slopes, g_subln[l], B, L)
    x1_p, h3_p, rt_p, cnt = _merge(oa_p.reshape(T, W_MIX), ob_p, gt_p, xp, m_p, L // MERGE_ROWS, *merge_w,
                                   jnp.zeros((1, N_EXPERTS), F32), MERGE_ROWS)

    aq_s, af_s, ai_s, ag_s, bq_s, _, _, gt_s, k4_s, v4_s = _inproj(xs, m_s, 1, g_pre_mix[l], w_in_b, NS)
    hg_s = jnp.concatenate([a.astype(F32) for a in (aq_s, af_s, ai_s, ag_s)], axis=1)
    oa_s, s_s = _hgrn_step(hg_s, state_hgrn[l], lb, g_hgrn[l])
    ob_s = _attn_step(bq_s.astype(F32), k4_s.reshape(NS, W_MIX), v4_s.reshape(NS, W_MIX), cache_k[l], cache_v[l],
                      page_table, lam, slopes, g_subln[l])
    x1_s, h3_s, rt_s, cnt = _merge(oa_s.astype(BF16), ob_s.astype(BF16), gt_s, xs, m_s, 1, *merge_w, cnt, NS)

    bm = MOE_BLOCK
    counts = cnt.reshape(N_EXPERTS).astype(jnp.int32)
    padded = (counts + bm - 1) // bm * bm
    pad_end = jnp.cumsum(padded)
    pad_start = pad_end - padded
    n_blocks = -(-T_all * TOP_K // bm) + N_EXPERTS
    route = jnp.concatenate([rt_p[:, :RT_GATE], rt_s[:, :RT_GATE]], axis=0)
    e_idx = route[:, RT_E:RT_E + TOP_K].astype(jnp.int32)
    rank = route[:, RT_RANK:RT_RANK + TOP_K].astype(jnp.int32)
    e_iota = jnp.arange(N_EXPERTS, dtype=jnp.int32)
    dest = rank + jnp.sum(jnp.where(e_idx[..., None] == e_iota, pad_start, 0), axis=-1)
    dest_flat = dest.reshape(-1) * SUB
    block_expert = jnp.minimum(
        jnp.sum((jnp.arange(n_blocks, dtype=jnp.int32)[:, None] * bm >= pad_end[None, :]).astype(jnp.int32), axis=1),
        N_EXPERTS - 1)
    n_used = pad_end[-1:] // bm
    later = lax.cummin(jnp.where(counts > 0, jnp.arange(N_EXPERTS, dtype=jnp.int32), N_EXPERTS), reverse=True)
    later = jnp.concatenate([later[1:], jnp.full((1,), N_EXPERTS, jnp.int32)])
    next_expert = jnp.where(later < N_EXPERTS, later, -1)[block_expert]

    x3 = _dispatch(h3_p, h3_s, dest_flat[:T * TOP_K], dest_flat[T * TOP_K:], pad_start + counts, pad_end, n_used,
                   n_blocks * bm)
    y3 = _experts(x3, block_expert, n_used, next_expert, w_gate_up[l], b_gate_up[l], w_down[l], b_down[l])
    y_p = _final(x1_p, m_p, tpm, g_post_ffn[l], rt_p, dest_flat, y3, 0, tm)
    y_s = _final(x1_s, m_s, 1, g_post_ffn[l], rt_s, dest_flat, y3, T, NS)

    return (y_p.reshape(B, L, D), y_s.reshape(NS, 1, D),
            k4_p.reshape(1, B, L, H, DH), v4_p.reshape(1, B, L, H, DH), s_p.reshape(1, B, H, DH, DH),
            k4_s.reshape(1, NS, 1, H, DH), v4_s.reshape(1, NS, 1, H, DH), s_s.reshape(1, NS, H, DH, DH))
```

```python
import functools
import math

import jax
import jax.numpy as jnp
from jax import lax
from jax.experimental import pallas as pl
from jax.experimental.pallas import tpu as pltpu

F32 = jnp.float32
BF16 = jnp.bfloat16

D = 1024
LANES = 128
SUB = D // LANES
H = 4
DH = 128
DK = 64
W_MIX = H * DH
HGRN_CHUNK = 64
N_EXPERTS = 32
TOP_K = 4
SWIGLU_LIMIT = 7.0
SWIGLU_ALPHA = 1.702
EPS = 1e-6
NEG_INF = -1e30
LOG2E = 1.4426950408889634
LAM_INIT = 0.8 - 0.6 * math.exp(-0.3 * 0)
C_HG = (0, 4 * W_MIX)
C_BQ = (4 * W_MIX, 5 * W_MIX)
C_BK = (5 * W_MIX, 6 * W_MIX)
C_BV = (6 * W_MIX, 7 * W_MIX)
C_GT = (7 * W_MIX, 7 * W_MIX + 2 * D)
D_IN = 7 * W_MIX + 2 * D
RT_E, RT_RANK, RT_GATE = 0, TOP_K, 2 * TOP_K

VMEM_LIMIT = 56 * 1024 * 1024
MOE_BLOCK = 256
DISPATCH_ROWS = 1024
MERGE_ROWS = 512
INPROJ_ROWS = 512
ATTN_STEP_TOKENS = 2
ATT_G = 256
ATT_NG = 8
ATT_TK = 256
ATT_VR = DH + 16


def _cp(sem, vmem=VMEM_LIMIT):
    return pltpu.CompilerParams(dimension_semantics=sem, vmem_limit_bytes=vmem)


def _sigmoid(x):
    return 1.0 / (1.0 + jnp.exp(-x))


def _rms(x, g):
    return x * lax.rsqrt(jnp.mean(x * x, axis=-1, keepdims=True) + EPS) * g


def _dot(a, b):
    return jnp.dot(a, b, preferred_element_type=F32)


def _dot_nt(a, b):
    return lax.dot_general(a, b, (((1,), (1,)), ((), ())), preferred_element_type=F32)


def _dot_tn(a, b):
    return lax.dot_general(a, b, (((0,), (0,)), ((), ())), preferred_element_type=F32)


def _store_row_tiles(ref2, x2d, row0=0):
    n = x2d.shape[0]
    for j in range(SUB):
        ref2[pl.ds(row0 * SUB + j, n, stride=SUB), :] = x2d[:, j * LANES:(j + 1) * LANES]


def _load_row_tiles(ref2):
    n = ref2.shape[0] // SUB
    return jnp.concatenate([ref2[pl.ds(j, n, stride=SUB), :] for j in range(SUB)], axis=1)


def _mod_kernel(c_ref, w_ref, b_ref, o_ref):
    c = c_ref[...]
    o_ref[...] = _dot(c * _sigmoid(c), w_ref[...]) + b_ref[...]


def _modulation(c_all, w_mod, b_mod):
    n = c_all.shape[0]
    tn = 1024
    return pl.pallas_call(
        _mod_kernel,
        grid=(6 * D // tn,),
        in_specs=[pl.BlockSpec((n, D), lambda j: (0, 0)),
                  pl.BlockSpec((D, tn), lambda j: (0, j)),
                  pl.BlockSpec((1, tn), lambda j: (0, j))],
        out_specs=pl.BlockSpec((n, tn), lambda j: (0, j)),
        out_shape=jax.ShapeDtypeStruct((n, 6 * D), F32),
        compiler_params=_cp(("arbitrary",)),
        name="modulation",
    )(c_all, w_mod, b_mod.reshape(1, 6 * D))


def _inproj_kernel(x_ref, sc_ref, sh_ref, g_ref, w_ref,
                   aq_ref, af_ref, ai_ref, ag_ref, bq_ref, k_ref, v_ref, gt_ref, k4_ref, v4_ref):
    h = _rms(x_ref[...], g_ref[...]) * (1.0 + sc_ref[...]) + sh_ref[...]
    hb = h.astype(BF16)
    mm = lambda lo, hi: _dot(hb, w_ref[:, lo:hi])
    for j, o_ref in enumerate((aq_ref, af_ref, ai_ref, ag_ref)):
        o_ref[...] = mm(C_HG[0] + j * W_MIX, C_HG[0] + (j + 1) * W_MIX).astype(o_ref.dtype)
    bq_ref[...] = mm(*C_BQ).astype(BF16)
    gt_ref[...] = mm(*C_GT).astype(BF16)
    for c, o_ref, o4_ref in ((C_BK, k_ref, k4_ref), (C_BV, v_ref, v4_ref)):
        r = mm(*c)
        o_ref[...] = r.astype(BF16)
        o4 = o4_ref.reshape(r.shape[0] * H, DH)
        for hh in range(H):
            o4[pl.ds(hh, r.shape[0], stride=H), :] = r[:, hh * DH:(hh + 1) * DH]


def _inproj(x2d, m3d, tiles_per_mod, g_pre, w_in_bf16, tm):
    R = x2d.shape[0]
    r = m3d.shape[1]
    mod = lambda col: pl.BlockSpec((None, r, D), lambda i: (i // tiles_per_mod, 0, col))
    row = lambda w: pl.BlockSpec((tm, w), lambda i: (i, 0))
    row4 = pl.BlockSpec((tm, H, DH), lambda i: (i, 0, 0))
    outs = [(W_MIX, BF16), (W_MIX, F32), (W_MIX, BF16), (W_MIX, BF16), (W_MIX, BF16), (W_MIX, BF16), (W_MIX, BF16),
            (2 * D, BF16)]
    kv4 = jax.ShapeDtypeStruct((R, H, DH), F32)
    return pl.pallas_call(
        _inproj_kernel,
        grid=(R // tm,),
        in_specs=[row(D), mod(1), mod(0),
                  pl.BlockSpec((1, D), lambda i: (0, 0)),
                  pl.BlockSpec((D, D_IN), lambda i: (0, 0))],
        out_specs=[row(w) for w, _ in outs] + [row4, row4],
        out_shape=[jax.ShapeDtypeStruct((R, w), dt) for w, dt in outs] + [kv4, kv4],
        compiler_params=_cp(("arbitrary",)),
        name="inproj",
    )(x2d, m3d, m3d, g_pre.reshape(1, D), w_in_bf16)


def _hgrn_prompt_kernel(lb_ref, gn_ref, q_ref, f_ref, i_ref, g_ref, o_ref, sout_ref, st_ref):
    c = pl.program_id(0)
    nb = q_ref.shape[0]
    C = HGRN_CHUNK

    @pl.when(c == 0)
    def _():
        st_ref[...] = jnp.zeros_like(st_ref)

    row = lax.broadcasted_iota(jnp.int32, (C, DH), 0)
    causal = lax.broadcasted_iota(jnp.int32, (C, C), 0) >= lax.broadcasted_iota(jnp.int32, (C, C), 1)

    def per_batch_group(bp, carry):
        ch = [(4 * bp + d, h, slice(h * DH, (h + 1) * DH)) for d in range(4) for h in range(H)]
        qd, kd, k2, v, eg = [], [], [], [], []
        for b, h, hs in ch:
            lb = lb_ref[:, hs]
            f = lb + (1.0 - lb) * _sigmoid(f_ref[b, :, hs].astype(F32))
            k = 1.0 - f
            gc = jnp.log(f)
            for s in (1, 2, 4, 8, 16, 32):
                gc = gc + jnp.where(row >= s, pltpu.roll(gc, s, axis=0), 0.0)
            g_last = gc[C - 1:C, :]
            qd.append(q_ref[b, :, hs].astype(F32) * jnp.exp(gc))
            kd.append(k * jnp.exp(-gc))
            k2.append(k * jnp.exp(g_last - gc))
            eg.append(jnp.exp(g_last))
            v.append(i_ref[b, :, hs].astype(F32))
        n = len(ch)
        st = [st_ref[b, h] for b, h, _ in ch]
        a = [jnp.where(causal, _dot_nt(qd[i], kd[i]), 0.0) for i in range(n)]
        o_inter = [_dot_nt(qd[i], st[i]) for i in range(n)]
        kv_t = [_dot_tn(v[i], k2[i]) for i in range(n)]
        o_intra = [_dot(a[i], v[i]) for i in range(n)]
        for i, (b, h, hs) in enumerate(ch):
            st_ref[b, h] = eg[i] * st[i] + kv_t[i]
            o = _rms(o_intra[i] + o_inter[i], gn_ref[:, hs])
            ag = g_ref[b, :, hs].astype(F32)
            o_ref[b, :, hs] = (o * (ag * _sigmoid(ag))).astype(o_ref.dtype)
        return carry

    lax.fori_loop(0, nb // 4, per_batch_group, 0)

    @pl.when(c == pl.num_programs(0) - 1)
    def _():
        def fin(b, carry):
            for h in range(H):
                sout_ref[b, h] = st_ref[b, h].T
            return carry
        lax.fori_loop(0, nb, fin, 0)


def _hgrn_prompt(aq, af, ai, ag, lb, g_hgrn):
    B, L, _ = aq.shape
    C = HGRN_CHUNK
    blk = pl.BlockSpec((B, C, W_MIX), lambda c: (0, c, 0))
    vec = pl.BlockSpec((1, W_MIX), lambda c: (0, 0))
    return pl.pallas_call(
        _hgrn_prompt_kernel,
        grid=(L // C,),
        in_specs=[vec, vec, blk, blk, blk, blk],
        out_specs=[blk, pl.BlockSpec((B, H, DH, DH), lambda c: (0, 0, 0, 0))],
        out_shape=[jax.ShapeDtypeStruct((B, L, W_MIX), BF16),
                   jax.ShapeDtypeStruct((B, H, DH, DH), F32)],
        scratch_shapes=[pltpu.VMEM((B, H, DH, DH), F32)],
        compiler_params=_cp(("arbitrary",)),
        name="hgrn_prompt",
    )(lb.reshape(1, W_MIX), g_hgrn.reshape(1, W_MIX), aq, af, ai, ag)


def _hgrn_step_kernel(lb_ref, gn_ref, x_ref, s_ref, o_ref, sout_ref):
    nt = x_ref.shape[0]
    ones = jnp.ones((8, DH), F32)
    pad = jnp.zeros((6, DH), F32)

    def col(r):
        hi = r.astype(BF16).astype(F32)
        return _dot_tn(jnp.concatenate([hi, r - hi, pad], axis=0), ones)

    for t in range(nt):
        for h in range(H):
            seg = lambda j: x_ref[t:t + 1, j * W_MIX + h * DH: j * W_MIX + (h + 1) * DH]
            lb = lb_ref[:, h * DH:(h + 1) * DH]
            q, v, ag = seg(0), seg(2), seg(3)
            f = lb + (1.0 - lb) * _sigmoid(seg(1))
            f_col = col(f)
            s_new = f_col * s_ref[t, h] + (1.0 - f_col) * v
            sout_ref[t, h] = s_new
            o = jnp.sum(col(q) * s_new, axis=0, keepdims=True)
            o = _rms(o, gn_ref[:, h * DH:(h + 1) * DH])
            o_ref[t:t + 1, h * DH:(h + 1) * DH] = o * (ag * _sigmoid(ag))


def _hgrn_step(hg, state, lb, g_hgrn):
    N = hg.shape[0]
    nt = 8
    vec = pl.BlockSpec((1, W_MIX), lambda i: (0, 0))
    st = pl.BlockSpec((nt, H, DH, DH), lambda i: (i, 0, 0, 0))
    return pl.pallas_call(
        _hgrn_step_kernel,
        grid=(N // nt,),
        in_specs=[vec, vec, pl.BlockSpec((nt, 4 * W_MIX), lambda i: (i, 0)), st],
        out_specs=[pl.BlockSpec((nt, W_MIX), lambda i: (i, 0)), st],
        out_shape=[jax.ShapeDtypeStruct((N, W_MIX), F32),
                   jax.ShapeDtypeStruct((N, H, DH, DH), F32)],
        compiler_params=_cp(("arbitrary",)),
        name="hgrn_step",
    )(lb.reshape(1, W_MIX), g_hgrn.reshape(1, W_MIX), hg, state)


def _attn_prompt_kernel(lam_ref, sl_ref, q_ref, k_ref, v_ref, g_ref, o_ref, qt_ref, vt_ref, *chain_refs):
    G, NG, tk = ATT_G, ATT_NG, ATT_TK
    tq = G * NG
    m_refs = [chain_refs[mp * NG:(mp + 1) * NG] for mp in range(2)]
    acc_refs = [chain_refs[(2 + mp) * NG:(3 + mp) * NG] for mp in range(2)]
    slope = sl_ref[pl.program_id(1)] * LOG2E
    lam = lam_ref[0]
    L = q_ref.shape[0]
    q_t = q_ref[...].astype(F32).T * (DK ** -0.5 * LOG2E)
    sub = lax.broadcasted_iota(jnp.int32, (DH, L), 0)
    for mp in range(2):
        qm_t = jnp.where(sub < DK, q_t, 0.0) if mp == 0 else jnp.where(sub >= DK, q_t, 0.0)
        for t in range(L // G):
            qt_ref[mp, t] = qm_t[:, t * G:(t + 1) * G].astype(BF16)
    v_t = v_ref[...].astype(F32).T
    for kb in range(L // tk):
        vt_ref[kb, :DH, :] = v_t[:, kb * tk:(kb + 1) * tk].astype(BF16)
        vt_ref[kb, DH:, :] = jnp.ones((ATT_VR - DH, tk), BF16)
    idiff = lax.broadcasted_iota(jnp.int32, (tk, G), 1) - lax.broadcasted_iota(jnp.int32, (tk, G), 0)
    dmat = idiff.astype(F32) * slope

    def fold(qi, kb, chains):
        k0 = pl.multiple_of(kb * tk, tk)
        kt = k_ref[pl.ds(k0, tk), :]
        scores = [_dot(kt, qt_ref[mp, qi * NG + g]) for mp, g, _ in chains]
        probs = []
        for (mp, g, thresh), s in zip(chains, scores):
            s = s - dmat
            if thresh is not None:
                s = jnp.where(idiff >= thresh, s, NEG_INF)
            c = slope * ((qi * NG + g) * G - k0).astype(F32)
            m = m_refs[mp][g][...]
            m_new = jnp.maximum(m, jnp.max(s, axis=0, keepdims=True) - c)
            m_refs[mp][g][...] = m_new
            probs.append((jnp.exp2(s - (m_new + c)), jnp.exp2(m - m_new)))
        va = vt_ref[kb]
        pvs = [_dot(va, p.astype(BF16)) for p, _ in probs]
        for (mp, g, _), (_, alpha), pv in zip(chains, probs, pvs):
            acc_refs[mp][g][...] = alpha * acc_refs[mp][g][...] + pv

    def q_tile(qi, carry):
        for mp in range(2):
            for g in range(NG):
                m_refs[mp][g][...] = jnp.full((1, G), NEG_INF, F32)
                acc_refs[mp][g][...] = jnp.zeros((ATT_VR, G), F32)

        every = [(mp, g, None) for mp in range(2) for g in range(NG)]
        n_full = qi * (tq // tk)
        lax.fori_loop(0, n_full, lambda kb, c: (fold(qi, kb, every), c)[1], 0)
        for j in range(tq // tk):
            chains = []
            for mp in range(2):
                for g in range(NG):
                    thresh = j * tk - G * g
                    if thresh <= G - 1:
                        chains.append((mp, g, None if thresh <= -(tk - 1) else thresh))
            fold(qi, n_full + j, chains)


        for g in range(NG):
            a0, a1 = acc_refs[0][g][...], acc_refs[1][g][...]
            o_t = a0[:DH] / a0[DH:DH + 1] - lam * (a1[:DH] / a1[DH:DH + 1])
            o_t = o_t * lax.rsqrt(jnp.mean(o_t * o_t, axis=0, keepdims=True) + EPS)
            q0 = pl.multiple_of((qi * NG + g) * G, G)
            o_ref[pl.ds(q0, G), :] = (o_t.T * (g_ref[...] * (1.0 - LAM_INIT))).astype(o_ref.dtype)
        return carry

    lax.fori_loop(0, L // tq, q_tile, 0)


def _attn_prompt(bq, bk, bv, lam, slopes, g_subln, B, L):
    smem = pl.BlockSpec(memory_space=pltpu.SMEM)
    spec = pl.BlockSpec((L, DH), lambda b, h: (b, h))
    return pl.pallas_call(
        _attn_prompt_kernel,
        grid=(B, H),
        in_specs=[smem, smem, spec, spec, spec, pl.BlockSpec((1, DH), lambda b, h: (0, 0))],
        out_specs=spec,
        out_shape=jax.ShapeDtypeStruct((B * L, W_MIX), BF16),
        scratch_shapes=[pltpu.VMEM((2, L // ATT_G, DH, ATT_G), BF16),
                        pltpu.VMEM((L // ATT_TK, ATT_VR, ATT_TK), BF16)]
                       + [pltpu.VMEM((1, ATT_G), F32)] * (2 * ATT_NG)
                       + [pltpu.VMEM((ATT_VR, ATT_G), F32)] * (2 * ATT_NG),
        compiler_params=_cp(("arbitrary", "arbitrary")),
        name="attn_prompt",
    )(lam.reshape(1), slopes, bq, bk, bv, g_subln.reshape(1, DH))


def _attn_step_kernel(pt_ref, lam_ref, sl_ref, q_ref, kn_ref, vn_ref, g_ref, *refs, n_pages, page, n_tok):
    o_ref = refs[2 * n_tok * n_pages]
    PR = page * H
    past = n_pages * page
    lam = lam_ref[0]
    R = 2 * H
    grp = lax.broadcasted_iota(jnp.int32, (R, W_MIX), 1) // DK
    rid = lax.broadcasted_iota(jnp.int32, (R, W_MIX), 0)
    rhead = lax.broadcasted_iota(jnp.int32, (R, DH), 0) // 2
    slope = jnp.concatenate([jnp.full((2, 1), sl_ref[hh], F32) for hh in range(H)], axis=0)
    col = lax.broadcasted_iota(jnp.int32, (R, n_pages * PR), 1)
    own = (col % H) == (lax.broadcasted_iota(jnp.int32, (R, n_pages * PR), 0) // 2)
    bias = jnp.where(own, -slope * (past - col // H).astype(F32), NEG_INF)

    for t in range(n_tok):
        k_refs = refs[t * n_pages:(t + 1) * n_pages]
        v_refs = refs[(n_tok + t) * n_pages:(n_tok + t + 1) * n_pages]
        qbd = jnp.where(grp == rid, jnp.broadcast_to(q_ref[t], (R, W_MIX)), 0.0) * (DK ** -0.5)
        s_new = jnp.sum(qbd * kn_ref[t], axis=1, keepdims=True)
        qx = sum(qbd[:, hh * DH:(hh + 1) * DH] for hh in range(H))
        vnx = sum(jnp.where(rhead == hh, vn_ref[t, :, hh * DH:(hh + 1) * DH], 0.0) for hh in range(H))
        s = jnp.concatenate([_dot_nt(qx, k_refs[j][...]) for j in range(n_pages)], axis=1)
        s = s + bias
        m = jnp.maximum(jnp.max(s, axis=1, keepdims=True), s_new)
        p = jnp.exp(s - m)
        p_new = jnp.exp(s_new - m)
        inv_l = 1.0 / (jnp.sum(p, axis=1, keepdims=True) + p_new)
        p = p * inv_l
        o8 = (p_new * inv_l) * vnx
        for j in range(n_pages):
            o8 = o8 + _dot(p[:, j * PR:(j + 1) * PR], v_refs[j][...])
        for hh in range(H):
            o = o8[2 * hh:2 * hh + 1, :] - lam * o8[2 * hh + 1:2 * hh + 2, :]
            o_ref[t, :, hh * DH:(hh + 1) * DH] = _rms(o, g_ref[...]) * (1.0 - LAM_INIT)


def _attn_step(bq, k_new, v_new, cache_k, cache_v, page_table, lam, slopes, g_subln):
    N = bq.shape[0]
    n_pool, page = cache_k.shape[0], cache_k.shape[1]
    n_pages = page_table.shape[1]
    n_tok = ATTN_STEP_TOKENS
    cache_k = cache_k.reshape(n_pool * page * H, DH)
    cache_v = cache_v.reshape(n_pool * page * H, DH)
    smem = pl.BlockSpec(memory_space=pltpu.SMEM)
    rowspec = pl.BlockSpec((n_tok, 1, W_MIX), lambda b, pt: (b, 0, 0))
    pspec = lambda t, j: pl.BlockSpec((page * H, DH), lambda b, pt: (pt[(b * n_tok + t) * n_pages + j], 0))
    pages = [pspec(t, j) for t in range(n_tok) for j in range(n_pages)]
    grid_spec = pltpu.PrefetchScalarGridSpec(
        num_scalar_prefetch=1,
        grid=(N // n_tok,),
        in_specs=[smem, smem, rowspec, rowspec, rowspec, pl.BlockSpec((1, DH), lambda b, pt: (0, 0))] + pages * 2,
        out_specs=rowspec,
    )
    out = pl.pallas_call(
        functools.partial(_attn_step_kernel, n_pages=n_pages, page=page, n_tok=n_tok),
        grid_spec=grid_spec,
        out_shape=jax.ShapeDtypeStruct((N, 1, W_MIX), F32),
        compiler_params=_cp(("arbitrary",)),
        name="attn_step",
    )(page_table.reshape(-1), lam.reshape(1), slopes, bq.reshape(N, 1, W_MIX), k_new.reshape(N, 1, W_MIX),
      v_new.reshape(N, 1, W_MIX), g_subln.reshape(1, DH), *([cache_k] * (n_tok * n_pages)),
      *([cache_v] * (n_tok * n_pages)))
    return out.reshape(N, W_MIX)


def _merge_kernel(oa_ref, ob_ref, gt_ref, x_ref, g1_ref, sc2_ref, sh2_ref, gpost_ref, gpre_ref,
                  wa_ref, wb_ref, wo_ref, wr_ref, br_ref, cnt0_ref, x1_ref, h3_ref, rt_ref, cnt_ref, run_ref,
                  *, n_part):
    i = pl.program_id(0)
    tm = x_ref.shape[0]
    tp = tm // n_part
    parts = [slice(p * tp, (p + 1) * tp) for p in range(n_part)]
    rows = lambda ref, r: ref[...] if ref.shape[0] == 1 else ref[r, :]

    @pl.when(i == 0)
    def _():
        run_ref[...] = cnt0_ref[...]

    pa = [_dot(oa_ref[r, :], wa_ref[...]) for r in parts]
    pb = [_dot(ob_ref[r, :], wb_ref[...]) for r in parts]
    merged = [_sigmoid(gt_ref[r, :D].astype(F32)) * pa[p] + _sigmoid(gt_ref[r, D:].astype(F32)) * pb[p]
              for p, r in enumerate(parts)]
    z = [_dot(m.astype(BF16), wo_ref[...]) for m in merged]
    h2 = []
    for p, r in enumerate(parts):
        x1 = x_ref[r, :] + rows(g1_ref, r) * _rms(z[p], gpost_ref[...])
        x1_ref[r, :] = x1
        h2.append(_rms(x1, gpre_ref[...]) * (1.0 + rows(sc2_ref, r)) + rows(sh2_ref, r))
        _store_row_tiles(h3_ref, h2[p], p * tp)
    logits = [_dot(h, wr_ref[...]) + br_ref[...] for h in h2]

    lane = lax.broadcasted_iota(jnp.int32, (tp, N_EXPERTS), 1).astype(F32)
    sels, idxs, gates, chosen = [], [], [], []
    for p in range(n_part):
        work = logits[p]
        sel_p, idx_p, val_p = [], [], []
        for _ in range(TOP_K):
            mx = jnp.max(work, axis=1, keepdims=True)
            idx = jnp.min(jnp.where(work == mx, lane, float(N_EXPERTS)), axis=1, keepdims=True)
            sel = lane == idx
            work = jnp.where(sel, -jnp.inf, work)
            sel_p.append(sel)
            idx_p.append(idx)
            val_p.append(mx)
        ex = [jnp.exp(v - val_p[0]) for v in val_p]
        inv_den = 1.0 / (ex[0] + ex[1] + ex[2] + ex[3])
        sels.append(sel_p)
        idxs.append(idx_p)
        gates.append([e * inv_den for e in ex])
        chosen.append(sum(s.astype(F32) for s in sel_p))

    chosen_all = jnp.concatenate(chosen, axis=0) if n_part > 1 else chosen[0]
    before = (lax.broadcasted_iota(jnp.int32, (tm, tm), 0) > lax.broadcasted_iota(jnp.int32, (tm, tm), 1))
    prior = _dot(before.astype(BF16), chosen_all.astype(BF16)) + run_ref[...]
    run_ref[...] = run_ref[...] + jnp.sum(chosen_all, axis=0, keepdims=True)
    cnt_ref[...] = run_ref[...]

    rlane = lax.broadcasted_iota(jnp.int32, (tp, LANES), 1)
    for p, r in enumerate(parts):
        rt = jnp.zeros((tp, LANES), F32)
        for kk in range(TOP_K):
            rank = jnp.sum(jnp.where(sels[p][kk], prior[r, :], 0.0), axis=1, keepdims=True)
            rt = jnp.where(rlane == RT_E + kk, idxs[p][kk], rt)
            rt = jnp.where(rlane == RT_RANK + kk, rank, rt)
            rt = jnp.where(rlane == RT_GATE + kk, gates[p][kk], rt)
        rt_ref[r, :] = rt


def _merge(oa, ob, gt, x2d, m3d, tiles_per_mod, g_post, g_pre, wa, wb, wo, wr, br, cnt0, tm):
    R = x2d.shape[0]
    r = m3d.shape[1]
    mod = lambda col: pl.BlockSpec((None, r, D), lambda i: (i // tiles_per_mod, 0, col))
    row = lambda w: pl.BlockSpec((tm, w), lambda i: (i, 0))
    full = lambda a, b: pl.BlockSpec((a, b), lambda i: (0, 0))
    return pl.pallas_call(
        functools.partial(_merge_kernel, n_part=4 if tm >= 512 else 1),
        grid=(R // tm,),
        in_specs=[row(W_MIX), row(W_MIX), row(2 * D), row(D), mod(2), mod(4), mod(3),
                  full(1, D), full(1, D), full(W_MIX, D), full(W_MIX, D), full(D, D),
                  full(D, N_EXPERTS), full(1, N_EXPERTS), full(1, N_EXPERTS)],
        out_specs=[row(D), pl.BlockSpec((tm * SUB, LANES), lambda i: (i, 0)), row(LANES), full(1, N_EXPERTS)],
        out_shape=[jax.ShapeDtypeStruct((R, D), F32),
                   jax.ShapeDtypeStruct((R * SUB, LANES), F32),
                   jax.ShapeDtypeStruct((R, LANES), F32),
                   jax.ShapeDtypeStruct((1, N_EXPERTS), F32)],
        scratch_shapes=[pltpu.VMEM((1, N_EXPERTS), F32)],
        compiler_params=_cp(("arbitrary",)),
        name="merge",
    )(oa, ob, gt, x2d, m3d, m3d, m3d, g_post.reshape(1, D), g_pre.reshape(1, D), wa, wb, wo, wr,
      br.reshape(1, N_EXPERTS), cnt0)


def _row_copy(src, s_row, dst, d_row, sem):
    return pltpu.make_async_copy(src.at[pl.ds(pl.multiple_of(s_row, SUB), SUB)],
                                 dst.at[pl.ds(pl.multiple_of(d_row, SUB), SUB)], sem)


def _dispatch_kernel(lo_ref, hi_ref, nu_ref, da_ref, db_ref, ha_ref, hb_ref, x_hbm, zero_ref, sem):
    i = pl.program_id(0)
    last = pl.num_programs(0) - 1
    unroll = 8

    def scatter_rows(d_ref, h_ref):
        n = h_ref.shape[0] // SUB

        def issue(j, carry):
            for u in range(unroll):
                t = j * unroll + u
                for kk in range(TOP_K):
                    _row_copy(h_ref, t * SUB, x_hbm, d_ref[t * TOP_K + kk], sem).start(priority=kk % 2)
            return carry

        lax.fori_loop(0, n // unroll, issue, 0)
        for _ in range(TOP_K):
            pltpu.make_async_copy(h_ref, x_hbm.at[pl.ds(0, n * SUB)], sem).wait()

    @pl.when(i < last)
    def _():
        scatter_rows(da_ref, ha_ref)

    @pl.when(i == last)
    def _():
        scatter_rows(db_ref, hb_ref)
        zero_ref[...] = jnp.zeros_like(zero_ref)

        def per_expert(e, carry):
            zrow = lambda r: _row_copy(zero_ref, 0, x_hbm, r * SUB, sem)
            lax.fori_loop(lo_ref[e], hi_ref[e], lambda r, c: (zrow(r).start(), c)[1], 0)
            lax.fori_loop(lo_ref[e], hi_ref[e], lambda r, c: (zrow(r).wait(), c)[1], 0)
            return carry

        lax.fori_loop(0, N_EXPERTS, per_expert, 0)

        blk = zero_ref.shape[0]
        zblk = lambda b: pltpu.make_async_copy(zero_ref, x_hbm.at[pl.ds(pl.multiple_of(b * blk, blk), blk)], sem)
        lax.fori_loop(nu_ref[0], x_hbm.shape[0] // blk, lambda b, c: (zblk(b).start(), c)[1], 0)
        lax.fori_loop(nu_ref[0], x_hbm.shape[0] // blk, lambda b, c: (zblk(b).wait(), c)[1], 0)


def _dispatch(h3_a, h3_b, dest8_a, dest8_b, pad_lo, pad_hi, n_used, n_rows):
    tm = DISPATCH_ROWS
    n_a = h3_a.shape[0] // (tm * SUB)
    clamp = lambda i: jnp.minimum(i, n_a - 1)
    grid_spec = pltpu.PrefetchScalarGridSpec(
        num_scalar_prefetch=3,
        grid=(n_a + 1,),
        in_specs=[pl.BlockSpec((tm * TOP_K,), lambda i, *_: (clamp(i),), memory_space=pltpu.SMEM),
                  pl.BlockSpec(dest8_b.shape, lambda i, *_: (0,), memory_space=pltpu.SMEM),
                  pl.BlockSpec((tm * SUB, LANES), lambda i, *_: (clamp(i), 0)),
                  pl.BlockSpec(h3_b.shape, lambda i, *_: (0, 0))],
        out_specs=pl.BlockSpec(memory_space=pl.ANY),
        scratch_shapes=[pltpu.VMEM((MOE_BLOCK * SUB, LANES), F32), pltpu.SemaphoreType.DMA],
    )
    return pl.pallas_call(
        _dispatch_kernel,
        grid_spec=grid_spec,
        out_shape=jax.ShapeDtypeStruct((n_rows * SUB, LANES), F32),
        compiler_params=_cp(("arbitrary",)),
        name="dispatch",
    )(pad_lo, pad_hi, n_used, dest8_a, dest8_b, h3_a, h3_b)


def _expert_kernel(be_ref, nu_ref, nxt_ref, x_ref, wgu_hbm, bgu_ref, wd_hbm, bd_ref, y_ref,
                   wgu_f, wd_f, wgu_b, wd_b, slot_ref, sem_gu, sem_d):
    i = pl.program_id(0)

    def fetch(e, slot):
        return (pltpu.make_async_copy(wgu_hbm.at[e], wgu_f.at[slot], sem_gu.at[slot]),
                pltpu.make_async_copy(wd_hbm.at[e], wd_f.at[slot], sem_d.at[slot]))

    @pl.when(i == 0)
    def _():
        slot_ref[0] = 0
        for cp in fetch(be_ref[0], 0):
            cp.start()

    @pl.when(i < nu_ref[0])
    def _():
        @pl.when((i == 0) | (be_ref[i] != be_ref[jnp.maximum(i - 1, 0)]))
        def _():
            slot = slot_ref[0]
            for cp in fetch(be_ref[i], slot):
                cp.wait()
            wgu_b[...] = wgu_f[slot].astype(BF16)
            wd_b[...] = wd_f[slot].astype(BF16)

            @pl.when(nxt_ref[i] >= 0)
            def _():
                for cp in fetch(nxt_ref[i], 1 - slot):
                    cp.start()
            slot_ref[0] = 1 - slot

        gu = _dot(_load_row_tiles(x_ref).astype(BF16), wgu_b[...]) + bgu_ref[...]
        glu = jnp.minimum(gu[:, :D], SWIGLU_LIMIT)
        lin = jnp.clip(gu[:, D:], -SWIGLU_LIMIT, SWIGLU_LIMIT)
        act = (lin + 1.0) * glu * _sigmoid(SWIGLU_ALPHA * glu)
        _store_row_tiles(y_ref, _dot(act.astype(BF16), wd_b[...]) + bd_ref[...])

    @pl.when(i >= nu_ref[0])
    def _():
        y_ref[...] = jnp.zeros_like(y_ref)


def _experts(x3, block_expert, n_used, next_expert, w_gate_up, b_gate_up, w_down, b_down):
    bm = MOE_BLOCK
    n_blocks = x3.shape[0] // (bm * SUB)
    used = lambda i, nu: jnp.minimum(i, nu[0] - 1)
    anyspec = pl.BlockSpec(memory_space=pl.ANY)
    grid_spec = pltpu.PrefetchScalarGridSpec(
        num_scalar_prefetch=3,
        grid=(n_blocks,),
        in_specs=[pl.BlockSpec((bm * SUB, LANES), lambda i, be, nu, nx: (used(i, nu), 0)),
                  anyspec,
                  pl.BlockSpec((None, 1, 2 * D), lambda i, be, nu, nx: (be[used(i, nu)], 0, 0)),
                  anyspec,
                  pl.BlockSpec((None, 1, D), lambda i, be, nu, nx: (be[used(i, nu)], 0, 0))],
        out_specs=pl.BlockSpec((bm * SUB, LANES), lambda i, be, nu, nx: (i, 0)),
        scratch_shapes=[pltpu.VMEM((2, D, 2 * D), F32), pltpu.VMEM((2, D, D), F32),
                        pltpu.VMEM((D, 2 * D), BF16), pltpu.VMEM((D, D), BF16),
                        pltpu.SMEM((1,), jnp.int32),
                        pltpu.SemaphoreType.DMA((2,)), pltpu.SemaphoreType.DMA((2,))],
    )
    return pl.pallas_call(
        _expert_kernel,
        grid_spec=grid_spec,
        out_shape=jax.ShapeDtypeStruct(x3.shape, F32),
        compiler_params=_cp(("arbitrary",)),
        name="experts",
    )(block_expert, n_used, next_expert, x3, w_gate_up, b_gate_up.reshape(N_EXPERTS, 1, 2 * D), w_down,
      b_down.reshape(N_EXPERTS, 1, D))


def _final_kernel(dcur_ref, dnext_ref, x1_ref, g2_ref, gpost_ref, rt_ref, y_hbm, o_ref, ybuf, sem):
    i = pl.program_id(0)
    n = pl.num_programs(0)
    tm = x1_ref.shape[0]
    slot = i % 2
    unroll = 8

    def issue(d_ref, s):
        def body(j, carry):
            for u in range(unroll):
                t = j * unroll + u
                for kk in range(TOP_K):
                    _row_copy(y_hbm, d_ref[t * TOP_K + kk], ybuf.at[s, kk], t * SUB, sem.at[s]).start(priority=kk % 2)
            return carry
        lax.fori_loop(0, tm // unroll, body, 0)

    @pl.when(i == 0)
    def _():
        issue(dcur_ref, 0)

    @pl.when(i + 1 < n)
    def _():
        issue(dnext_ref, 1 - slot)

    for kk in range(TOP_K):
        pltpu.make_async_copy(y_hbm.at[pl.ds(0, tm * SUB)], ybuf.at[slot, kk], sem.at[slot]).wait()

    y = jnp.zeros((tm, D), F32)
    for kk in range(TOP_K):
        y = y + rt_ref[:, RT_GATE + kk:RT_GATE + kk + 1] * _load_row_tiles(ybuf.at[slot, kk])
    o_ref[...] = x1_ref[...] + g2_ref[...] * _rms(y, gpost_ref[...])


def _final(x1, m3d, tiles_per_mod, g_post, route, dest_flat, y3, row0, tm):
    R = x1.shape[0]
    r = m3d.shape[1]
    off = row0 // tm
    nsteps = R // tm
    row = pl.BlockSpec((tm, D), lambda i: (i, 0))
    dspec = lambda nxt: pl.BlockSpec((tm * TOP_K,), lambda i: (jnp.minimum(i + nxt, nsteps - 1) + off,),
                                     memory_space=pltpu.SMEM)
    return pl.pallas_call(
        _final_kernel,
        grid=(nsteps,),
        in_specs=[dspec(0), dspec(1), row,
                  pl.BlockSpec((None, r, D), lambda i: (i // tiles_per_mod, 0, 5)),
                  pl.BlockSpec((1, D), lambda i: (0, 0)),
                  pl.BlockSpec((tm, LANES), lambda i: (i, 0)),
                  pl.BlockSpec(memory_space=pl.ANY)],
        out_specs=row,
        out_shape=jax.ShapeDtypeStruct((R, D), F32),
        scratch_shapes=[pltpu.VMEM((2, TOP_K, tm * SUB, LANES), F32), pltpu.SemaphoreType.DMA((2,))],
        compiler_params=_cp(("arbitrary",)),
        name="final",
    )(dest_flat, dest_flat, x1, m3d, g_post.reshape(1, D), route, y3)


def kernel(x_prompt, x_sample, cache_k, cache_v, state_hgrn, page_table, c_prompt, c_sample, w_mod, b_mod, g_pre_mix, g_post_mix, g_pre_ffn, g_post_ffn, w_in, hgrn_lb, g_hgrn, lam_q1, lam_k1, lam_q2, lam_k2, g_subln, w_proj_a, w_proj_b, w_out, w_router, b_router, w_gate_up, b_gate_up, w_down, b_down):
    B, L, _ = x_prompt.shape
    NS = x_sample.shape[0]
    T = B * L
    T_all = T + NS
    l = 0

    lb = jnp.cumsum(jax.nn.softmax(hgrn_lb.astype(F32), axis=0), axis=0)[l]
    lam = (jnp.exp(jnp.sum(lam_q1[l] * lam_k1[l])) - jnp.exp(jnp.sum(lam_q2[l] * lam_k2[l])) + LAM_INIT).astype(F32)
    slopes = jnp.exp2(-8.0 * jnp.arange(1, H + 1, dtype=F32) / H)

    m_all = _modulation(jnp.concatenate([c_prompt, c_sample], axis=0), w_mod[l], b_mod[l])
    m_p = m_all[:B].reshape(B, 1, 6 * D)
    m_s = m_all[B:].reshape(1, NS, 6 * D)

    w_in_b = w_in[l].astype(BF16)
    tm = 512
    tpm = L // tm
    xp = x_prompt.reshape(T, D)
    xs = x_sample.reshape(NS, D)
    merge_w = (g_post_mix[l], g_pre_ffn[l], w_proj_a[l].astype(BF16), w_proj_b[l].astype(BF16),
               w_out[l].astype(BF16), w_router[l], b_router[l])

    aq_p, af_p, ai_p, ag_p, bq_p, k_p, v_p, gt_p, k4_p, v4_p = _inproj(xp, m_p, L // INPROJ_ROWS, g_pre_mix[l], w_in_b,
                                                                       INPROJ_ROWS)
    seq = lambda a: a.reshape(B, L, W_MIX)
    oa_p, s_p = _hgrn_prompt(seq(aq_p), seq(af_p), seq(ai_p), seq(ag_p), lb, g_hgrn[l])
    ob_p = _attn_prompt(bq_p, k_p, v_p, lam, slopes, g_subln[l], B, L)
    x1_p, h3_p, rt_p, cnt = _merge(oa_p.reshape(T, W_MIX), ob_p, gt_p, xp, m_p, L // MERGE_ROWS, *merge_w,
                                   jnp.zeros((1, N_EXPERTS), F32), MERGE_ROWS)

    aq_s, af_s, ai_s, ag_s, bq_s, _, _, gt_s, k4_s, v4_s = _inproj(xs, m_s, 1, g_pre_mix[l], w_in_b, NS)
    hg_s = jnp.concatenate([a.astype(F32) for a in (aq_s, af_s, ai_s, ag_s)], axis=1)
    oa_s, s_s = _hgrn_step(hg_s, state_hgrn[l], lb, g_hgrn[l])
    ob_s = _attn_step(bq_s.astype(F32), k4_s.reshape(NS, W_MIX), v4_s.reshape(NS, W_MIX), cache_k[l], cache_v[l],
                      page_table, lam, slopes, g_subln[l])
    x1_s, h3_s, rt_s, cnt = _merge(oa_s.astype(BF16), ob_s.astype(BF16), gt_s, xs, m_s, 1, *merge_w, cnt, NS)

    bm = MOE_BLOCK
    counts = cnt.reshape(N_EXPERTS).astype(jnp.int32)
    padded = (counts + bm - 1) // bm * bm
    pad_end = jnp.cumsum(padded)
    pad_start = pad_end - padded
    n_blocks = -(-T_all * TOP_K // bm) + N_EXPERTS
    route = jnp.concatenate([rt_p[:, :RT_GATE], rt_s[:, :RT_GATE]], axis=0)
    e_idx = route[:, RT_E:RT_E + TOP_K].astype(jnp.int32)
    rank = route[:, RT_RANK:RT_RANK + TOP_K].astype(jnp.int32)
    e_iota = jnp.arange(N_EXPERTS, dtype=jnp.int32)
    dest = rank + jnp.sum(jnp.where(e_idx[..., None] == e_iota, pad_start, 0), axis=-1)
    dest_flat = dest.reshape(-1) * SUB
    block_expert = jnp.minimum(
        jnp.sum((jnp.arange(n_blocks, dtype=jnp.int32)[:, None] * bm >= pad_end[None, :]).astype(jnp.int32), axis=1),
        N_EXPERTS - 1)
    n_used = pad_end[-1:] // bm
    later = lax.cummin(jnp.where(counts > 0, jnp.arange(N_EXPERTS, dtype=jnp.int32), N_EXPERTS), reverse=True)
    later = jnp.concatenate([later[1:], jnp.full((1,), N_EXPERTS, jnp.int32)])
    next_expert = jnp.where(later < N_EXPERTS, later, -1)[block_expert]

    x3 = _dispatch(h3_p, h3_s, dest_flat[:T * TOP_K], dest_flat[T * TOP_K:], pad_start + counts, pad_end, n_used,
                   n_blocks * bm)
    y3 = _experts(x3, block_expert, n_used, next_expert, w_gate_up[l], b_gate_up[l], w_down[l], b_down[l])
    y_p = _final(x1_p, m_p, tpm, g_post_ffn[l], rt_p, dest_flat, y3, 0, tm)
    y_s = _final(x1_s, m_s, 1, g_post_ffn[l], rt_s, dest_flat, y3, T, NS)

    return (y_p.reshape(B, L, D), y_s.reshape(NS, 1, D),
            k4_p.reshape(1, B, L, H, DH), v4_p.reshape(1, B, L, H, DH), s_p.reshape(1, B, H, DH, DH),
            k4_s.reshape(1, NS, 1, H, DH), v4_s.reshape(1, NS, 1, H, DH), s_s.reshape(1, NS, H, DH, DH))
```
